```python
import math
import jax, jax.numpy as jnp
from jax import lax
import numpy as np

D_MODEL = 1024
BATCH = 4
SEQ = 8192
DEPTH = 2

CHUNK = 64
Q_BLOCK = 128
N_BUCKETS = 32
MAX_DISTANCE = 128
EPS = 1e-6
A_HEADS = D_MODEL // 256
A_QK_DIM = 64
A_V_DIM = 2 * A_QK_DIM
A_WIDTH = A_HEADS * A_V_DIM
CONV_CH = D_MODEL // 2
CONV_WIDTH = 31
EVEN_IN = 3 * A_WIDTH + 2 * CONV_CH
GLA_HEADS = 4
GLA_KEY = D_MODEL // 2
GLA_VAL = D_MODEL
GLA_DK = GLA_KEY // GLA_HEADS
GLA_DV = GLA_VAL // GLA_HEADS
GATE_RANK = 16
GATE_TAU = 16.0
ODD_IN = 2 * GLA_KEY + 2 * GLA_VAL + GATE_RANK
PEER_HEADS = 8
N_KEYS = 128
N_EXPERTS = N_KEYS * N_KEYS
PEER_TOPK = 16
PEER_KEY_DIM = 256
PEER_HALF = PEER_KEY_DIM // 2
TOKEN_BLOCK = 128

kernel_name = "hybrid_diffattn_conformer_gla_peer"


def rmsnorm(x, g):
    xf = x.astype(jnp.float32)
    y = xf * lax.rsqrt(jnp.mean(xf * xf, axis=-1, keepdims=True) + EPS)
    return (y * g.astype(jnp.float32)).astype(x.dtype)


def layernorm(x, g, b):
    xf = x.astype(jnp.float32)
    mu = jnp.mean(xf, axis=-1, keepdims=True)
    var = jnp.mean(jnp.square(xf - mu), axis=-1, keepdims=True)
    y = (xf - mu) * lax.rsqrt(var + EPS)
    return (y * g.astype(jnp.float32) + b.astype(jnp.float32)).astype(x.dtype)


def t5_bucket(rel):
    nb = N_BUCKETS // 2
    max_exact = nb // 2
    ret = jnp.where(rel > 0, nb, 0)
    n = jnp.abs(rel)
    nf = jnp.maximum(n, 1).astype(jnp.float32)
    large = max_exact + (jnp.log(nf / max_exact) / math.log(MAX_DISTANCE / max_exact)
                         * (nb - max_exact)).astype(jnp.int32)
    large = jnp.minimum(large, nb - 1)
    return ret + jnp.where(n < max_exact, n, large)


def diff_attention(q1, q2, k1, k2, v, lam, rel_bias):
    S = q1.shape[2]
    scale = A_QK_DIM ** -0.5
    outs = []
    for i in range(S // Q_BLOCK):
        q_lo = i * Q_BLOCK
        kv_len = q_lo + Q_BLOCK
        qpos = jnp.arange(q_lo, kv_len)
        kpos = jnp.arange(kv_len)
        mask = (kpos[None, :] // CHUNK) <= (qpos[:, None] // CHUNK)
        bias = rel_bias[t5_bucket(kpos[None, :] - qpos[:, None])]
        bias = jnp.transpose(bias, (2, 0, 1))[None]

        def probs(q, k):
            logits = jnp.einsum('bhqd,bhkd->bhqk', q[:, :, q_lo:kv_len], k[:, :, :kv_len]) * scale + bias
            return jax.nn.softmax(jnp.where(mask, logits, -jnp.inf), axis=-1)

        p = probs(q1, k1) - lam * probs(q2, k2)
        outs.append(jnp.einsum('bhqk,bhkd->bhqd', p, v[:, :, :kv_len]))
    return jnp.concatenate(outs, axis=2)


def causal_depthwise_conv(u, w, b):
    u_pad = jnp.pad(u, ((0, 0), (CONV_WIDTH - 1, 0), (0, 0)))
    y = lax.conv_general_dilated(u_pad, w[:, None, :].astype(u.dtype), window_strides=(1,), padding='VALID',
                                 dimension_numbers=('NWC', 'WIO', 'NWC'), feature_group_count=u.shape[-1])
    return y + b.astype(u.dtype)


def even_mixer(h, w_in, lam_q1, lam_k1, lam_q2, lam_k2, subln_g, conv_w, conv_b, conv_ln_g, conv_ln_b,
               w_out, rel_bias, layer_idx):
    B, S, _ = h.shape
    proj = h @ w_in
    q, k, v, glu = jnp.split(proj, [A_WIDTH, 2 * A_WIDTH, 3 * A_WIDTH], axis=-1)

    def heads(t, d):
        return t.reshape(B, S, A_HEADS, d).transpose(0, 2, 1, 3).astype(jnp.float32)

    q = heads(q, 2 * A_QK_DIM)
    k = heads(k, 2 * A_QK_DIM)
    v = heads(v, A_V_DIM)
    lam_init = 0.8 - 0.6 * math.exp(-0.3 * layer_idx)
    lam = (jnp.exp(jnp.sum(lam_q1.astype(jnp.float32) * lam_k1.astype(jnp.float32)))
           - jnp.exp(jnp.sum(lam_q2.astype(jnp.float32) * lam_k2.astype(jnp.float32))) + lam_init)
    o = diff_attention(q[..., :A_QK_DIM], q[..., A_QK_DIM:], k[..., :A_QK_DIM], k[..., A_QK_DIM:],
                       v, lam, rel_bias.astype(jnp.float32))
    o = o * lax.rsqrt(jnp.mean(o * o, axis=-1, keepdims=True) + EPS) * subln_g.astype(jnp.float32) * (1.0 - lam_init)
    a_out = o.transpose(0, 2, 1, 3).reshape(B, S, A_WIDTH).astype(h.dtype)

    c_val, c_gate = jnp.split(glu, 2, axis=-1)
    c = c_val * jax.nn.sigmoid(c_gate)
    c = causal_depthwise_conv(c, conv_w, conv_b)
    c = jax.nn.silu(layernorm(c, conv_ln_g, conv_ln_b))
    return jnp.concatenate([a_out, c], axis=-1) @ w_out


def gla_mixer(h, w_in, w_gate2, b_gate2, gla_norm_g, w_out):
    B, S, _ = h.shape
    nc = S // CHUNK
    proj = h @ w_in
    q, k, v, g, z = jnp.split(proj, [GLA_KEY, 2 * GLA_KEY, 2 * GLA_KEY + GLA_VAL,
                                     2 * GLA_KEY + 2 * GLA_VAL], axis=-1)
    log_a = jax.nn.log_sigmoid((z @ w_gate2 + b_gate2).astype(jnp.float32)) / GATE_TAU

    def chunks(t, d):
        return t.reshape(B, nc, CHUNK, GLA_HEADS, d).transpose(0, 3, 1, 2, 4).astype(jnp.float32)

    qc = chunks(q, GLA_DK) * (GLA_DK ** -0.5)
    kc = chunks(k, GLA_DK)
    vc = chunks(v, GLA_DV)
    cum = jnp.cumsum(chunks(log_a, GLA_DK), axis=3)
    k_dec = kc * jnp.exp(cum[:, :, :, -1:, :] - cum)
    chunk_decay = jnp.exp(cum[:, :, :, -1, :])

    def step(state, inp):
        q_i, k_i, v_i, d_i = inp
        state = state * d_i[..., None] + jnp.einsum('bhcd,bhce->bhde', k_i, v_i)
        return state, jnp.einsum('bhcd,bhde->bhce', q_i, state)

    xs = tuple(jnp.moveaxis(t, 2, 0) for t in (qc, k_dec, vc, chunk_decay))
    _, o = lax.scan(step, jnp.zeros((B, GLA_HEADS, GLA_DK, GLA_DV), jnp.float32), xs)
    o = o.transpose(1, 0, 3, 2, 4).reshape(B, S, GLA_HEADS, GLA_DV)
    o = o * lax.rsqrt(jnp.mean(o * o, axis=-1, keepdims=True) + EPS) * gla_norm_g.astype(jnp.float32)
    o = o.reshape(B, S, GLA_VAL) * jax.nn.silu(g.astype(jnp.float32))
    return o.astype(h.dtype) @ w_out


def peer(h, w_query, sub_keys, expert_u, expert_v):
    B, S, D = h.shape
    sk = sub_keys.astype(jnp.float32)

    def block(xb):
        t = xb.shape[0]
        q = (xb @ w_query).reshape(t, PEER_HEADS, 2, PEER_HALF).astype(jnp.float32)
        scores = jnp.einsum('thpd,hpnd->thpn', q, sk)
        s_val, s_idx = lax.top_k(scores, PEER_TOPK)
        cand = s_val[:, :, 0, :, None] + s_val[:, :, 1, None, :]
        c_val, c_idx = lax.top_k(cand.reshape(t, PEER_HEADS, PEER_TOPK * PEER_TOPK), PEER_TOPK)
        i1 = jnp.take_along_axis(s_idx[:, :, 0], c_idx // PEER_TOPK, axis=-1)
        i2 = jnp.take_along_axis(s_idx[:, :, 1], c_idx % PEER_TOPK, axis=-1)
        expert = i1 * N_KEYS + i2
        gate = jax.nn.softmax(c_val, axis=-1)
        act = jax.nn.gelu(jnp.einsum('thkd,td->thk', expert_u[expert], xb).astype(jnp.float32))
        coef = (gate * act).astype(xb.dtype)
        return jnp.einsum('thk,thkd->td', coef, expert_v[expert])

    out = lax.map(block, h.reshape(-1, TOKEN_BLOCK, D))
    return out.reshape(B, S, D)


def setup_inputs(seed: int = 0) -> dict:
    key = jax.random.key(seed)
    ks = iter(jax.random.split(key, 32))
    f32 = jnp.float32
    n_even = (DEPTH + 1) // 2
    n_odd = DEPTH // 2

    def nrm(shape, scale):
        return jax.random.normal(next(ks), shape, f32) * scale

    def gain(shape):
        return 1.0 + nrm(shape, 0.02)

    return {
        "x": nrm((BATCH, SEQ, D_MODEL), 1.0),
        "rel_bias": nrm((N_BUCKETS, A_HEADS), 0.1),
        "ln_mix": gain((DEPTH, D_MODEL)),
        "ln_ffn": gain((DEPTH, D_MODEL)),
        "even_w_in": nrm((n_even, D_MODEL, EVEN_IN), D_MODEL ** -0.5),
        "lam_q1": nrm((n_even, A_QK_DIM), 0.1),
        "lam_k1": nrm((n_even, A_QK_DIM), 0.1),
        "lam_q2": nrm((n_even, A_QK_DIM), 0.1),
        "lam_k2": nrm((n_even, A_QK_DIM), 0.1),
        "subln_g": gain((n_even, A_V_DIM)),
        "conv_w": nrm((n_even, CONV_WIDTH, CONV_CH), CONV_WIDTH ** -0.5),
        "conv_b": nrm((n_even, CONV_CH), 0.02),
        "conv_ln_g": gain((n_even, CONV_CH)),
        "conv_ln_b": nrm((n_even, CONV_CH), 0.02),
        "even_w_out": nrm((n_even, A_WIDTH + CONV_CH, D_MODEL), (A_WIDTH + CONV_CH) ** -0.5),
        "odd_w_in": nrm((n_odd, D_MODEL, ODD_IN), D_MODEL ** -0.5),
        "w_gate2": nrm((n_odd, GATE_RANK, GLA_KEY), GATE_RANK ** -0.5),
        "b_gate2": nrm((n_odd, GLA_KEY), 0.1),
        "gla_norm_g": gain((n_odd, GLA_DV)),
        "odd_w_out": nrm((n_odd, GLA_VAL, D_MODEL), GLA_VAL ** -0.5),
        "peer_w_query": nrm((DEPTH, D_MODEL, PEER_HEADS * PEER_KEY_DIM), D_MODEL ** -0.5),
        "peer_sub_keys": nrm((DEPTH, PEER_HEADS, 2, N_KEYS, PEER_HALF), PEER_HALF ** -0.5),
        "peer_u": nrm((DEPTH, N_EXPERTS, D_MODEL), D_MODEL ** -0.5),
        "peer_v": nrm((DEPTH, N_EXPERTS, D_MODEL), 0.25),
        "ln_final": gain((D_MODEL,)),
    }


def reference(x, rel_bias, ln_mix, ln_ffn, even_w_in, lam_q1, lam_k1, lam_q2, lam_k2, subln_g, conv_w,
              conv_b, conv_ln_g, conv_ln_b, even_w_out, odd_w_in, w_gate2, b_gate2, gla_norm_g, odd_w_out,
              peer_w_query, peer_sub_keys, peer_u, peer_v, ln_final):
    for l in range(DEPTH):
        hn = rmsnorm(x, ln_mix[l])
        i = l // 2
        if l % 2 == 0:
            x = x + even_mixer(hn, even_w_in[i], lam_q1[i], lam_k1[i], lam_q2[i], lam_k2[i], subln_g[i],
                               conv_w[i], conv_b[i], conv_ln_g[i], conv_ln_b[i], even_w_out[i], rel_bias, l)
        else:
            x = x + gla_mixer(hn, odd_w_in[i], w_gate2[i], b_gate2[i], gla_norm_g[i], odd_w_out[i])
        x = x + peer(rmsnorm(x, ln_ffn[l]), peer_w_query[l], peer_sub_keys[l], peer_u[l], peer_v[l])
    return rmsnorm(x, ln_final)
```

```python
import functools
import math

import jax
import jax.numpy as jnp
import numpy as np
from jax import lax
from jax.experimental import pallas as pl
from jax.experimental.pallas import tpu as pltpu

F32 = jnp.float32
BF16 = jnp.bfloat16

EPS = 1e-6
LANES = 128
VMEM_LIMIT = 56 * 1024 * 1024

CHUNK = 64
N_BUCKETS = 32
MAX_DISTANCE = 128
A_HEADS = 4
A_QK_DIM = 64
A_V_DIM = 128
CONV_WIDTH = 31
GLA_HEADS = 4
GATE_RANK = 16
GATE_TAU = 16.0
PEER_HEADS = 8
N_KEYS = 128
PEER_TOPK = 16
PEER_HALF = 128
NEG_BIG = -1e30


def _cparams(sem):
    return pltpu.CompilerParams(dimension_semantics=sem, vmem_limit_bytes=VMEM_LIMIT)


def _rms_normed(x, g):
    ms = jnp.mean(x * x, axis=-1, keepdims=True)
    return x * lax.rsqrt(ms + EPS) * g


def _norm_matmul_kernel(x_ref, g_ref, w_ref, o_ref):
    xn = _rms_normed(x_ref[...], g_ref[...]).astype(BF16)
    o_ref[...] = jnp.dot(xn, w_ref[...], preferred_element_type=F32).astype(o_ref.dtype)


def norm_matmul(x, g, w, *, tm=512, out_dtype=BF16):
    T, D = x.shape
    N = w.shape[1]
    return pl.pallas_call(
        _norm_matmul_kernel,
        out_shape=jax.ShapeDtypeStruct((T, N), out_dtype),
        grid=(T // tm,),
        in_specs=[
            pl.BlockSpec((tm, D), lambda i: (i, 0)),
            pl.BlockSpec((1, D), lambda i: (0, 0)),
            pl.BlockSpec((D, N), lambda i: (0, 0)),
        ],
        out_specs=pl.BlockSpec((tm, N), lambda i: (i, 0)),
        compiler_params=_cparams(("parallel",)),
        name="norm_matmul",
    )(x, g.reshape(1, D).astype(F32), w)


def _proj_residual_kernel(x_ref, a_ref, c_ref, wa_ref, wc_ref, o_ref):
    acc = jnp.dot(a_ref[...], wa_ref[...], preferred_element_type=F32)
    acc = acc + jnp.dot(c_ref[...], wc_ref[...], preferred_element_type=F32)
    o_ref[...] = x_ref[...] + acc


def proj_residual(x, a, c, wa, wc, *, a_col=0, c_col=0, tm=512):
    T, D = x.shape
    Ka, Kc = wa.shape[0], wc.shape[0]
    return pl.pallas_call(
        _proj_residual_kernel,
        out_shape=jax.ShapeDtypeStruct((T, D), F32),
        grid=(T // tm,),
        in_specs=[
            pl.BlockSpec((tm, D), lambda i: (i, 0)),
            pl.BlockSpec((tm, Ka), lambda i: (i, a_col)),
            pl.BlockSpec((tm, Kc), lambda i: (i, c_col)),
            pl.BlockSpec((Ka, D), lambda i: (0, 0)),
            pl.BlockSpec((Kc, D), lambda i: (0, 0)),
        ],
        out_specs=pl.BlockSpec((tm, D), lambda i: (i, 0)),
        compiler_params=_cparams(("parallel",)),
        name="proj_residual",
    )(x, a, c, wa, wc)


def _pair_candidates():
    return [(a, b) for a in range(PEER_TOPK) for b in range(PEER_TOPK) if (a + 1) * (b + 1) <= PEER_TOPK]


def _dup_bf16_bits(v):
    hi = pltpu.bitcast(v.astype(BF16).astype(F32), jnp.uint32)
    return hi | (hi >> 16)


def _pack_bf16_pairs(even, odd):
    be = pltpu.bitcast(even.astype(BF16).astype(F32), jnp.uint32)
    bo = pltpu.bitcast(odd.astype(BF16).astype(F32), jnp.uint32)
    return (be >> 16) | bo


def _route_kernel(x_ref, g_ref, wqT_ref, sk_ref, rank2_ref, e2_ref, r1_ref, e1z_ref,
                  xn_scr, s_scr, rank_scr, top_scr, r1top_scr, stat_scr):
    tm = x_ref.shape[0]
    n_groups = tm // LANES
    xn_scr[...] = _rms_normed(x_ref[...], g_ref[...]).astype(BF16)

    def score_body(h, carry):
        w_h = wqT_ref[pl.ds(pl.multiple_of(h * 2 * PEER_HALF, 2 * PEER_HALF), 2 * PEER_HALF), :]
        qT = lax.dot_general(w_h, xn_scr[...], (((1,), (1,)), ((), ())),
                             preferred_element_type=F32).astype(BF16)
        for p in range(2):
            s = jnp.dot(sk_ref[2 * h + p], qT[p * PEER_HALF:(p + 1) * PEER_HALF, :],
                        preferred_element_type=F32)
            for grp in range(n_groups):
                s_scr[2 * h + p, grp] = s[:, grp * LANES:(grp + 1) * LANES]
        return carry

    lax.fori_loop(0, PEER_HEADS, score_body, 0)

    def top_body(hp, carry):
        for grp in range(n_groups):
            work = s_scr[hp, grp]
            rank = jnp.full(work.shape, float(PEER_TOPK), F32)
            for k in range(PEER_TOPK):
                m = jnp.max(work, axis=0, keepdims=True)
                eq = work == m
                rank = jnp.where(eq, float(k), rank)
                work = jnp.where(eq, -jnp.inf, work)
                top_scr[hp, k, grp:grp + 1, :] = m
            rank_scr[hp, grp] = rank
        return carry

    lax.fori_loop(0, 2 * PEER_HEADS, top_body, 0)

    cands = _pair_candidates()

    def pair_body(h, carry):
        v1 = [top_scr[2 * h, a] for a in range(PEER_TOPK)]
        v2 = [top_scr[2 * h + 1, b] for b in range(PEER_TOPK)]
        sums = [v1[a] + v2[b] for (a, b) in cands]
        work = list(sums)
        tau = None
        for k in range(PEER_TOPK):
            tau = functools.reduce(jnp.maximum, work)
            if k + 1 < PEER_TOPK:
                work = [jnp.where(w == tau, -jnp.inf, w) for w in work]
        cmax = v1[0] + v2[0]
        z = jnp.zeros_like(tau)
        r1 = [jnp.zeros_like(tau) for _ in range(PEER_TOPK)]
        for (a, b), c in zip(cands, sums):
            sel = c >= tau
            z = z + jnp.where(sel, jnp.exp(c - cmax), 0.0)
            r1[a] = r1[a] + jnp.where(sel, 1.0, 0.0)
        for a in range(PEER_TOPK):
            r1top_scr[h, a] = r1[a]
        stat_scr[h, 0] = v1[0]
        stat_scr[h, 1] = v2[0]
        stat_scr[h, 2] = 1.0 / z
        return carry

    lax.fori_loop(0, PEER_HEADS, pair_body, 0)

    even_rows = pl.ds(0, N_KEYS // 2, stride=2)
    odd_rows = pl.ds(1, N_KEYS // 2, stride=2)
    def table_body(h, carry):
        for grp in range(n_groups):
            lanes = slice(grp * LANES, (grp + 1) * LANES)
            max1 = stat_scr[h, 0, grp:grp + 1, :]
            max2 = stat_scr[h, 1, grp:grp + 1, :]
            inv_z = stat_scr[h, 2, grp:grp + 1, :]
            rank1 = rank_scr[2 * h, grp]
            r1_key = jnp.zeros(rank1.shape, F32)
            for a in range(PEER_TOPK):
                r1_key = jnp.where(rank1 == float(a), r1top_scr[h, a, grp:grp + 1, :], r1_key)
            r1_ref[h, :, lanes] = _dup_bf16_bits(r1_key)
            e1z = jnp.exp(s_scr[2 * h, grp] - max1) * inv_z
            e1z_ref[h, :, lanes] = _dup_bf16_bits(e1z)
            rk_e = rank_scr[2 * h + 1, grp, even_rows, :]
            rk_o = rank_scr[2 * h + 1, grp, odd_rows, :]
            rank2_ref[h, :, lanes] = _pack_bf16_pairs(rk_e, rk_o)
            e2_e = jnp.exp(jnp.minimum(s_scr[2 * h + 1, grp, even_rows, :] - max2, 0.0))
            e2_o = jnp.exp(jnp.minimum(s_scr[2 * h + 1, grp, odd_rows, :] - max2, 0.0))
            e2_ref[h, :, lanes] = _pack_bf16_pairs(e2_e, e2_o)
        return carry

    lax.fori_loop(0, PEER_HEADS, table_body, 0)


def peer_route(x, g, wqT, sk, *, tm=512):
    T, D = x.shape
    H = PEER_HEADS
    n_groups = tm // LANES
    tab = lambda rows: jax.ShapeDtypeStruct((H, rows, T), jnp.uint32)
    tab_spec = lambda rows: pl.BlockSpec((H, rows, tm), lambda i: (0, 0, i))
    return pl.pallas_call(
        _route_kernel,
        out_shape=(tab(N_KEYS // 2), tab(N_KEYS // 2), tab(N_KEYS), tab(N_KEYS)),
        grid=(T // tm,),
        in_specs=[
            pl.BlockSpec((tm, D), lambda i: (i, 0)),
            pl.BlockSpec((1, D), lambda i: (0, 0)),
            pl.BlockSpec((H * 2 * PEER_HALF, D), lambda i: (0, 0)),
            pl.BlockSpec((2 * H, N_KEYS, PEER_HALF), lambda i: (0, 0, 0)),
        ],
        out_specs=(tab_spec(N_KEYS // 2), tab_spec(N_KEYS // 2), tab_spec(N_KEYS), tab_spec(N_KEYS)),
        scratch_shapes=[
            pltpu.VMEM((tm, D), BF16),
            pltpu.VMEM((2 * H, n_groups, N_KEYS, LANES), F32),
            pltpu.VMEM((2 * H, n_groups, N_KEYS, LANES), F32),
            pltpu.VMEM((2 * H, PEER_TOPK, n_groups, LANES), F32),
            pltpu.VMEM((H, PEER_TOPK, n_groups, LANES), F32),
            pltpu.VMEM((H, 3, n_groups, LANES), F32),
        ],
        compiler_params=_cparams(("parallel",)),
        name="peer_route",
    )(x, g.reshape(1, D).astype(F32), wqT, sk)


GELU_C0 = math.sqrt(2.0 / math.pi)
GELU_C1 = 0.044715


def _gelu_tanh(a):
    inner = GELU_C0 * (a + GELU_C1 * (a * a * a))
    return 0.5 * a * (1.0 + jnp.tanh(inner))


def _peer_main_kernel(x_ref, g_ref, rank2_ref, e2_ref, r1_ref, e1z_ref, u_ref, vt_ref, gf_ref, o_ref,
                      xn_scr, acc_scr, coef_scr, *, final_norm):
    j = pl.program_id(1)
    tm = x_ref.shape[0]
    n_groups = tm // LANES
    n_i1 = r1_ref.shape[1]
    rows = 16
    words = rows * jnp.dtype(BF16).itemsize // 4

    @pl.when(j == 0)
    def _():
        xn_scr[...] = _rms_normed(x_ref[...], g_ref[...]).astype(BF16)
        acc_scr[...] = jnp.zeros_like(acc_scr)

    act = lax.dot_general(u_ref[...], xn_scr[...], (((1,), (1,)), ((), ())), preferred_element_type=F32)

    for i1 in range(n_i1):
        for grp in range(n_groups):
            lanes = slice(grp * LANES, (grp + 1) * LANES)
            r1b, e1b = [], []
            for h in range(PEER_HEADS):
                r1row = jnp.broadcast_to(r1_ref[h, i1:i1 + 1, lanes], (words, LANES))
                e1row = jnp.broadcast_to(e1z_ref[h, i1:i1 + 1, lanes], (words, LANES))
                r1b.append(pltpu.bitcast(r1row, BF16))
                e1b.append(pltpu.bitcast(e1row, BF16))
            for blk in range(N_KEYS // rows):
                i2w = slice(blk * words, (blk + 1) * words)
                gate = None
                for h in range(PEER_HEADS):
                    sel = pltpu.bitcast(rank2_ref[h, i2w, lanes], BF16) < r1b[h]
                    term = jnp.where(sel, pltpu.bitcast(e2_ref[h, i2w, lanes], BF16), jnp.zeros((), BF16)) * e1b[h]
                    gate = term if gate is None else gate + term
                r0 = i1 * N_KEYS + blk * rows
                a = act[r0:r0 + rows, lanes].astype(BF16)
                coef_scr[r0:r0 + rows, lanes] = _gelu_tanh(a) * gate

    acc_scr[...] += jnp.dot(vt_ref[...], coef_scr[...], preferred_element_type=F32)

    @pl.when(j == pl.num_programs(1) - 1)
    def _():
        y = x_ref[...] + acc_scr[...].T
        if final_norm:
            y = _rms_normed(y, gf_ref[...])
        o_ref[...] = y


def peer_main(x, g, tables, u, vt, g_final, *, final_norm, tm=512, eb=1024):
    T, D = x.shape
    E = u.shape[0]
    H = PEER_HEADS
    rank2, e2, r1, e1z = tables
    n_i1 = eb // N_KEYS
    full_tab = pl.BlockSpec((H, N_KEYS // 2, tm), lambda i, j: (0, 0, i))
    i1_tab = pl.BlockSpec((H, n_i1, tm), lambda i, j: (0, j, i))
    return pl.pallas_call(
        functools.partial(_peer_main_kernel, final_norm=final_norm),
        out_shape=jax.ShapeDtypeStruct((T, D), F32),
        grid=(T // tm, E // eb),
        in_specs=[
            pl.BlockSpec((tm, D), lambda i, j: (i, 0)),
            pl.BlockSpec((1, D), lambda i, j: (0, 0)),
            full_tab, full_tab, i1_tab, i1_tab,
            pl.BlockSpec((eb, D), lambda i, j: (j, 0)),
            pl.BlockSpec((D, eb), lambda i, j: (0, j)),
            pl.BlockSpec((1, D), lambda i, j: (0, 0)),
        ],
        out_specs=pl.BlockSpec((tm, D), lambda i, j: (i, 0)),
        scratch_shapes=[
            pltpu.VMEM((tm, D), BF16),
            pltpu.VMEM((D, tm), F32),
            pltpu.VMEM((eb, tm), BF16),
        ],
        compiler_params=_cparams(("parallel", "arbitrary")),
        name="peer_main",
    )(x, g.reshape(1, D).astype(F32), rank2, e2, r1, e1z, u, vt, g_final.reshape(1, D).astype(F32))


def peer_layer(x, g, w_query, sub_keys, expert_u, expert_v, g_final, *, final_norm):
    wqT = w_query.T.astype(BF16)
    sk = sub_keys.reshape(2 * PEER_HEADS, N_KEYS, PEER_HALF).astype(BF16)
    tables = peer_route(x, g, wqT, sk)
    return peer_main(x, g, tables, expert_u.astype(BF16), expert_v.T.astype(BF16), g_final,
                     final_norm=final_norm)


ATTN_TILE = 256


def _attn_kernel(lam_ref, q_ref, k_ref, v_ref, bias_ref, gsub_ref, o_ref, m_scr, l_scr, acc_scr,
                 *, out_scale):
    i = pl.program_id(2)
    tq = q_ref.shape[1]
    tk = tq
    q = q_ref[0]
    lane = lax.broadcasted_iota(jnp.int32, q.shape, 1)
    scale = jnp.asarray(A_QK_DIM ** -0.5, BF16)
    zero = jnp.zeros((), BF16)
    qs = jnp.concatenate([jnp.where(lane < A_QK_DIM, q, zero) * scale,
                          jnp.where(lane >= A_QK_DIM, q, zero) * scale], axis=0)

    m_scr[...] = jnp.full(m_scr.shape, NEG_BIG, F32)
    l_scr[...] = jnp.zeros(l_scr.shape, F32)
    acc_scr[...] = jnp.zeros(acc_scr.shape, F32)

    def step(j, bias):
        start = pl.multiple_of(j * tk, tk)
        k = k_ref[0, pl.ds(start, tk), :]
        v = v_ref[0, pl.ds(start, tk), :]
        s = lax.dot_general(qs, k, (((1,), (1,)), ((), ())), preferred_element_type=F32)
        if bias is not None:
            s = s + jnp.concatenate([bias, bias], axis=0)
        m_prev = m_scr[...]
        m_new = jnp.maximum(m_prev, jnp.max(s, axis=1, keepdims=True))
        alpha = jnp.exp(m_prev - m_new)
        p = jnp.exp(s - m_new)
        l_scr[...] = alpha * l_scr[...] + jnp.sum(p, axis=1, keepdims=True)
        acc_scr[...] = alpha * acc_scr[...] + jnp.dot(p.astype(BF16), v, preferred_element_type=F32)
        m_scr[...] = m_new

    def far_body(j, carry):
        step(j, None)
        return carry

    lax.fori_loop(0, i - 1, far_body, 0)

    @pl.when(i >= 1)
    def _():
        step(i - 1, bias_ref[0, 0])

    step(i, bias_ref[0, 1])

    o = acc_scr[...] / l_scr[...]
    o = o[:tq] - lam_ref[0] * o[tq:]
    o = o * lax.rsqrt(jnp.mean(o * o, axis=-1, keepdims=True) + EPS) * gsub_ref[...] * out_scale
    o_ref[0] = o.astype(o_ref.dtype)


def _t5_bucket(rel):
    nb = N_BUCKETS // 2
    max_exact = nb // 2
    ret = jnp.where(rel > 0, nb, 0)
    n = jnp.abs(rel)
    nf = jnp.maximum(n, 1).astype(F32)
    large = max_exact + (jnp.log(nf / max_exact) / math.log(MAX_DISTANCE / max_exact)
                         * (nb - max_exact)).astype(jnp.int32)
    large = jnp.minimum(large, nb - 1)
    return ret + jnp.where(n < max_exact, n, large)


def _attn_bias_tiles(rel_bias, tile):
    r = jnp.arange(tile)[:, None]
    c = jnp.arange(tile)[None, :]
    rb = rel_bias.astype(F32)
    far = rb[_t5_bucket(jnp.asarray(-(2 * tile)))]
    prev = rb[_t5_bucket(c - r - tile)] - far
    diag = rb[_t5_bucket(c - r)] - far
    diag = jnp.where(((c // CHUNK) <= (r // CHUNK))[..., None], diag, NEG_BIG)
    return jnp.transpose(jnp.stack([prev, diag], axis=0), (3, 0, 1, 2))


def diff_attention(proj, lam, bias_tiles, subln_g, *, out_scale):
    B, S, _ = proj.shape
    H = A_HEADS
    tq = ATTN_TILE
    return pl.pallas_call(
        functools.partial(_attn_kernel, out_scale=out_scale),
        out_shape=jax.ShapeDtypeStruct((B, S, H * A_V_DIM), BF16),
        grid=(B, H, S // tq),
        in_specs=[
            pl.BlockSpec(memory_space=pltpu.SMEM),
            pl.BlockSpec((1, tq, A_V_DIM), lambda b, h, i: (b, i, h)),
            pl.BlockSpec((1, S, A_V_DIM), lambda b, h, i: (b, 0, H + h)),
            pl.BlockSpec((1, S, A_V_DIM), lambda b, h, i: (b, 0, 2 * H + h)),
            pl.BlockSpec((1, 2, tq, tq), lambda b, h, i: (h, 0, 0, 0)),
            pl.BlockSpec((1, A_V_DIM), lambda b, h, i: (0, 0)),
        ],
        out_specs=pl.BlockSpec((1, tq, A_V_DIM), lambda b, h, i: (b, i, h)),
        scratch_shapes=[
            pltpu.VMEM((2 * tq, 1), F32),
            pltpu.VMEM((2 * tq, 1), F32),
            pltpu.VMEM((2 * tq, A_V_DIM), F32),
        ],
        compiler_params=_cparams(("parallel", "parallel", "arbitrary")),
        name="diff_attention",
    )(lam.reshape(1).astype(F32), proj, proj, proj, bias_tiles, subln_g.reshape(1, A_V_DIM).astype(F32))


CONV_HALO = 32
CONV_ROWS = 64


def _conv_kernel(val_ref, gate_ref, hval_ref, hgate_ref, w_ref, b_ref, g_ref, beta_ref, o_ref, buf_scr):
    i = pl.program_id(1)
    ts = val_ref.shape[1]

    def glu(v, gt):
        return v.astype(F32) * jax.nn.sigmoid(gt.astype(F32))

    halo = glu(hval_ref[0], hgate_ref[0])
    buf_scr[0:CONV_HALO, :] = jnp.where(i > 0, halo, 0.0)
    buf_scr[CONV_HALO:, :] = glu(val_ref[0], gate_ref[0])

    first = CONV_HALO - (CONV_WIDTH - 1)
    for r in range(ts // CONV_ROWS):
        acc = None
        for w in range(CONV_WIDTH):
            lo = r * CONV_ROWS + first + w
            term = buf_scr[lo:lo + CONV_ROWS, :] * w_ref[w:w + 1, :]
            acc = term if acc is None else acc + term
        y = acc + b_ref[...]
        mu = jnp.mean(y, axis=-1, keepdims=True)
        d = y - mu
        var = jnp.mean(d * d, axis=-1, keepdims=True)
        y = d * lax.rsqrt(var + EPS) * g_ref[...] + beta_ref[...]
        y = y * jax.nn.sigmoid(y)
        o_ref[0, r * CONV_ROWS:(r + 1) * CONV_ROWS, :] = y.astype(o_ref.dtype)


def conv_module(proj, conv_w, conv_b, ln_g, ln_b, *, ts=512):
    B, S, _ = proj.shape
    C = conv_w.shape[1]
    hb = ts // CONV_HALO
    row = lambda a: a.reshape(1, C).astype(F32)
    cur = lambda col: pl.BlockSpec((1, ts, C), lambda b, i: (b, i, col))
    halo = lambda col: pl.BlockSpec((1, CONV_HALO, C), lambda b, i: (b, jnp.maximum(i * hb - 1, 0), col))
    vec = pl.BlockSpec((1, C), lambda b, i: (0, 0))
    return pl.pallas_call(
        _conv_kernel,
        out_shape=jax.ShapeDtypeStruct((B, S, C), BF16),
        grid=(B, S // ts),
        in_specs=[cur(3), cur(4), halo(3), halo(4),
                  pl.BlockSpec((CONV_WIDTH + 1, C), lambda b, i: (0, 0)), vec, vec, vec],
        out_specs=pl.BlockSpec((1, ts, C), lambda b, i: (b, i, 0)),
        scratch_shapes=[pltpu.VMEM((ts + CONV_HALO, C), F32)],
        compiler_params=_cparams(("parallel", "parallel")),
        name="conv_module",
    )(proj, proj, proj, proj, jnp.pad(conv_w.astype(F32), ((0, 1), (0, 0))), row(conv_b), row(ln_g), row(ln_b))


GLA_DK = 128
GLA_DV = 256
GLA_Z_COLS = 128


def _split_bf16(x):
    hi = x.astype(BF16)
    lo = (x - hi.astype(F32)).astype(BF16)
    return hi, lo


def _gla_kernel(q_ref, k_ref, v_ref, g_ref, z_ref, wg_ref, bg_ref, gn_ref, o_ref, state_scr):
    ts = q_ref.shape[0]
    n_chunks = ts // CHUNK

    @pl.when(pl.program_id(1) == 0)
    def _():
        state_scr[...] = jnp.zeros(state_scr.shape, F32)

    zz = jnp.dot(z_ref[...], wg_ref[...], preferred_element_type=F32) + bg_ref[...]
    log_a = (jnp.minimum(zz, 0.0) - jnp.log1p(jnp.exp(-jnp.abs(zz)))) / GATE_TAU

    row = lax.broadcasted_iota(jnp.int32, (CHUNK, CHUNK), 0)
    col = lax.broadcasted_iota(jnp.int32, (CHUNK, CHUNK), 1)
    tri = jnp.where(col <= row, 1.0, 0.0).astype(BF16)
    ones = jnp.ones((CHUNK, LANES), BF16)
    tn = (((0,), (0,)), ((), ()))
    q_scale = GLA_DK ** -0.5

    for c in range(n_chunks):
        rows = slice(c * CHUNK, (c + 1) * CHUNK)
        la_hi, la_lo = _split_bf16(log_a[rows, :])
        cum = (jnp.dot(tri, la_hi, preferred_element_type=F32)
               + jnp.dot(tri, la_lo, preferred_element_type=F32))
        k_dec = k_ref[rows, :].astype(F32) * jnp.exp(cum[CHUNK - 1:CHUNK, :] - cum)
        for h in range(GLA_HEADS):
            kc = slice(h * GLA_DK, (h + 1) * GLA_DK)
            vc = slice(h * GLA_DV, (h + 1) * GLA_DV)
            tot = (lax.dot_general(la_hi[:, kc], ones, tn, preferred_element_type=F32)
                   + lax.dot_general(la_lo[:, kc], ones, tn, preferred_element_type=F32))
            decay = jnp.exp(tot)
            decay = jnp.concatenate([decay] * (GLA_DV // LANES), axis=1)
            kv = lax.dot_general(k_dec[:, kc].astype(BF16), v_ref[rows, vc], tn,
                                 preferred_element_type=F32)
            state = state_scr[h] * decay + kv
            state_scr[h] = state
            o = jnp.dot(q_ref[rows, kc], state.astype(BF16), preferred_element_type=F32) * q_scale
            o = o * lax.rsqrt(jnp.mean(o * o, axis=-1, keepdims=True) + EPS) * gn_ref[...]
            gt = g_ref[rows, vc].astype(F32)
            o_ref[rows, vc] = (o * (gt * jax.nn.sigmoid(gt))).astype(o_ref.dtype)


def gla_scan(proj, w_gate, b_gate, norm_g, *, batch, ts=512):
    T = proj.shape[0]
    S = T // batch
    nb = S // ts
    KD = GLA_HEADS * GLA_DK
    VD = GLA_HEADS * GLA_DV
    rows = lambda width, col: pl.BlockSpec((ts, width), lambda b, i: (b * nb + i, col))
    return pl.pallas_call(
        _gla_kernel,
        out_shape=jax.ShapeDtypeStruct((T, VD), BF16),
        grid=(batch, nb),
        in_specs=[
            rows(KD, 0), rows(KD, 1), rows(VD, 1), rows(VD, 2), rows(GLA_Z_COLS, (2 * KD + 2 * VD) // GLA_Z_COLS),
            pl.BlockSpec((GLA_Z_COLS, KD), lambda b, i: (0, 0)),
            pl.BlockSpec((1, KD), lambda b, i: (0, 0)),
            pl.BlockSpec((1, GLA_DV), lambda b, i: (0, 0)),
        ],
        out_specs=pl.BlockSpec((ts, VD), lambda b, i: (b * nb + i, 0)),
        scratch_shapes=[pltpu.VMEM((GLA_HEADS, GLA_DK, GLA_DV), F32)],
        compiler_params=_cparams(("parallel", "arbitrary")),
        name="gla_scan",
    )(proj, proj, proj, proj, proj, w_gate, b_gate.reshape(1, KD).astype(F32),
      norm_g.reshape(1, GLA_DV).astype(F32))


def kernel(x, rel_bias, ln_mix, ln_ffn, even_w_in, lam_q1, lam_k1, lam_q2, lam_k2, subln_g, conv_w, conv_b, conv_ln_g, conv_ln_b, even_w_out, odd_w_in, w_gate2, b_gate2, gla_norm_g, odd_w_out, peer_w_query, peer_sub_keys, peer_u, peer_v, ln_final):
    B, S, D = x.shape
    T = B * S
    xf = x.reshape(T, D)

    lam_init = 0.8 - 0.6 * math.exp(-0.3 * 0)
    lam = (jnp.exp(jnp.sum(lam_q1[0].astype(F32) * lam_k1[0].astype(F32)))
           - jnp.exp(jnp.sum(lam_q2[0].astype(F32) * lam_k2[0].astype(F32))) + lam_init)
    proj = norm_matmul(xf, ln_mix[0], even_w_in[0].astype(BF16)).reshape(B, S, -1)
    a_out = diff_attention(proj, lam, _attn_bias_tiles(rel_bias, ATTN_TILE), subln_g[0],
                           out_scale=1.0 - lam_init)
    c_out = conv_module(proj, conv_w[0], conv_b[0], conv_ln_g[0], conv_ln_b[0])
    a_width = A_HEADS * A_V_DIM
    w_out = even_w_out[0].astype(BF16)
    xf = proj_residual(xf, a_out.reshape(T, -1), c_out.reshape(T, -1), w_out[:a_width], w_out[a_width:])
    xf = peer_layer(xf, ln_ffn[0], peer_w_query[0], peer_sub_keys[0], peer_u[0], peer_v[0], ln_final,
                    final_norm=False)

    kd = GLA_HEADS * GLA_DK
    vd = GLA_HEADS * GLA_DV
    w_in = jnp.pad(odd_w_in[0], ((0, 0), (0, GLA_Z_COLS - GATE_RANK))).astype(BF16)
    w_gate = jnp.pad(w_gate2[0], ((0, GLA_Z_COLS - GATE_RANK), (0, 0))).astype(BF16)
    proj = norm_matmul(xf, ln_mix[1], w_in)
    og = gla_scan(proj, w_gate, b_gate2[0], gla_norm_g[0], batch=B)
    w_out = odd_w_out[0].astype(BF16)
    xf = proj_residual(xf, og, og, w_out[:vd // 2], w_out[vd // 2:], a_col=0, c_col=1)
    xf = peer_layer(xf, ln_ffn[1], peer_w_query[1], peer_sub_keys[1], peer_u[1], peer_v[1], ln_final,
                    final_norm=True)
    return xf.reshape(B, S, D)
```

```python
import functools
import math

import jax
import jax.numpy as jnp
import numpy as np
from jax import lax
from jax.experimental import pallas as pl
from jax.experimental.pallas import tpu as pltpu

F32 = jnp.float32
BF16 = jnp.bfloat16

EPS = 1e-6
LANES = 128
VMEM_LIMIT = 56 * 1024 * 1024

CHUNK = 64
N_BUCKETS = 32
MAX_DISTANCE = 128
A_HEADS = 4
A_QK_DIM = 64
A_V_DIM = 128
CONV_WIDTH = 31
GLA_HEADS = 4
GATE_RANK = 16
GATE_TAU = 16.0
PEER_HEADS = 8
N_KEYS = 128
PEER_TOPK = 16
PEER_HALF = 128
NEG_BIG = -1e30


def _cparams(sem):
    return pltpu.CompilerParams(dimension_semantics=sem, vmem_limit_bytes=VMEM_LIMIT)


def _rms_normed(x, g):
    ms = jnp.mean(x * x, axis=-1, keepdims=True)
    return x * lax.rsqrt(ms + EPS) * g


def _norm_matmul_kernel(x_ref, g_ref, w_ref, o_ref):
    xn = _rms_normed(x_ref[...], g_ref[...]).astype(BF16)
    o_ref[...] = jnp.dot(xn, w_ref[...], preferred_element_type=F32).astype(o_ref.dtype)


def norm_matmul(x, g, w, *, tm=512, out_dtype=BF16):
    T, D = x.shape
    N = w.shape[1]
    return pl.pallas_call(
        _norm_matmul_kernel,
        out_shape=jax.ShapeDtypeStruct((T, N), out_dtype),
        grid=(T // tm,),
        in_specs=[
            pl.BlockSpec((tm, D), lambda i: (i, 0)),
            pl.BlockSpec((1, D), lambda i: (0, 0)),
            pl.BlockSpec((D, N), lambda i: (0, 0)),
        ],
        out_specs=pl.BlockSpec((tm, N), lambda i: (i, 0)),
        compiler_params=_cparams(("parallel",)),
        name="norm_matmul",
    )(x, g.reshape(1, D).astype(F32), w)


def _proj_residual_kernel(x_ref, a_ref, c_ref, wa_ref, wc_ref, o_ref):
    acc = jnp.dot(a_ref[...], wa_ref[...], preferred_element_type=F32)
    acc = acc + jnp.dot(c_ref[...], wc_ref[...], preferred_element_type=F32)
    o_ref[...] = x_ref[...] + acc


def proj_residual(x, a, c, wa, wc, *, a_col=0, c_col=0, tm=512):
    T, D = x.shape
    Ka, Kc = wa.shape[0], wc.shape[0]
    return pl.pallas_call(
        _proj_residual_kernel,
        out_shape=jax.ShapeDtypeStruct((T, D), F32),
        grid=(T // tm,),
        in_specs=[
            pl.BlockSpec((tm, D), lambda i: (i, 0)),
            pl.BlockSpec((tm, Ka), lambda i: (i, a_col)),
            pl.BlockSpec((tm, Kc), lambda i: (i, c_col)),
            pl.BlockSpec((Ka, D), lambda i: (0, 0)),
            pl.BlockSpec((Kc, D), lambda i: (0, 0)),
        ],
        out_specs=pl.BlockSpec((tm, D), lambda i: (i, 0)),
        compiler_params=_cparams(("parallel",)),
        name="proj_residual",
    )(x, a, c, wa, wc)


def _pair_candidates():
    return [(a, b) for a in range(PEER_TOPK) for b in range(PEER_TOPK) if (a + 1) * (b + 1) <= PEER_TOPK]


def _dup_bf16_bits(v):
    hi = pltpu.bitcast(v.astype(BF16).astype(F32), jnp.uint32)
    return hi | (hi >> 16)


def _pack_bf16_pairs(even, odd):
    be = pltpu.bitcast(even.astype(BF16).astype(F32), jnp.uint32)
    bo = pltpu.bitcast(odd.astype(BF16).astype(F32), jnp.uint32)
    return (be >> 16) | bo


def _route_kernel(x_ref, g_ref, wqT_ref, sk_ref, rank2_ref, e2_ref, r1_ref, e1z_ref,
                  xn_scr, s_scr, rank_scr, top_scr, r1top_scr, stat_scr):
    tm = x_ref.shape[0]
    n_groups = tm // LANES
    xn_scr[...] = _rms_normed(x_ref[...], g_ref[...]).astype(BF16)

    def score_body(h, carry):
        w_h = wqT_ref[pl.ds(pl.multiple_of(h * 2 * PEER_HALF, 2 * PEER_HALF), 2 * PEER_HALF), :]
        qT = lax.dot_general(w_h, xn_scr[...], (((1,), (1,)), ((), ())),
                             preferred_element_type=F32).astype(BF16)
        for p in range(2):
            s = jnp.dot(sk_ref[2 * h + p], qT[p * PEER_HALF:(p + 1) * PEER_HALF, :],
                        preferred_element_type=F32)
            for grp in range(n_groups):
                s_scr[2 * h + p, grp] = s[:, grp * LANES:(grp + 1) * LANES]
        return carry

    lax.fori_loop(0, PEER_HEADS, score_body, 0)

    def top_body(hp, carry):
        for grp in range(n_groups):
            work = s_scr[hp, grp]
            rank = jnp.full(work.shape, float(PEER_TOPK), F32)
            for k in range(PEER_TOPK):
                m = jnp.max(work, axis=0, keepdims=True)
                eq = work == m
                rank = jnp.where(eq, float(k), rank)
                work = jnp.where(eq, -jnp.inf, work)
                top_scr[hp, k, grp:grp + 1, :] = m
            rank_scr[hp, grp] = rank
        return carry

    lax.fori_loop(0, 2 * PEER_HEADS, top_body, 0)

    cands = _pair_candidates()

    def pair_body(h, carry):
        v1 = [top_scr[2 * h, a] for a in range(PEER_TOPK)]
        v2 = [top_scr[2 * h + 1, b] for b in range(PEER_TOPK)]
        sums = [v1[a] + v2[b] for (a, b) in cands]
        work = list(sums)
        tau = None
        for k in range(PEER_TOPK):
            tau = functools.reduce(jnp.maximum, work)
            if k + 1 < PEER_TOPK:
                work = [jnp.where(w == tau, -jnp.inf, w) for w in work]
        cmax = v1[0] + v2[0]
        z = jnp.zeros_like(tau)
        r1 = [jnp.zeros_like(tau) for _ in range(PEER_TOPK)]
        for (a, b), c in zip(cands, sums):
            sel = c >= tau
            z = z + jnp.where(sel, jnp.exp(c - cmax), 0.0)
            r1[a] = r1[a] + jnp.where(sel, 1.0, 0.0)
        for a in range(PEER_TOPK):
            r1top_scr[h, a] = r1[a]
        stat_scr[h, 0] = v1[0]
        stat_scr[h, 1] = v2[0]
        stat_scr[h, 2] = 1.0 / z
        return carry

    lax.fori_loop(0, PEER_HEADS, pair_body, 0)

    even_rows = pl.ds(0, N_KEYS // 2, stride=2)
    odd_rows = pl.ds(1, N_KEYS // 2, stride=2)
    def table_body(h, carry):
        for grp in range(n_groups):
            lanes = slice(grp * LANES, (grp + 1) * LANES)
            max1 = stat_scr[h, 0, grp:grp + 1, :]
            max2 = stat_scr[h, 1, grp:grp + 1, :]
            inv_z = stat_scr[h, 2, grp:grp + 1, :]
            rank1 = rank_scr[2 * h, grp]
            r1_key = jnp.zeros(rank1.shape, F32)
            for a in range(PEER_TOPK):
                r1_key = jnp.where(rank1 == float(a), r1top_scr[h, a, grp:grp + 1, :], r1_key)
            r1_ref[h, :, lanes] = _dup_bf16_bits(r1_key)
            e1z = jnp.exp(s_scr[2 * h, grp] - max1) * inv_z
            e1z_ref[h, :, lanes] = _dup_bf16_bits(e1z)
            rk_e = rank_scr[2 * h + 1, grp, even_rows, :]
            rk_o = rank_scr[2 * h + 1, grp, odd_rows, :]
            rank2_ref[h, :, lanes] = _pack_bf16_pairs(rk_e, rk_o)
            e2_e = jnp.exp(jnp.minimum(s_scr[2 * h + 1, grp, even_rows, :] - max2, 0.0))
            e2_o = jnp.exp(jnp.minimum(s_scr[2 * h + 1, grp, odd_rows, :] - max2, 0.0))
            e2_ref[h, :, lanes] = _pack_bf16_pairs(e2_e, e2_o)
        return carry

    lax.fori_loop(0, PEER_HEADS, table_body, 0)


def peer_route(x, g, wqT, sk, *, tm=512):
    T, D = x.shape
    H = PEER_HEADS
    n_groups = tm // LANES
    tab = lambda rows: jax.ShapeDtypeStruct((H, rows, T), jnp.uint32)
    tab_spec = lambda rows: pl.BlockSpec((H, rows, tm), lambda i: (0, 0, i))
    return pl.pallas_call(
        _route_kernel,
        out_shape=(tab(N_KEYS // 2), tab(N_KEYS // 2), tab(N_KEYS), tab(N_KEYS)),
        grid=(T // tm,),
        in_specs=[
            pl.BlockSpec((tm, D), lambda i: (i, 0)),
            pl.BlockSpec((1, D), lambda i: (0, 0)),
            pl.BlockSpec((H * 2 * PEER_HALF, D), lambda i: (0, 0)),
            pl.BlockSpec((2 * H, N_KEYS, PEER_HALF), lambda i: (0, 0, 0)),
        ],
        out_specs=(tab_spec(N_KEYS // 2), tab_spec(N_KEYS // 2), tab_spec(N_KEYS), tab_spec(N_KEYS)),
        scratch_shapes=[
            pltpu.VMEM((tm, D), BF16),
            pltpu.VMEM((2 * H, n_groups, N_KEYS, LANES), F32),
            pltpu.VMEM((2 * H, n_groups, N_KEYS, LANES), F32),
            pltpu.VMEM((2 * H, PEER_TOPK, n_groups, LANES), F32),
            pltpu.VMEM((H, PEER_TOPK, n_groups, LANES), F32),
            pltpu.VMEM((H, 3, n_groups, LANES), F32),
        ],
        compiler_params=_cparams(("parallel",)),
        name="peer_route",
    )(x, g.reshape(1, D).astype(F32), wqT, sk)


GELU_C0 = math.sqrt(2.0 / math.pi)
GELU_C1 = 0.044715


def _gelu_tanh(a):
    inner = GELU_C0 * (a + GELU_C1 * (a * a * a))
    return 0.5 * a * (1.0 + jnp.tanh(inner))


def _peer_main_kernel(x_ref, g_ref, rank2_ref, e2_ref, r1_ref, e1z_ref, u_ref, vt_ref, gf_ref, o_ref,
                      xn_scr, acc_scr, act0_scr, act1_scr, coef0_scr, coef1_scr, *, final_norm):
    j = pl.program_id(1)
    tm = x_ref.shape[0]
    n_groups = tm // LANES
    n_i1 = r1_ref.shape[1]
    rows = 16
    words = rows * jnp.dtype(BF16).itemsize // 4

    @pl.when(j == 0)
    def _():
        xn_scr[...] = _rms_normed(x_ref[...], g_ref[...]).astype(BF16)
        acc_scr[...] = jnp.zeros_like(acc_scr)
        act1_scr[...] = jnp.zeros_like(act1_scr)
        coef0_scr[...] = jnp.zeros_like(coef0_scr)
        coef1_scr[...] = jnp.zeros_like(coef1_scr)

    eb = 2 * u_ref.shape[0]
    d_model = 2 * vt_ref.shape[0]
    tn = 2 * LANES
    n_tok = tm // tn

    def body(act_w, act_r, coef_w, coef_r):
        def stage_a(n, half):
            r = slice(half * (eb // 2), (half + 1) * (eb // 2))
            rw = slice(half * (eb // 4), (half + 1) * (eb // 4))
            c = slice(n * tn, (n + 1) * tn)
            act_w[r, c] = lax.dot_general(pltpu.bitcast(u_ref[rw, :], BF16), xn_scr[c, :],
                                          (((1,), (1,)), ((), ())), preferred_element_type=F32)

        def stage_c(n, half):
            r = slice(half * (d_model // 2), (half + 1) * (d_model // 2))
            rw = slice(half * (d_model // 4), (half + 1) * (d_model // 4))
            c = slice(n * tn, (n + 1) * tn)
            acc_scr[r, c] += jnp.dot(pltpu.bitcast(vt_ref[rw, :], BF16), coef_r[:, c],
                                     preferred_element_type=F32)

        def stage_b(grp, i1_range):
            lanes = slice(grp * LANES, (grp + 1) * LANES)
            for i1 in i1_range:
                r1b, e1b = [], []
                for h in range(PEER_HEADS):
                    r1row = jnp.broadcast_to(r1_ref[h, i1:i1 + 1, lanes], (words, LANES))
                    e1row = jnp.broadcast_to(e1z_ref[h, i1:i1 + 1, lanes], (words, LANES))
                    r1b.append(pltpu.bitcast(r1row, BF16))
                    e1b.append(pltpu.bitcast(e1row, BF16))
                for blk in range(N_KEYS // rows):
                    i2w = slice(blk * words, (blk + 1) * words)
                    gate = None
                    for h in range(PEER_HEADS):
                        sel = pltpu.bitcast(rank2_ref[h, i2w, lanes], BF16) < r1b[h]
                        term = jnp.where(sel, pltpu.bitcast(e2_ref[h, i2w, lanes], BF16),
                                         jnp.zeros((), BF16)) * e1b[h]
                        gate = term if gate is None else gate + term
                    r0 = i1 * N_KEYS + blk * rows
                    a = act_r[r0:r0 + rows, lanes].astype(BF16)
                    coef_w[r0:r0 + rows, lanes] = _gelu_tanh(a) * gate

        b_pieces = [(grp, range(h * (n_i1 // 2), (h + 1) * (n_i1 // 2)))
                    for grp in range(n_groups) for h in range(2)]
        mxu_pieces = []
        for n in range(n_tok):
            for half in range(2):
                mxu_pieces += [(stage_a, n, half), (stage_c, n, half)]
        for k in range(max(len(b_pieces), len(mxu_pieces))):
            if k < len(mxu_pieces):
                fn, n, half = mxu_pieces[k]
                fn(n, half)
            if k < len(b_pieces):
                stage_b(*b_pieces[k])

    @pl.when(j % 2 == 0)
    def _():
        body(act0_scr, act1_scr, coef1_scr, coef0_scr)

    @pl.when(j % 2 == 1)
    def _():
        body(act1_scr, act0_scr, coef0_scr, coef1_scr)

    @pl.when(j == pl.num_programs(1) - 1)
    def _():
        y = x_ref[...] + acc_scr[...].T
        if final_norm:
            y = _rms_normed(y, gf_ref[...])
        o_ref[...] = y


def _pack_row_pairs(w):
    r, c = w.shape
    return lax.bitcast_convert_type(w.reshape(r // 2, 2, c).transpose(0, 2, 1), jnp.uint32)


def peer_main(x, g, tables, u, vt, g_final, *, final_norm, tm=512, eb=1024):
    T, D = x.shape
    E = 2 * u.shape[0]
    H = PEER_HEADS
    rank2, e2, r1, e1z = tables
    n_i1 = eb // N_KEYS
    nb = E // eb
    blk = lambda j, lag: jnp.clip(j - lag, 0, nb - 1)
    full_tab = pl.BlockSpec((H, N_KEYS // 2, tm), lambda i, j: (0, 0, i))
    i1_tab = pl.BlockSpec((H, n_i1, tm), lambda i, j: (0, blk(j, 1), i))
    return pl.pallas_call(
        functools.partial(_peer_main_kernel, final_norm=final_norm),
        out_shape=jax.ShapeDtypeStruct((T, D), F32),
        grid=(T // tm, nb + 2),
        in_specs=[
            pl.BlockSpec((tm, D), lambda i, j: (i, 0)),
            pl.BlockSpec((1, D), lambda i, j: (0, 0)),
            full_tab, full_tab, i1_tab, i1_tab,
            pl.BlockSpec((eb // 2, D), lambda i, j: (blk(j, 0), 0)),
            pl.BlockSpec((D // 2, eb), lambda i, j: (0, blk(j, 2))),
            pl.BlockSpec((1, D), lambda i, j: (0, 0)),
        ],
        out_specs=pl.BlockSpec((tm, D), lambda i, j: (i, 0)),
        scratch_shapes=[
            pltpu.VMEM((tm, D), BF16),
            pltpu.VMEM((D, tm), F32),
            pltpu.VMEM((eb, tm), F32),
            pltpu.VMEM((eb, tm), F32),
            pltpu.VMEM((eb, tm), BF16),
            pltpu.VMEM((eb, tm), BF16),
        ],
        compiler_params=_cparams(("parallel", "arbitrary")),
        name="peer_main",
    )(x, g.reshape(1, D).astype(F32), rank2, e2, r1, e1z, u, vt, g_final.reshape(1, D).astype(F32))


def peer_layer(x, g, w_query, sub_keys, expert_u, expert_v, g_final, *, final_norm):
    wqT = w_query.T.astype(BF16)
    sk = sub_keys.reshape(2 * PEER_HEADS, N_KEYS, PEER_HALF).astype(BF16)
    tables = peer_route(x, g, wqT, sk)
    return peer_main(x, g, tables, _pack_row_pairs(expert_u.astype(BF16)),
                     _pack_row_pairs(expert_v.T.astype(BF16)), g_final, final_norm=final_norm)


ATTN_TILE = 256


def _attn_kernel(lam_ref, q_ref, k_ref, v_ref, bias_ref, gsub_ref, o_ref, vt_scr, m_scr, l_scr, acc_scr,
                 s0_scr, s1_scr, *, out_scale):
    i = pl.program_id(2)
    tq = q_ref.shape[1]
    tk = tq

    @pl.when(i == 0)
    def _():
        vt_scr[...] = v_ref[0].T

    q = q_ref[0]
    lane = lax.broadcasted_iota(jnp.int32, q.shape, 1)
    scale = jnp.asarray(A_QK_DIM ** -0.5, BF16)
    zero = jnp.zeros((), BF16)
    qs = jnp.concatenate([jnp.where(lane < A_QK_DIM, q, zero) * scale,
                          jnp.where(lane >= A_QK_DIM, q, zero) * scale], axis=0)

    m_scr[...] = jnp.full(m_scr.shape, NEG_BIG, F32)
    l_scr[...] = jnp.zeros(l_scr.shape, F32)
    acc_scr[...] = jnp.zeros(acc_scr.shape, F32)

    def scores(j):
        k = k_ref[0, pl.ds(pl.multiple_of(j * tk, tk), tk), :]
        return lax.dot_general(k, qs, (((1,), (1,)), ((), ())), preferred_element_type=F32)

    def consume(j, bias, s_cur, s_next):
        if s_next is not None:
            s_next[...] = scores(j + 1)
        s = s_cur[...]
        if bias is not None:
            s = s + jnp.concatenate([bias, bias], axis=1)
        vt = vt_scr[:, pl.ds(pl.multiple_of(j * tk, tk), tk)]
        m_prev = m_scr[...]
        m_new = jnp.maximum(m_prev, jnp.max(s, axis=0, keepdims=True))
        alpha = jnp.exp(m_prev - m_new)
        p = jnp.exp(s - m_new)
        l_scr[...] = alpha * l_scr[...] + jnp.sum(p, axis=0, keepdims=True)
        acc_scr[...] = alpha * acc_scr[...] + jnp.dot(vt, p.astype(BF16), preferred_element_type=F32)
        m_scr[...] = m_new

    def by_parity(j, fn):
        @pl.when(j % 2 == 0)
        def _():
            fn(s0_scr, s1_scr)

        @pl.when(j % 2 == 1)
        def _():
            fn(s1_scr, s0_scr)

    s0_scr[...] = scores(0)

    def far_body(j, carry):
        by_parity(j, lambda cur, nxt: consume(j, None, cur, nxt))
        return carry

    lax.fori_loop(0, i - 1, far_body, 0)

    @pl.when(i >= 1)
    def _():
        by_parity(i - 1, lambda cur, nxt: consume(i - 1, bias_ref[0, 0], cur, nxt))

    by_parity(i, lambda cur, nxt: consume(i, bias_ref[0, 1], cur, None))

    ot = acc_scr[...] / l_scr[...]
    ot = ot[:, :tq] - lam_ref[0] * ot[:, tq:]
    ot = ot * lax.rsqrt(jnp.mean(ot * ot, axis=0, keepdims=True) + EPS)
    o_ref[0] = (ot.T * (gsub_ref[...] * out_scale)).astype(o_ref.dtype)


def _t5_bucket(rel):
    nb = N_BUCKETS // 2
    max_exact = nb // 2
    ret = jnp.where(rel > 0, nb, 0)
    n = jnp.abs(rel)
    nf = jnp.maximum(n, 1).astype(F32)
    large = max_exact + (jnp.log(nf / max_exact) / math.log(MAX_DISTANCE / max_exact)
                         * (nb - max_exact)).astype(jnp.int32)
    large = jnp.minimum(large, nb - 1)
    return ret + jnp.where(n < max_exact, n, large)


def _attn_bias_tiles(rel_bias, tile):
    r = jnp.arange(tile)[None, :]
    c = jnp.arange(tile)[:, None]
    rb = rel_bias.astype(F32)

    def lookup(bucket):
        return jnp.einsum("...b,bh->...h", jax.nn.one_hot(bucket, N_BUCKETS, dtype=F32), rb,
                          precision=lax.Precision.HIGHEST)

    far = lookup(_t5_bucket(jnp.asarray(-(2 * tile))))
    prev = lookup(_t5_bucket(c - r - tile)) - far
    diag = lookup(_t5_bucket(c - r)) - far
    diag = jnp.where(((c // CHUNK) <= (r // CHUNK))[..., None], diag, NEG_BIG)
    return jnp.transpose(jnp.stack([prev, diag], axis=0), (3, 0, 1, 2))


def diff_attention(proj, lam, bias_tiles, subln_g, *, out_scale):
    B, S, _ = proj.shape
    H = A_HEADS
    tq = ATTN_TILE
    return pl.pallas_call(
        functools.partial(_attn_kernel, out_scale=out_scale),
        out_shape=jax.ShapeDtypeStruct((B, S, H * A_V_DIM), BF16),
        grid=(B, H, S // tq),
        in_specs=[
            pl.BlockSpec(memory_space=pltpu.SMEM),
            pl.BlockSpec((1, tq, A_V_DIM), lambda b, h, i: (b, i, h)),
            pl.BlockSpec((1, S, A_V_DIM), lambda b, h, i: (b, 0, H + h)),
            pl.BlockSpec((1, S, A_V_DIM), lambda b, h, i: (b, 0, 2 * H + h)),
            pl.BlockSpec((1, 2, tq, tq), lambda b, h, i: (h, 0, 0, 0)),
            pl.BlockSpec((1, A_V_DIM), lambda b, h, i: (0, 0)),
        ],
        out_specs=pl.BlockSpec((1, tq, A_V_DIM), lambda b, h, i: (b, i, h)),
        scratch_shapes=[
            pltpu.VMEM((A_V_DIM, S), BF16),
            pltpu.VMEM((1, 2 * tq), F32),
            pltpu.VMEM((1, 2 * tq), F32),
            pltpu.VMEM((A_V_DIM, 2 * tq), F32),
            pltpu.VMEM((tq, 2 * tq), F32),
            pltpu.VMEM((tq, 2 * tq), F32),
        ],
        compiler_params=_cparams(("parallel", "parallel", "arbitrary")),
        name="diff_attention",
    )(lam.reshape(1).astype(F32), proj, proj, proj, bias_tiles, subln_g.reshape(1, A_V_DIM).astype(F32))


CONV_HALO = 32
CONV_ROWS = 64


def _conv_kernel(val_ref, gate_ref, hval_ref, hgate_ref, w_ref, b_ref, g_ref, beta_ref, o_ref, buf_scr):
    i = pl.program_id(1)
    ts = val_ref.shape[1]

    def glu(v, gt):
        return v.astype(F32) * jax.nn.sigmoid(gt.astype(F32))

    halo = glu(hval_ref[0], hgate_ref[0])
    buf_scr[0:CONV_HALO, :] = jnp.where(i > 0, halo, 0.0)
    buf_scr[CONV_HALO:, :] = glu(val_ref[0], gate_ref[0])

    first = CONV_HALO - (CONV_WIDTH - 1)
    for r in range(ts // CONV_ROWS):
        acc = None
        for w in range(CONV_WIDTH):
            lo = r * CONV_ROWS + first + w
            term = buf_scr[lo:lo + CONV_ROWS, :] * w_ref[w:w + 1, :]
            acc = term if acc is None else acc + term
        y = acc + b_ref[...]
        mu = jnp.mean(y, axis=-1, keepdims=True)
        d = y - mu
        var = jnp.mean(d * d, axis=-1, keepdims=True)
        y = d * lax.rsqrt(var + EPS) * g_ref[...] + beta_ref[...]
        y = y * jax.nn.sigmoid(y)
        o_ref[0, r * CONV_ROWS:(r + 1) * CONV_ROWS, :] = y.astype(o_ref.dtype)


def conv_module(proj, conv_w, conv_b, ln_g, ln_b, *, ts=512):
    B, S, _ = proj.shape
    C = conv_w.shape[1]
    hb = ts // CONV_HALO
    row = lambda a: a.reshape(1, C).astype(F32)
    cur = lambda col: pl.BlockSpec((1, ts, C), lambda b, i: (b, i, col))
    halo = lambda col: pl.BlockSpec((1, CONV_HALO, C), lambda b, i: (b, jnp.maximum(i * hb - 1, 0), col))
    vec = pl.BlockSpec((1, C), lambda b, i: (0, 0))
    return pl.pallas_call(
        _conv_kernel,
        out_shape=jax.ShapeDtypeStruct((B, S, C), BF16),
        grid=(B, S // ts),
        in_specs=[cur(3), cur(4), halo(3), halo(4),
                  pl.BlockSpec((CONV_WIDTH + 1, C), lambda b, i: (0, 0)), vec, vec, vec],
        out_specs=pl.BlockSpec((1, ts, C), lambda b, i: (b, i, 0)),
        scratch_shapes=[pltpu.VMEM((ts + CONV_HALO, C), F32)],
        compiler_params=_cparams(("parallel", "parallel")),
        name="conv_module",
    )(proj, proj, proj, proj, jnp.pad(conv_w.astype(F32), ((0, 1), (0, 0))), row(conv_b), row(ln_g), row(ln_b))


GLA_DK = 128
GLA_DV = 256
GLA_Z_COLS = 128


def _split_bf16(x):
    hi = x.astype(BF16)
    lo = (x - hi.astype(F32)).astype(BF16)
    return hi, lo


def _gla_kernel(q_ref, k_ref, v_ref, g_ref, z_ref, wg_ref, bg_ref, gn_ref, o_ref, state_scr):
    ts = q_ref.shape[0]
    n_chunks = ts // CHUNK

    @pl.when(pl.program_id(1) == 0)
    def _():
        state_scr[...] = jnp.zeros(state_scr.shape, F32)

    zz = jnp.dot(z_ref[...], wg_ref[...], preferred_element_type=F32) + bg_ref[...]
    log_a = (jnp.minimum(zz, 0.0) - jnp.log1p(jnp.exp(-jnp.abs(zz)))) / GATE_TAU

    row = lax.broadcasted_iota(jnp.int32, (CHUNK, CHUNK), 0)
    col = lax.broadcasted_iota(jnp.int32, (CHUNK, CHUNK), 1)
    tri = jnp.where(col <= row, 1.0, 0.0).astype(BF16)
    ones = jnp.ones((CHUNK, LANES), BF16)
    tn = (((0,), (0,)), ((), ()))
    q_scale = GLA_DK ** -0.5

    for c in range(n_chunks):
        rows = slice(c * CHUNK, (c + 1) * CHUNK)
        la_hi, la_lo = _split_bf16(log_a[rows, :])
        cum = (jnp.dot(tri, la_hi, preferred_element_type=F32)
               + jnp.dot(tri, la_lo, preferred_element_type=F32))
        k_dec = k_ref[rows, :].astype(F32) * jnp.exp(cum[CHUNK - 1:CHUNK, :] - cum)
        for h in range(GLA_HEADS):
            kc = slice(h * GLA_DK, (h + 1) * GLA_DK)
            vc = slice(h * GLA_DV, (h + 1) * GLA_DV)
            tot = (lax.dot_general(la_hi[:, kc], ones, tn, preferred_element_type=F32)
                   + lax.dot_general(la_lo[:, kc], ones, tn, preferred_element_type=F32))
            decay = jnp.exp(tot)
            decay = jnp.concatenate([decay] * (GLA_DV // LANES), axis=1)
            kv = lax.dot_general(k_dec[:, kc].astype(BF16), v_ref[rows, vc], tn,
                                 preferred_element_type=F32)
            state = state_scr[h] * decay + kv
            state_scr[h] = state
            o = jnp.dot(q_ref[rows, kc], state.astype(BF16), preferred_element_type=F32) * q_scale
            o = o * lax.rsqrt(jnp.mean(o * o, axis=-1, keepdims=True) + EPS) * gn_ref[...]
            gt = g_ref[rows, vc].astype(F32)
            o_ref[rows, vc] = (o * (gt * jax.nn.sigmoid(gt))).astype(o_ref.dtype)


def gla_scan(proj, w_gate, b_gate, norm_g, *, batch, ts=512):
    T = proj.shape[0]
    S = T // batch
    nb = S // ts
    KD = GLA_HEADS * GLA_DK
    VD = GLA_HEADS * GLA_DV
    rows = lambda width, col: pl.BlockSpec((ts, width), lambda b, i: (b * nb + i, col))
    return pl.pallas_call(
        _gla_kernel,
        out_shape=jax.ShapeDtypeStruct((T, VD), BF16),
        grid=(batch, nb),
        in_specs=[
            rows(KD, 0), rows(KD, 1), rows(VD, 1), rows(VD, 2), rows(GLA_Z_COLS, (2 * KD + 2 * VD) // GLA_Z_COLS),
            pl.BlockSpec((GLA_Z_COLS, KD), lambda b, i: (0, 0)),
            pl.BlockSpec((1, KD), lambda b, i: (0, 0)),
            pl.BlockSpec((1, GLA_DV), lambda b, i: (0, 0)),
        ],
        out_specs=pl.BlockSpec((ts, VD), lambda b, i: (b * nb + i, 0)),
        scratch_shapes=[pltpu.VMEM((GLA_HEADS, GLA_DK, GLA_DV), F32)],
        compiler_params=_cparams(("parallel", "arbitrary")),
        name="gla_scan",
    )(proj, proj, proj, proj, proj, w_gate, b_gate.reshape(1, KD).astype(F32),
      norm_g.reshape(1, GLA_DV).astype(F32))


def kernel(x, rel_bias, ln_mix, ln_ffn, even_w_in, lam_q1, lam_k1, lam_q2, lam_k2, subln_g, conv_w, conv_b, conv_ln_g, conv_ln_b, even_w_out, odd_w_in, w_gate2, b_gate2, gla_norm_g, odd_w_out, peer_w_query, peer_sub_keys, peer_u, peer_v, ln_final):
    B, S, D = x.shape
    T = B * S
    xf = x.reshape(T, D)

    lam_init = 0.8 - 0.6 * math.exp(-0.3 * 0)
    lam = (jnp.exp(jnp.sum(lam_q1[0].astype(F32) * lam_k1[0].astype(F32)))
           - jnp.exp(jnp.sum(lam_q2[0].astype(F32) * lam_k2[0].astype(F32))) + lam_init)
    proj = norm_matmul(xf, ln_mix[0], even_w_in[0].astype(BF16)).reshape(B, S, -1)
    a_out = diff_attention(proj, lam, _attn_bias_tiles(rel_bias, ATTN_TILE), subln_g[0],
                           out_scale=1.0 - lam_init)
    c_out = conv_module(proj, conv_w[0], conv_b[0], conv_ln_g[0], conv_ln_b[0])
    a_width = A_HEADS * A_V_DIM
    w_out = even_w_out[0].astype(BF16)
    xf = proj_residual(xf, a_out.reshape(T, -1), c_out.reshape(T, -1), w_out[:a_width], w_out[a_width:])
    xf = peer_layer(xf, ln_ffn[0], peer_w_query[0], peer_sub_keys[0], peer_u[0], peer_v[0], ln_final,
                    final_norm=False)

    kd = GLA_HEADS * GLA_DK
    vd = GLA_HEADS * GLA_DV
    w_in = jnp.pad(odd_w_in[0], ((0, 0), (0, GLA_Z_COLS - GATE_RANK))).astype(BF16)
    w_gate = jnp.pad(w_gate2[0], ((0, GLA_Z_COLS - GATE_RANK), (0, 0))).astype(BF16)
    proj = norm_matmul(xf, ln_mix[1], w_in)
    og = gla_scan(proj, w_gate, b_gate2[0], gla_norm_g[0], batch=B)
    w_out = odd_w_out[0].astype(BF16)
    xf = proj_residual(xf, og, og, w_out[:vd // 2], w_out[vd // 2:], a_col=0, c_col=1)
    xf = peer_layer(xf, ln_ffn[1], peer_w_query[1], peer_sub_keys[1], peer_u[1], peer_v[1], ln_final,
                    final_norm=True)
    return xf.reshape(B, S, D)
```

```python
import functools
import math

import jax
import jax.numpy as jnp
import numpy as np
from jax import lax
from jax.experimental import pallas as pl
from jax.experimental.pallas import tpu as pltpu

F32 = jnp.float32
BF16 = jnp.bfloat16

EPS = 1e-6
LANES = 128
VMEM_LIMIT = 56 * 1024 * 1024

CHUNK = 64
N_BUCKETS = 32
MAX_DISTANCE = 128
A_HEADS = 4
A_QK_DIM = 64
A_V_DIM = 128
CONV_WIDTH = 31
GLA_HEADS = 4
GATE_RANK = 16
GATE_TAU = 16.0
PEER_HEADS = 8
N_KEYS = 128
PEER_TOPK = 16
PEER_HALF = 128
NEG_BIG = -1e30


def _cparams(sem):
    return pltpu.CompilerParams(dimension_semantics=sem, vmem_limit_bytes=VMEM_LIMIT)


def _rms_normed(x, g):
    ms = jnp.mean(x * x, axis=-1, keepdims=True)
    return x * lax.rsqrt(ms + EPS) * g


def _norm_matmul_kernel(x_ref, g_ref, w_ref, o_ref):
    xn = _rms_normed(x_ref[...], g_ref[...]).astype(BF16)
    o_ref[...] = jnp.dot(xn, w_ref[...], preferred_element_type=F32).astype(o_ref.dtype)


def norm_matmul(x, g, w, *, tm=512, out_dtype=BF16):
    T, D = x.shape
    N = w.shape[1]
    return pl.pallas_call(
        _norm_matmul_kernel,
        out_shape=jax.ShapeDtypeStruct((T, N), out_dtype),
        grid=(T // tm,),
        in_specs=[
            pl.BlockSpec((tm, D), lambda i: (i, 0)),
            pl.BlockSpec((1, D), lambda i: (0, 0)),
            pl.BlockSpec((D, N), lambda i: (0, 0)),
        ],
        out_specs=pl.BlockSpec((tm, N), lambda i: (i, 0)),
        compiler_params=_cparams(("parallel",)),
        name="norm_matmul",
    )(x, g.reshape(1, D).astype(F32), w)


def _proj_residual_kernel(x_ref, a_ref, c_ref, wa_ref, wc_ref, o_ref):
    acc = jnp.dot(a_ref[...], wa_ref[...], preferred_element_type=F32)
    acc = acc + jnp.dot(c_ref[...], wc_ref[...], preferred_element_type=F32)
    o_ref[...] = x_ref[...] + acc


def proj_residual(x, a, c, wa, wc, *, a_col=0, c_col=0, tm=512):
    T, D = x.shape
    Ka, Kc = wa.shape[0], wc.shape[0]
    return pl.pallas_call(
        _proj_residual_kernel,
        out_shape=jax.ShapeDtypeStruct((T, D), F32),
        grid=(T // tm,),
        in_specs=[
            pl.BlockSpec((tm, D), lambda i: (i, 0)),
            pl.BlockSpec((tm, Ka), lambda i: (i, a_col)),
            pl.BlockSpec((tm, Kc), lambda i: (i, c_col)),
            pl.BlockSpec((Ka, D), lambda i: (0, 0)),
            pl.BlockSpec((Kc, D), lambda i: (0, 0)),
        ],
        out_specs=pl.BlockSpec((tm, D), lambda i: (i, 0)),
        compiler_params=_cparams(("parallel",)),
        name="proj_residual",
    )(x, a, c, wa, wc)


def _pair_candidates():
    return [(a, b) for a in range(PEER_TOPK) for b in range(PEER_TOPK) if (a + 1) * (b + 1) <= PEER_TOPK]


def _dup_bf16_bits(v):
    hi = pltpu.bitcast(v.astype(BF16).astype(F32), jnp.uint32)
    return hi | (hi >> 16)


def _pack_bf16_pairs(even, odd):
    be = pltpu.bitcast(even.astype(BF16).astype(F32), jnp.uint32)
    bo = pltpu.bitcast(odd.astype(BF16).astype(F32), jnp.uint32)
    return (be >> 16) | bo


def _route_kernel(x_ref, g_ref, wqT_ref, sk_ref, rank2_ref, e2_ref, r1_ref, e1z_ref,
                  xn_scr, s_scr, rank_scr, top_scr, r1top_scr, stat_scr):
    tm = x_ref.shape[0]
    n_groups = tm // LANES
    xn_scr[...] = _rms_normed(x_ref[...], g_ref[...]).astype(BF16)

    def score_body(h, carry):
        w_h = wqT_ref[pl.ds(pl.multiple_of(h * 2 * PEER_HALF, 2 * PEER_HALF), 2 * PEER_HALF), :]
        qT = lax.dot_general(w_h, xn_scr[...], (((1,), (1,)), ((), ())),
                             preferred_element_type=F32).astype(BF16)
        for p in range(2):
            s = jnp.dot(sk_ref[2 * h + p], qT[p * PEER_HALF:(p + 1) * PEER_HALF, :],
                        preferred_element_type=F32)
            for grp in range(n_groups):
                s_scr[2 * h + p, grp] = s[:, grp * LANES:(grp + 1) * LANES]
        return carry

    lax.fori_loop(0, PEER_HEADS, score_body, 0)

    def top_body(hp, carry):
        for grp in range(n_groups):
            work = s_scr[hp, grp]
            rank = jnp.full(work.shape, float(PEER_TOPK), F32)
            for k in range(PEER_TOPK):
                m = jnp.max(work, axis=0, keepdims=True)
                eq = work == m
                rank = jnp.where(eq, float(k), rank)
                work = jnp.where(eq, -jnp.inf, work)
                top_scr[hp, k, grp:grp + 1, :] = m
            rank_scr[hp, grp] = rank
        return carry

    lax.fori_loop(0, 2 * PEER_HEADS, top_body, 0)

    cands = _pair_candidates()

    def pair_body(h, carry):
        v1 = [top_scr[2 * h, a] for a in range(PEER_TOPK)]
        v2 = [top_scr[2 * h + 1, b] for b in range(PEER_TOPK)]
        sums = [v1[a] + v2[b] for (a, b) in cands]
        work = list(sums)
        tau = None
        for k in range(PEER_TOPK):
            tau = functools.reduce(jnp.maximum, work)
            if k + 1 < PEER_TOPK:
                work = [jnp.where(w == tau, -jnp.inf, w) for w in work]
        cmax = v1[0] + v2[0]
        z = jnp.zeros_like(tau)
        r1 = [jnp.zeros_like(tau) for _ in range(PEER_TOPK)]
        for (a, b), c in zip(cands, sums):
            sel = c >= tau
            z = z + jnp.where(sel, jnp.exp(c - cmax), 0.0)
            r1[a] = r1[a] + jnp.where(sel, 1.0, 0.0)
        for a in range(PEER_TOPK):
            r1top_scr[h, a] = r1[a]
        stat_scr[h, 0] = v1[0]
        stat_scr[h, 1] = v2[0]
        stat_scr[h, 2] = 1.0 / z
        return carry

    lax.fori_loop(0, PEER_HEADS, pair_body, 0)

    even_rows = pl.ds(0, N_KEYS // 2, stride=2)
    odd_rows = pl.ds(1, N_KEYS // 2, stride=2)
    def table_body(h, carry):
        for grp in range(n_groups):
            lanes = slice(grp * LANES, (grp + 1) * LANES)
            max1 = stat_scr[h, 0, grp:grp + 1, :]
            max2 = stat_scr[h, 1, grp:grp + 1, :]
            inv_z = stat_scr[h, 2, grp:grp + 1, :]
            rank1 = rank_scr[2 * h, grp]
            r1_key = jnp.zeros(rank1.shape, F32)
            for a in range(PEER_TOPK):
                r1_key = jnp.where(rank1 == float(a), r1top_scr[h, a, grp:grp + 1, :], r1_key)
            r1_ref[h, :, lanes] = _dup_bf16_bits(r1_key)
            e1z = jnp.exp(s_scr[2 * h, grp] - max1) * inv_z
            e1z_ref[h, :, lanes] = _dup_bf16_bits(e1z)
            rk_e = rank_scr[2 * h + 1, grp, even_rows, :]
            rk_o = rank_scr[2 * h + 1, grp, odd_rows, :]
            rank2_ref[h, :, lanes] = _pack_bf16_pairs(rk_e, rk_o)
            e2_e = jnp.exp(jnp.minimum(s_scr[2 * h + 1, grp, even_rows, :] - max2, 0.0))
            e2_o = jnp.exp(jnp.minimum(s_scr[2 * h + 1, grp, odd_rows, :] - max2, 0.0))
            e2_ref[h, :, lanes] = _pack_bf16_pairs(e2_e, e2_o)
        return carry

    lax.fori_loop(0, PEER_HEADS, table_body, 0)


def peer_route(x, g, wqT, sk, *, tm=512):
    T, D = x.shape
    H = PEER_HEADS
    n_groups = tm // LANES
    tab = lambda rows: jax.ShapeDtypeStruct((H, rows, T), jnp.uint32)
    tab_spec = lambda rows: pl.BlockSpec((H, rows, tm), lambda i: (0, 0, i))
    return pl.pallas_call(
        _route_kernel,
        out_shape=(tab(N_KEYS // 2), tab(N_KEYS // 2), tab(N_KEYS), tab(N_KEYS)),
        grid=(T // tm,),
        in_specs=[
            pl.BlockSpec((tm, D), lambda i: (i, 0)),
            pl.BlockSpec((1, D), lambda i: (0, 0)),
            pl.BlockSpec((H * 2 * PEER_HALF, D), lambda i: (0, 0)),
            pl.BlockSpec((2 * H, N_KEYS, PEER_HALF), lambda i: (0, 0, 0)),
        ],
        out_specs=(tab_spec(N_KEYS // 2), tab_spec(N_KEYS // 2), tab_spec(N_KEYS), tab_spec(N_KEYS)),
        scratch_shapes=[
            pltpu.VMEM((tm, D), BF16),
            pltpu.VMEM((2 * H, n_groups, N_KEYS, LANES), F32),
            pltpu.VMEM((2 * H, n_groups, N_KEYS, LANES), F32),
            pltpu.VMEM((2 * H, PEER_TOPK, n_groups, LANES), F32),
            pltpu.VMEM((H, PEER_TOPK, n_groups, LANES), F32),
            pltpu.VMEM((H, 3, n_groups, LANES), F32),
        ],
        compiler_params=_cparams(("parallel",)),
        name="peer_route",
    )(x, g.reshape(1, D).astype(F32), wqT, sk)


GELU_C0 = math.sqrt(2.0 / math.pi)
GELU_C1 = 0.044715


def _gelu_tanh(a):
    inner = GELU_C0 * (a + GELU_C1 * (a * a * a))
    return 0.5 * a * (1.0 + jnp.tanh(inner))


PEER_SUB = 512


def _peer_main_kernel(x_ref, g_ref, rank2_ref, e2_ref, r1_ref, e1z_ref, u_ref, vt_ref, gf_ref, o_ref,
                      xn_scr, acc_scr, act_scr, coef_scr, *, final_norm):
    j = pl.program_id(1)
    tm = x_ref.shape[0]
    n_i1 = r1_ref.shape[1]
    rows = 16
    words = rows * jnp.dtype(BF16).itemsize // 4

    @pl.when(j == 0)
    def _():
        xn_scr[...] = _rms_normed(x_ref[...], g_ref[...]).astype(BF16)
        acc_scr[...] = jnp.zeros_like(acc_scr)

    eb = 2 * u_ref.shape[0]
    d_model = 2 * vt_ref.shape[0]
    tn = 2 * LANES

    def stage_a(n, q):
        r = slice(q * PEER_SUB, (q + 1) * PEER_SUB)
        rw = slice(q * PEER_SUB // 2, (q + 1) * PEER_SUB // 2)
        c = slice(n * tn, (n + 1) * tn)
        act_scr[n, r, :] = lax.dot_general(pltpu.bitcast(u_ref[rw, :], BF16), xn_scr[c, :],
                                           (((1,), (1,)), ((), ())), preferred_element_type=F32)

    def stage_b(n, q):
        for sub in range(tn // LANES):
            lanes = slice((n * tn // LANES + sub) * LANES, (n * tn // LANES + sub + 1) * LANES)
            sl = slice(sub * LANES, (sub + 1) * LANES)
            for i1 in range(q * PEER_SUB // N_KEYS, (q + 1) * PEER_SUB // N_KEYS):
                r1b, e1b = [], []
                for h in range(PEER_HEADS):
                    r1row = jnp.broadcast_to(r1_ref[h, i1:i1 + 1, lanes], (words, LANES))
                    e1row = jnp.broadcast_to(e1z_ref[h, i1:i1 + 1, lanes], (words, LANES))
                    r1b.append(pltpu.bitcast(r1row, BF16))
                    e1b.append(pltpu.bitcast(e1row, BF16))
                for blk in range(N_KEYS // rows):
                    i2w = slice(blk * words, (blk + 1) * words)
                    gate = None
                    for h in range(PEER_HEADS):
                        sel = pltpu.bitcast(rank2_ref[h, i2w, lanes], BF16) < r1b[h]
                        term = jnp.where(sel, pltpu.bitcast(e2_ref[h, i2w, lanes], BF16),
                                         jnp.zeros((), BF16)) * e1b[h]
                        gate = term if gate is None else gate + term
                    r0 = i1 * N_KEYS + blk * rows
                    a = act_scr[n, r0:r0 + rows, sl].astype(BF16)
                    coef_scr[n, r0:r0 + rows, sl] = _gelu_tanh(a) * gate

    def stage_c(n):
        for half in range(2):
            r = slice(half * (d_model // 2), (half + 1) * (d_model // 2))
            rw = slice(half * (d_model // 4), (half + 1) * (d_model // 4))
            acc_scr[n, r, :] += jnp.dot(pltpu.bitcast(vt_ref[rw, :], BF16), coef_scr[n],
                                        preferred_element_type=F32)

    for n in range(tm // tn):
        for q in range(eb // PEER_SUB):
            stage_a(n, q)
            stage_b(n, q)
        stage_c(n)

    @pl.when(j == pl.num_programs(1) - 1)
    def _():
        acc_t = jnp.concatenate([acc_scr[n] for n in range(tm // tn)], axis=1)
        y = x_ref[...] + acc_t.T
        if final_norm:
            y = _rms_normed(y, gf_ref[...])
        o_ref[...] = y


def _pack_row_pairs(w):
    r, c = w.shape
    return lax.bitcast_convert_type(w.reshape(r // 2, 2, c).transpose(0, 2, 1), jnp.uint32)


def peer_main(x, g, tables, u, vt, g_final, *, final_norm, tm=512, eb=2048):
    T, D = x.shape
    E = 2 * u.shape[0]
    H = PEER_HEADS
    rank2, e2, r1, e1z = tables
    n_i1 = eb // N_KEYS
    full_tab = pl.BlockSpec((H, N_KEYS // 2, tm), lambda i, j: (0, 0, i))
    i1_tab = pl.BlockSpec((H, n_i1, tm), lambda i, j: (0, j, i))
    return pl.pallas_call(
        functools.partial(_peer_main_kernel, final_norm=final_norm),
        out_shape=jax.ShapeDtypeStruct((T, D), F32),
        grid=(T // tm, E // eb),
        in_specs=[
            pl.BlockSpec((tm, D), lambda i, j: (i, 0)),
            pl.BlockSpec((1, D), lambda i, j: (0, 0)),
            full_tab, full_tab, i1_tab, i1_tab,
            pl.BlockSpec((eb // 2, D), lambda i, j: (j, 0)),
            pl.BlockSpec((D // 2, eb), lambda i, j: (0, j)),
            pl.BlockSpec((1, D), lambda i, j: (0, 0)),
        ],
        out_specs=pl.BlockSpec((tm, D), lambda i, j: (i, 0)),
        scratch_shapes=[
            pltpu.VMEM((tm, D), BF16),
            pltpu.VMEM((tm // (2 * LANES), D, 2 * LANES), F32),
            pltpu.VMEM((tm // (2 * LANES), eb, 2 * LANES), F32),
            pltpu.VMEM((tm // (2 * LANES), eb, 2 * LANES), BF16),
        ],
        compiler_params=_cparams(("parallel", "arbitrary")),
        name="peer_main",
    )(x, g.reshape(1, D).astype(F32), rank2, e2, r1, e1z, u, vt, g_final.reshape(1, D).astype(F32))


def peer_layer(x, g, w_query, sub_keys, expert_u, expert_v, g_final, *, final_norm):
    wqT = w_query.T.astype(BF16)
    sk = sub_keys.reshape(2 * PEER_HEADS, N_KEYS, PEER_HALF).astype(BF16)
    tables = peer_route(x, g, wqT, sk)
    return peer_main(x, g, tables, _pack_row_pairs(expert_u.astype(BF16)),
                     _pack_row_pairs(expert_v.T.astype(BF16)), g_final, final_norm=final_norm)


ATTN_TILE = 256


def _attn_kernel(lam_ref, q_ref, k_ref, v_ref, bias_ref, gsub_ref, o_ref, vt_scr, m_scr, l_scr, acc_scr,
                 s0_scr, s1_scr, *, out_scale):
    i = pl.program_id(2)
    tq = q_ref.shape[1]
    tk = tq

    @pl.when(i == 0)
    def _():
        vt_scr[...] = v_ref[0].T

    q = q_ref[0]
    lane = lax.broadcasted_iota(jnp.int32, q.shape, 1)
    scale = jnp.asarray(A_QK_DIM ** -0.5, BF16)
    zero = jnp.zeros((), BF16)
    qs = jnp.concatenate([jnp.where(lane < A_QK_DIM, q, zero) * scale,
                          jnp.where(lane >= A_QK_DIM, q, zero) * scale], axis=0)

    m_scr[...] = jnp.full(m_scr.shape, NEG_BIG, F32)
    l_scr[...] = jnp.zeros(l_scr.shape, F32)
    acc_scr[...] = jnp.zeros(acc_scr.shape, F32)

    def scores(j):
        k = k_ref[0, pl.ds(pl.multiple_of(j * tk, tk), tk), :]
        return lax.dot_general(k, qs, (((1,), (1,)), ((), ())), preferred_element_type=F32)

    def consume(j, bias, s_cur, s_next):
        if s_next is not None:
            s_next[...] = scores(j + 1)
        s = s_cur[...]
        if bias is not None:
            s = s + jnp.concatenate([bias, bias], axis=1)
        vt = vt_scr[:, pl.ds(pl.multiple_of(j * tk, tk), tk)]
        m_prev = m_scr[...]
        m_new = jnp.maximum(m_prev, jnp.max(s, axis=0, keepdims=True))
        alpha = jnp.exp(m_prev - m_new)
        p = jnp.exp(s - m_new)
        l_scr[...] = alpha * l_scr[...] + jnp.sum(p, axis=0, keepdims=True)
        acc_scr[...] = alpha * acc_scr[...] + jnp.dot(vt, p.astype(BF16), preferred_element_type=F32)
        m_scr[...] = m_new

    def by_parity(j, fn):
        @pl.when(j % 2 == 0)
        def _():
            fn(s0_scr, s1_scr)

        @pl.when(j % 2 == 1)
        def _():
            fn(s1_scr, s0_scr)

    s0_scr[...] = scores(0)

    def far_body(j, carry):
        by_parity(j, lambda cur, nxt: consume(j, None, cur, nxt))
        return carry

    lax.fori_loop(0, i - 1, far_body, 0)

    @pl.when(i >= 1)
    def _():
        by_parity(i - 1, lambda cur, nxt: consume(i - 1, bias_ref[0, 0], cur, nxt))

    by_parity(i, lambda cur, nxt: consume(i, bias_ref[0, 1], cur, None))

    ot = acc_scr[...] / l_scr[...]
    ot = ot[:, :tq] - lam_ref[0] * ot[:, tq:]
    ot = ot * lax.rsqrt(jnp.mean(ot * ot, axis=0, keepdims=True) + EPS)
    o_ref[0] = (ot.T * (gsub_ref[...] * out_scale)).astype(o_ref.dtype)


def _t5_bucket(rel):
    nb = N_BUCKETS // 2
    max_exact = nb // 2
    ret = jnp.where(rel > 0, nb, 0)
    n = jnp.abs(rel)
    nf = jnp.maximum(n, 1).astype(F32)
    large = max_exact + (jnp.log(nf / max_exact) / math.log(MAX_DISTANCE / max_exact)
                         * (nb - max_exact)).astype(jnp.int32)
    large = jnp.minimum(large, nb - 1)
    return ret + jnp.where(n < max_exact, n, large)


def _attn_bias_tiles(rel_bias, tile):
    r = jnp.arange(tile)[None, :]
    c = jnp.arange(tile)[:, None]
    rb = rel_bias.astype(F32)

    def lookup(bucket):
        return jnp.einsum("...b,bh->...h", jax.nn.one_hot(bucket, N_BUCKETS, dtype=F32), rb,
                          precision=lax.Precision.HIGHEST)

    far = lookup(_t5_bucket(jnp.asarray(-(2 * tile))))
    prev = lookup(_t5_bucket(c - r - tile)) - far
    diag = lookup(_t5_bucket(c - r)) - far
    diag = jnp.where(((c // CHUNK) <= (r // CHUNK))[..., None], diag, NEG_BIG)
    return jnp.transpose(jnp.stack([prev, diag], axis=0), (3, 0, 1, 2))


def diff_attention(proj, lam, bias_tiles, subln_g, *, out_scale):
    B, S, _ = proj.shape
    H = A_HEADS
    tq = ATTN_TILE
    return pl.pallas_call(
        functools.partial(_attn_kernel, out_scale=out_scale),
        out_shape=jax.ShapeDtypeStruct((B, S, H * A_V_DIM), BF16),
        grid=(B, H, S // tq),
        in_specs=[
            pl.BlockSpec(memory_space=pltpu.SMEM),
            pl.BlockSpec((1, tq, A_V_DIM), lambda b, h, i: (b, i, h)),
            pl.BlockSpec((1, S, A_V_DIM), lambda b, h, i: (b, 0, H + h)),
            pl.BlockSpec((1, S, A_V_DIM), lambda b, h, i: (b, 0, 2 * H + h)),
            pl.BlockSpec((1, 2, tq, tq), lambda b, h, i: (h, 0, 0, 0)),
            pl.BlockSpec((1, A_V_DIM), lambda b, h, i: (0, 0)),
        ],
        out_specs=pl.BlockSpec((1, tq, A_V_DIM), lambda b, h, i: (b, i, h)),
        scratch_shapes=[
            pltpu.VMEM((A_V_DIM, S), BF16),
            pltpu.VMEM((1, 2 * tq), F32),
            pltpu.VMEM((1, 2 * tq), F32),
            pltpu.VMEM((A_V_DIM, 2 * tq), F32),
            pltpu.VMEM((tq, 2 * tq), F32),
            pltpu.VMEM((tq, 2 * tq), F32),
        ],
        compiler_params=_cparams(("parallel", "parallel", "arbitrary")),
        name="diff_attention",
    )(lam.reshape(1).astype(F32), proj, proj, proj, bias_tiles, subln_g.reshape(1, A_V_DIM).astype(F32))


CONV_HALO = 32
CONV_ROWS = 64


def _conv_kernel(val_ref, gate_ref, hval_ref, hgate_ref, w_ref, b_ref, g_ref, beta_ref, o_ref, buf_scr):
    i = pl.program_id(1)
    ts = val_ref.shape[1]

    def glu(v, gt):
        return v.astype(F32) * jax.nn.sigmoid(gt.astype(F32))

    halo = glu(hval_ref[0], hgate_ref[0])
    buf_scr[0:CONV_HALO, :] = jnp.where(i > 0, halo, 0.0)
    buf_scr[CONV_HALO:, :] = glu(val_ref[0], gate_ref[0])

    first = CONV_HALO - (CONV_WIDTH - 1)
    for r in range(ts // CONV_ROWS):
        acc = None
        for w in range(CONV_WIDTH):
            lo = r * CONV_ROWS + first + w
            term = buf_scr[lo:lo + CONV_ROWS, :] * w_ref[w:w + 1, :]
            acc = term if acc is None else acc + term
        y = acc + b_ref[...]
        mu = jnp.mean(y, axis=-1, keepdims=True)
        d = y - mu
        var = jnp.mean(d * d, axis=-1, keepdims=True)
        y = d * lax.rsqrt(var + EPS) * g_ref[...] + beta_ref[...]
        y = y * jax.nn.sigmoid(y)
        o_ref[0, r * CONV_ROWS:(r + 1) * CONV_ROWS, :] = y.astype(o_ref.dtype)


def conv_module(proj, conv_w, conv_b, ln_g, ln_b, *, ts=512):
    B, S, _ = proj.shape
    C = conv_w.shape[1]
    hb = ts // CONV_HALO
    row = lambda a: a.reshape(1, C).astype(F32)
    cur = lambda col: pl.BlockSpec((1, ts, C), lambda b, i: (b, i, col))
    halo = lambda col: pl.BlockSpec((1, CONV_HALO, C), lambda b, i: (b, jnp.maximum(i * hb - 1, 0), col))
    vec = pl.BlockSpec((1, C), lambda b, i: (0, 0))
    return pl.pallas_call(
        _conv_kernel,
        out_shape=jax.ShapeDtypeStruct((B, S, C), BF16),
        grid=(B, S // ts),
        in_specs=[cur(3), cur(4), halo(3), halo(4),
                  pl.BlockSpec((CONV_WIDTH + 1, C), lambda b, i: (0, 0)), vec, vec, vec],
        out_specs=pl.BlockSpec((1, ts, C), lambda b, i: (b, i, 0)),
        scratch_shapes=[pltpu.VMEM((ts + CONV_HALO, C), F32)],
        compiler_params=_cparams(("parallel", "parallel")),
        name="conv_module",
    )(proj, proj, proj, proj, jnp.pad(conv_w.astype(F32), ((0, 1), (0, 0))), row(conv_b), row(ln_g), row(ln_b))


GLA_DK = 128
GLA_DV = 256
GLA_Z_COLS = 128


def _split_bf16(x):
    hi = x.astype(BF16)
    lo = (x - hi.astype(F32)).astype(BF16)
    return hi, lo


def _gla_kernel(q_ref, k_ref, v_ref, g_ref, z_ref, wg_ref, bg_ref, gn_ref, o_ref, state_scr):
    ts = q_ref.shape[0]
    n_chunks = ts // CHUNK

    @pl.when(pl.program_id(1) == 0)
    def _():
        state_scr[...] = jnp.zeros(state_scr.shape, F32)

    zz = jnp.dot(z_ref[...], wg_ref[...], preferred_element_type=F32) + bg_ref[...]
    log_a = (jnp.minimum(zz, 0.0) - jnp.log1p(jnp.exp(-jnp.abs(zz)))) / GATE_TAU

    row = lax.broadcasted_iota(jnp.int32, (CHUNK, CHUNK), 0)
    col = lax.broadcasted_iota(jnp.int32, (CHUNK, CHUNK), 1)
    tri = jnp.where(col <= row, 1.0, 0.0).astype(BF16)
    ones = jnp.ones((CHUNK, LANES), BF16)
    tn = (((0,), (0,)), ((), ()))
    q_scale = GLA_DK ** -0.5

    for c in range(n_chunks):
        rows = slice(c * CHUNK, (c + 1) * CHUNK)
        la_hi, la_lo = _split_bf16(log_a[rows, :])
        cum = (jnp.dot(tri, la_hi, preferred_element_type=F32)
               + jnp.dot(tri, la_lo, preferred_element_type=F32))
        k_dec = k_ref[rows, :].astype(F32) * jnp.exp(cum[CHUNK - 1:CHUNK, :] - cum)
        for h in range(GLA_HEADS):
            kc = slice(h * GLA_DK, (h + 1) * GLA_DK)
            vc = slice(h * GLA_DV, (h + 1) * GLA_DV)
            tot = (lax.dot_general(la_hi[:, kc], ones, tn, preferred_element_type=F32)
                   + lax.dot_general(la_lo[:, kc], ones, tn, preferred_element_type=F32))
            decay = jnp.exp(tot)
            decay = jnp.concatenate([decay] * (GLA_DV // LANES), axis=1)
            kv = lax.dot_general(k_dec[:, kc].astype(BF16), v_ref[rows, vc], tn,
                                 preferred_element_type=F32)
            state = state_scr[h] * decay + kv
            state_scr[h] = state
            o = jnp.dot(q_ref[rows, kc], state.astype(BF16), preferred_element_type=F32) * q_scale
            o = o * lax.rsqrt(jnp.mean(o * o, axis=-1, keepdims=True) + EPS) * gn_ref[...]
            gt = g_ref[rows, vc].astype(F32)
            o_ref[rows, vc] = (o * (gt * jax.nn.sigmoid(gt))).astype(o_ref.dtype)


def gla_scan(proj, w_gate, b_gate, norm_g, *, batch, ts=512):
    T = proj.shape[0]
    S = T // batch
    nb = S // ts
    KD = GLA_HEADS * GLA_DK
    VD = GLA_HEADS * GLA_DV
    rows = lambda width, col: pl.BlockSpec((ts, width), lambda b, i: (b * nb + i, col))
    return pl.pallas_call(
        _gla_kernel,
        out_shape=jax.ShapeDtypeStruct((T, VD), BF16),
        grid=(batch, nb),
        in_specs=[
            rows(KD, 0), rows(KD, 1), rows(VD, 1), rows(VD, 2), rows(GLA_Z_COLS, (2 * KD + 2 * VD) // GLA_Z_COLS),
            pl.BlockSpec((GLA_Z_COLS, KD), lambda b, i: (0, 0)),
            pl.BlockSpec((1, KD), lambda b, i: (0, 0)),
            pl.BlockSpec((1, GLA_DV), lambda b, i: (0, 0)),
        ],
        out_specs=pl.BlockSpec((ts, VD), lambda b, i: (b * nb + i, 0)),
        scratch_shapes=[pltpu.VMEM((GLA_HEADS, GLA_DK, GLA_DV), F32)],
        compiler_params=_cparams(("parallel", "arbitrary")),
        name="gla_scan",
    )(proj, proj, proj, proj, proj, w_gate, b_gate.reshape(1, KD).astype(F32),
      norm_g.reshape(1, GLA_DV).astype(F32))


def kernel(x, rel_bias, ln_mix, ln_ffn, even_w_in, lam_q1, lam_k1, lam_q2, lam_k2, subln_g, conv_w, conv_b, conv_ln_g, conv_ln_b, even_w_out, odd_w_in, w_gate2, b_gate2, gla_norm_g, odd_w_out, peer_w_query, peer_sub_keys, peer_u, peer_v, ln_final):
    B, S, D = x.shape
    T = B * S
    xf = x.reshape(T, D)

    lam_init = 0.8 - 0.6 * math.exp(-0.3 * 0)
    lam = (jnp.exp(jnp.sum(lam_q1[0].astype(F32) * lam_k1[0].astype(F32)))
           - jnp.exp(jnp.sum(lam_q2[0].astype(F32) * lam_k2[0].astype(F32))) + lam_init)
    proj = norm_matmul(xf, ln_mix[0], even_w_in[0].astype(BF16)).reshape(B, S, -1)
    a_out = diff_attention(proj, lam, _attn_bias_tiles(rel_bias, ATTN_TILE), subln_g[0],
                           out_scale=1.0 - lam_init)
    c_out = conv_module(proj, conv_w[0], conv_b[0], conv_ln_g[0], conv_ln_b[0])
    a_width = A_HEADS * A_V_DIM
    w_out = even_w_out[0].astype(BF16)
    xf = proj_residual(xf, a_out.reshape(T, -1), c_out.reshape(T, -1), w_out[:a_width], w_out[a_width:])
    xf = peer_layer(xf, ln_ffn[0], peer_w_query[0], peer_sub_keys[0], peer_u[0], peer_v[0], ln_final,
                    final_norm=False)

    kd = GLA_HEADS * GLA_DK
    vd = GLA_HEADS * GLA_DV
    w_in = jnp.pad(odd_w_in[0], ((0, 0), (0, GLA_Z_COLS - GATE_RANK))).astype(BF16)
    w_gate = jnp.pad(w_gate2[0], ((0, GLA_Z_COLS - GATE_RANK), (0, 0))).astype(BF16)
    proj = norm_matmul(xf, ln_mix[1], w_in)
    og = gla_scan(proj, w_gate, b_gate2[0], gla_norm_g[0], batch=B)
    w_out = odd_w_out[0].astype(BF16)
    xf = proj_residual(xf, og, og, w_out[:vd // 2], w_out[vd // 2:], a_col=0, c_col=1)
    xf = peer_layer(xf, ln_ffn[1], peer_w_query[1], peer_sub_keys[1], peer_u[1], peer_v[1], ln_final,
                    final_norm=True)
    return xf.reshape(B, S, D)
```

```python
import functools
import math

import jax
import jax.numpy as jnp
import numpy as np
from jax import lax
from jax.experimental import pallas as pl
from jax.experimental.pallas import tpu as pltpu

F32 = jnp.float32
BF16 = jnp.bfloat16

EPS = 1e-6
LANES = 128
VMEM_LIMIT = 56 * 1024 * 1024

CHUNK = 64
N_BUCKETS = 32
MAX_DISTANCE = 128
A_HEADS = 4
A_QK_DIM = 64
A_V_DIM = 128
CONV_WIDTH = 31
GLA_HEADS = 4
GATE_RANK = 16
GATE_TAU = 16.0
PEER_HEADS = 8
N_KEYS = 128
PEER_TOPK = 16
PEER_HALF = 128
NEG_BIG = -1e30


def _cparams(sem):
    return pltpu.CompilerParams(dimension_semantics=sem, vmem_limit_bytes=VMEM_LIMIT)


def _rms_normed(x, g):
    ms = jnp.mean(x * x, axis=-1, keepdims=True)
    return x * lax.rsqrt(ms + EPS) * g


def _norm_matmul_kernel(x_ref, g_ref, w_ref, o_ref):
    xn = _rms_normed(x_ref[...], g_ref[...]).astype(BF16)
    o_ref[...] = jnp.dot(xn, w_ref[...], preferred_element_type=F32).astype(o_ref.dtype)


def norm_matmul(x, g, w, *, tm=512, out_dtype=BF16):
    T, D = x.shape
    N = w.shape[1]
    return pl.pallas_call(
        _norm_matmul_kernel,
        out_shape=jax.ShapeDtypeStruct((T, N), out_dtype),
        grid=(T // tm,),
        in_specs=[
            pl.BlockSpec((tm, D), lambda i: (i, 0)),
            pl.BlockSpec((1, D), lambda i: (0, 0)),
            pl.BlockSpec((D, N), lambda i: (0, 0)),
        ],
        out_specs=pl.BlockSpec((tm, N), lambda i: (i, 0)),
        compiler_params=_cparams(("parallel",)),
        name="norm_matmul",
    )(x, g.reshape(1, D).astype(F32), w)


def _proj_residual_kernel(x_ref, a_ref, c_ref, wa_ref, wc_ref, o_ref):
    acc = jnp.dot(a_ref[...], wa_ref[...], preferred_element_type=F32)
    acc = acc + jnp.dot(c_ref[...], wc_ref[...], preferred_element_type=F32)
    o_ref[...] = x_ref[...] + acc


def proj_residual(x, a, c, wa, wc, *, a_col=0, c_col=0, tm=512):
    T, D = x.shape
    Ka, Kc = wa.shape[0], wc.shape[0]
    return pl.pallas_call(
        _proj_residual_kernel,
        out_shape=jax.ShapeDtypeStruct((T, D), F32),
        grid=(T // tm,),
        in_specs=[
            pl.BlockSpec((tm, D), lambda i: (i, 0)),
            pl.BlockSpec((tm, Ka), lambda i: (i, a_col)),
            pl.BlockSpec((tm, Kc), lambda i: (i, c_col)),
            pl.BlockSpec((Ka, D), lambda i: (0, 0)),
            pl.BlockSpec((Kc, D), lambda i: (0, 0)),
        ],
        out_specs=pl.BlockSpec((tm, D), lambda i: (i, 0)),
        compiler_params=_cparams(("parallel",)),
        name="proj_residual",
    )(x, a, c, wa, wc)


def _pair_candidates():
    return [(a, b) for a in range(PEER_TOPK) for b in range(PEER_TOPK) if (a + 1) * (b + 1) <= PEER_TOPK]


def _dup_bf16_bits(v):
    hi = pltpu.bitcast(v.astype(BF16).astype(F32), jnp.uint32)
    return hi | (hi >> 16)


def _pack_bf16_pairs(even, odd):
    be = pltpu.bitcast(even.astype(BF16).astype(F32), jnp.uint32)
    bo = pltpu.bitcast(odd.astype(BF16).astype(F32), jnp.uint32)
    return (be >> 16) | bo


def _route_kernel(x_ref, g_ref, wqT_ref, sk_ref, rank2_ref, e2_ref, r1_ref, e1z_ref,
                  xn_scr, s_scr, rank_scr, top_scr, r1top_scr, stat_scr):
    tm = x_ref.shape[0]
    n_groups = tm // LANES
    xn_scr[...] = _rms_normed(x_ref[...], g_ref[...]).astype(BF16)

    def score_body(h, carry):
        w_h = wqT_ref[pl.ds(pl.multiple_of(h * 2 * PEER_HALF, 2 * PEER_HALF), 2 * PEER_HALF), :]
        qT = lax.dot_general(w_h, xn_scr[...], (((1,), (1,)), ((), ())),
                             preferred_element_type=F32).astype(BF16)
        for p in range(2):
            s = jnp.dot(sk_ref[2 * h + p], qT[p * PEER_HALF:(p + 1) * PEER_HALF, :],
                        preferred_element_type=F32)
            for grp in range(n_groups):
                s_scr[2 * h + p, grp] = s[:, grp * LANES:(grp + 1) * LANES]
        return carry

    lax.fori_loop(0, PEER_HEADS, score_body, 0)

    def top_body(hp, carry):
        for grp in range(n_groups):
            work = s_scr[hp, grp]
            rank = jnp.full(work.shape, float(PEER_TOPK), F32)
            for k in range(PEER_TOPK):
                m = jnp.max(work, axis=0, keepdims=True)
                eq = work == m
                rank = jnp.where(eq, float(k), rank)
                work = jnp.where(eq, -jnp.inf, work)
                top_scr[hp, k, grp:grp + 1, :] = m
            rank_scr[hp, grp] = rank
        return carry

    lax.fori_loop(0, 2 * PEER_HEADS, top_body, 0)

    cands = _pair_candidates()

    def pair_body(hh, carry):
        h = 2 * hh

        def both(scr, idx, k):
            return jnp.concatenate([scr[idx(h), k], scr[idx(h + 1), k]], axis=0)

        v1 = [both(top_scr, lambda x: 2 * x, a) for a in range(PEER_TOPK)]
        v2 = [both(top_scr, lambda x: 2 * x + 1, b) for b in range(PEER_TOPK)]
        sums = [v1[a] + v2[b] for (a, b) in cands]
        work = list(sums)
        tau = None
        for k in range(PEER_TOPK):
            tau = functools.reduce(jnp.maximum, work)
            if k + 1 < PEER_TOPK:
                work = [jnp.where(w == tau, -jnp.inf, w) for w in work]
        cmax = v1[0] + v2[0]
        z = jnp.zeros_like(tau)
        r1 = [jnp.zeros_like(tau) for _ in range(PEER_TOPK)]
        for (a, b), c in zip(cands, sums):
            sel = c >= tau
            z = z + jnp.where(sel, jnp.exp(c - cmax), 0.0)
            r1[a] = r1[a] + jnp.where(sel, 1.0, 0.0)
        inv_z = 1.0 / z
        for d in range(2):
            part = slice(d * n_groups, (d + 1) * n_groups)
            for a in range(PEER_TOPK):
                r1top_scr[h + d, a] = r1[a][part]
            stat_scr[h + d, 0] = v1[0][part]
            stat_scr[h + d, 1] = v2[0][part]
            stat_scr[h + d, 2] = inv_z[part]
        return carry

    lax.fori_loop(0, PEER_HEADS // 2, pair_body, 0)

    even_rows = pl.ds(0, N_KEYS // 2, stride=2)
    odd_rows = pl.ds(1, N_KEYS // 2, stride=2)
    def table_body(h, carry):
        for grp in range(n_groups):
            lanes = slice(grp * LANES, (grp + 1) * LANES)
            max1 = stat_scr[h, 0, grp:grp + 1, :]
            max2 = stat_scr[h, 1, grp:grp + 1, :]
            inv_z = stat_scr[h, 2, grp:grp + 1, :]
            rank1 = rank_scr[2 * h, grp]
            r1_key = jnp.zeros(rank1.shape, F32)
            for a in range(PEER_TOPK):
                r1_key = jnp.where(rank1 == float(a), r1top_scr[h, a, grp:grp + 1, :], r1_key)
            r1_ref[h, :, lanes] = _dup_bf16_bits(r1_key)
            e1z = jnp.exp(s_scr[2 * h, grp] - max1) * inv_z
            e1z_ref[h, :, lanes] = _dup_bf16_bits(e1z)
            rk_e = rank_scr[2 * h + 1, grp, even_rows, :]
            rk_o = rank_scr[2 * h + 1, grp, odd_rows, :]
            rank2_ref[h, :, lanes] = _pack_bf16_pairs(rk_e, rk_o)
            e2_e = jnp.exp(jnp.minimum(s_scr[2 * h + 1, grp, even_rows, :] - max2, 0.0))
            e2_o = jnp.exp(jnp.minimum(s_scr[2 * h + 1, grp, odd_rows, :] - max2, 0.0))
            e2_ref[h, :, lanes] = _pack_bf16_pairs(e2_e, e2_o)
        return carry

    lax.fori_loop(0, PEER_HEADS, table_body, 0)


def peer_route(x, g, wqT, sk, *, tm=512):
    T, D = x.shape
    H = PEER_HEADS
    n_groups = tm // LANES
    tab = lambda rows: jax.ShapeDtypeStruct((H, rows, T), jnp.uint32)
    tab_spec = lambda rows: pl.BlockSpec((H, rows, tm), lambda i: (0, 0, i))
    return pl.pallas_call(
        _route_kernel,
        out_shape=(tab(N_KEYS // 2), tab(N_KEYS // 2), tab(N_KEYS), tab(N_KEYS)),
        grid=(T // tm,),
        in_specs=[
            pl.BlockSpec((tm, D), lambda i: (i, 0)),
            pl.BlockSpec((1, D), lambda i: (0, 0)),
            pl.BlockSpec((H * 2 * PEER_HALF, D), lambda i: (0, 0)),
            pl.BlockSpec((2 * H, N_KEYS, PEER_HALF), lambda i: (0, 0, 0)),
        ],
        out_specs=(tab_spec(N_KEYS // 2), tab_spec(N_KEYS // 2), tab_spec(N_KEYS), tab_spec(N_KEYS)),
        scratch_shapes=[
            pltpu.VMEM((tm, D), BF16),
            pltpu.VMEM((2 * H, n_groups, N_KEYS, LANES), F32),
            pltpu.VMEM((2 * H, n_groups, N_KEYS, LANES), F32),
            pltpu.VMEM((2 * H, PEER_TOPK, n_groups, LANES), F32),
            pltpu.VMEM((H, PEER_TOPK, n_groups, LANES), F32),
            pltpu.VMEM((H, 3, n_groups, LANES), F32),
        ],
        compiler_params=_cparams(("parallel",)),
        name="peer_route",
    )(x, g.reshape(1, D).astype(F32), wqT, sk)


GELU_C0 = math.sqrt(2.0 / math.pi)
GELU_C1 = 0.044715


def _gelu_tanh(a):
    inner = GELU_C0 * (a + GELU_C1 * (a * a * a))
    return 0.5 * a * (1.0 + jnp.tanh(inner))


PEER_SUB = 512


def _peer_main_kernel(x_ref, g_ref, rank2_ref, e2_ref, r1_ref, e1z_ref, u_ref, vt_ref, gf_ref, o_ref,
                      xn_scr, acc_scr, act_scr, coef_scr, *, final_norm):
    j = pl.program_id(1)
    tm = x_ref.shape[0]
    n_i1 = r1_ref.shape[1]
    rows = 16
    words = rows * jnp.dtype(BF16).itemsize // 4

    @pl.when(j == 0)
    def _():
        xn_scr[...] = _rms_normed(x_ref[...], g_ref[...]).astype(BF16)
        acc_scr[...] = jnp.zeros_like(acc_scr)

    eb = 2 * u_ref.shape[0]
    d_model = 2 * vt_ref.shape[0]
    tn = 2 * LANES

    def stage_a(n, q):
        r = slice(q * PEER_SUB, (q + 1) * PEER_SUB)
        rw = slice(q * PEER_SUB // 2, (q + 1) * PEER_SUB // 2)
        c = slice(n * tn, (n + 1) * tn)
        act_scr[n, r, :] = lax.dot_general(pltpu.bitcast(u_ref[rw, :], BF16), xn_scr[c, :],
                                           (((1,), (1,)), ((), ())), preferred_element_type=F32)

    def stage_b(n, q):
        for sub in range(tn // LANES):
            lanes = slice((n * tn // LANES + sub) * LANES, (n * tn // LANES + sub + 1) * LANES)
            sl = slice(sub * LANES, (sub + 1) * LANES)
            for i1 in range(q * PEER_SUB // N_KEYS, (q + 1) * PEER_SUB // N_KEYS, 2):
                r1b, e1b = {}, {}
                for d in range(2):
                    for h in range(PEER_HEADS):
                        r1row = jnp.broadcast_to(r1_ref[h, i1 + d:i1 + d + 1, lanes], (words, LANES))
                        e1row = jnp.broadcast_to(e1z_ref[h, i1 + d:i1 + d + 1, lanes], (words, LANES))
                        r1b[d, h] = pltpu.bitcast(r1row, BF16)
                        e1b[d, h] = pltpu.bitcast(e1row, BF16)
                for blk in range(N_KEYS // rows):
                    i2w = slice(blk * words, (blk + 1) * words)
                    gate = [None, None]
                    for h in range(PEER_HEADS):
                        rk = pltpu.bitcast(rank2_ref[h, i2w, lanes], BF16)
                        ev = pltpu.bitcast(e2_ref[h, i2w, lanes], BF16)
                        for d in range(2):
                            term = jnp.where(rk < r1b[d, h], ev, jnp.zeros((), BF16)) * e1b[d, h]
                            gate[d] = term if gate[d] is None else gate[d] + term
                    for d in range(2):
                        r0 = (i1 + d) * N_KEYS + blk * rows
                        a = act_scr[n, r0:r0 + rows, sl].astype(BF16)
                        coef_scr[n, r0:r0 + rows, sl] = _gelu_tanh(a) * gate[d]

    def stage_c(n):
        for half in range(2):
            r = slice(half * (d_model // 2), (half + 1) * (d_model // 2))
            rw = slice(half * (d_model // 4), (half + 1) * (d_model // 4))
            acc_scr[n, r, :] += jnp.dot(pltpu.bitcast(vt_ref[rw, :], BF16), coef_scr[n],
                                        preferred_element_type=F32)

    for n in range(tm // tn):
        for q in range(eb // PEER_SUB):
            stage_a(n, q)
            stage_b(n, q)
        stage_c(n)

    @pl.when(j == pl.num_programs(1) - 1)
    def _():
        acc_t = jnp.concatenate([acc_scr[n] for n in range(tm // tn)], axis=1)
        y = x_ref[...] + acc_t.T
        if final_norm:
            y = _rms_normed(y, gf_ref[...])
        o_ref[...] = y


def _pack_row_pairs(w):
    r, c = w.shape
    return lax.bitcast_convert_type(w.reshape(r // 2, 2, c).transpose(0, 2, 1), jnp.uint32)


def peer_main(x, g, tables, u, vt, g_final, *, final_norm, tm=512, eb=2048):
    T, D = x.shape
    E = 2 * u.shape[0]
    H = PEER_HEADS
    rank2, e2, r1, e1z = tables
    n_i1 = eb // N_KEYS
    full_tab = pl.BlockSpec((H, N_KEYS // 2, tm), lambda i, j: (0, 0, i))
    i1_tab = pl.BlockSpec((H, n_i1, tm), lambda i, j: (0, j, i))
    return pl.pallas_call(
        functools.partial(_peer_main_kernel, final_norm=final_norm),
        out_shape=jax.ShapeDtypeStruct((T, D), F32),
        grid=(T // tm, E // eb),
        in_specs=[
            pl.BlockSpec((tm, D), lambda i, j: (i, 0)),
            pl.BlockSpec((1, D), lambda i, j: (0, 0)),
            full_tab, full_tab, i1_tab, i1_tab,
            pl.BlockSpec((eb // 2, D), lambda i, j: (j, 0)),
            pl.BlockSpec((D // 2, eb), lambda i, j: (0, j)),
            pl.BlockSpec((1, D), lambda i, j: (0, 0)),
        ],
        out_specs=pl.BlockSpec((tm, D), lambda i, j: (i, 0)),
        scratch_shapes=[
            pltpu.VMEM((tm, D), BF16),
            pltpu.VMEM((tm // (2 * LANES), D, 2 * LANES), F32),
            pltpu.VMEM((tm // (2 * LANES), eb, 2 * LANES), F32),
            pltpu.VMEM((tm // (2 * LANES), eb, 2 * LANES), BF16),
        ],
        compiler_params=_cparams(("parallel", "arbitrary")),
        name="peer_main",
    )(x, g.reshape(1, D).astype(F32), rank2, e2, r1, e1z, u, vt, g_final.reshape(1, D).astype(F32))


def peer_layer(x, g, w_query, sub_keys, expert_u, expert_v, g_final, *, final_norm):
    wqT = w_query.T.astype(BF16)
    sk = sub_keys.reshape(2 * PEER_HEADS, N_KEYS, PEER_HALF).astype(BF16)
    tables = peer_route(x, g, wqT, sk)
    return peer_main(x, g, tables, _pack_row_pairs(expert_u.astype(BF16)),
                     _pack_row_pairs(expert_v.T.astype(BF16)), g_final, final_norm=final_norm)


ATTN_TILE = 256


def _attn_kernel(lam_ref, q_ref, k_ref, v_ref, bias_ref, gsub_ref, o_ref, vt_scr, m_scr, l_scr, acc_scr,
                 s0_scr, s1_scr, p0_scr, p1_scr, *, out_scale):
    i = pl.program_id(2)
    tq = q_ref.shape[1]
    tk = tq

    @pl.when(i == 0)
    def _():
        vt_scr[...] = v_ref[0].T

    q = q_ref[0]
    lane = lax.broadcasted_iota(jnp.int32, q.shape, 1)
    scale = jnp.asarray(A_QK_DIM ** -0.5, BF16)
    zero = jnp.zeros((), BF16)
    qs = jnp.concatenate([jnp.where(lane < A_QK_DIM, q, zero) * scale,
                          jnp.where(lane >= A_QK_DIM, q, zero) * scale], axis=0)

    m_scr[...] = jnp.full(m_scr.shape, NEG_BIG, F32)
    l_scr[...] = jnp.zeros(l_scr.shape, F32)
    acc_scr[...] = jnp.zeros(acc_scr.shape, F32)

    def scores(j):
        k = k_ref[0, pl.ds(pl.multiple_of(j * tk, tk), tk), :]
        return lax.dot_general(k, qs, (((1,), (1,)), ((), ())), preferred_element_type=F32)

    def pv(j, p_ref):
        vt = vt_scr[:, pl.ds(pl.multiple_of(j * tk, tk), tk)]
        return jnp.dot(vt, p_ref[...], preferred_element_type=F32)

    def consume(j, bias, cur, nxt, prefetch):
        s_cur, p_cur = cur
        s_nxt, p_prev = nxt
        if prefetch:
            s_nxt[...] = scores(j + 1)
        pv_prev = pv(jnp.maximum(j - 1, 0), p_prev)
        s = s_cur[...]
        if bias is not None:
            s = s + jnp.concatenate([bias, bias], axis=1)
        m_prev = m_scr[...]
        m_new = jnp.maximum(m_prev, jnp.max(s, axis=0, keepdims=True))
        alpha = jnp.exp(m_prev - m_new)
        p = jnp.exp(s - m_new)
        p_cur[...] = p.astype(BF16)
        l_scr[...] = alpha * l_scr[...] + jnp.sum(p, axis=0, keepdims=True)
        acc_scr[...] = alpha * (acc_scr[...] + pv_prev)
        m_scr[...] = m_new

    def by_parity(j, fn):
        @pl.when(j % 2 == 0)
        def _():
            fn((s0_scr, p0_scr), (s1_scr, p1_scr))

        @pl.when(j % 2 == 1)
        def _():
            fn((s1_scr, p1_scr), (s0_scr, p0_scr))

    s0_scr[...] = scores(0)
    p1_scr[...] = jnp.zeros(p1_scr.shape, BF16)

    def far_body(j, carry):
        by_parity(j, lambda cur, nxt: consume(j, None, cur, nxt, True))
        return carry

    lax.fori_loop(0, i - 1, far_body, 0)

    @pl.when(i >= 1)
    def _():
        by_parity(i - 1, lambda cur, nxt: consume(i - 1, bias_ref[0, 0], cur, nxt, True))

    def last(cur, nxt):
        consume(i, bias_ref[0, 1], cur, nxt, False)
        acc_scr[...] += pv(i, cur[1])

    by_parity(i, last)

    ot = acc_scr[...] / l_scr[...]
    ot = ot[:, :tq] - lam_ref[0] * ot[:, tq:]
    ot = ot * lax.rsqrt(jnp.mean(ot * ot, axis=0, keepdims=True) + EPS)
    o_ref[0] = (ot.T * (gsub_ref[...] * out_scale)).astype(o_ref.dtype)


def _t5_bucket(rel):
    nb = N_BUCKETS // 2
    max_exact = nb // 2
    ret = jnp.where(rel > 0, nb, 0)
    n = jnp.abs(rel)
    nf = jnp.maximum(n, 1).astype(F32)
    large = max_exact + (jnp.log(nf / max_exact) / math.log(MAX_DISTANCE / max_exact)
                         * (nb - max_exact)).astype(jnp.int32)
    large = jnp.minimum(large, nb - 1)
    return ret + jnp.where(n < max_exact, n, large)


def _attn_bias_tiles(rel_bias, tile):
    r = jnp.arange(tile)[None, :]
    c = jnp.arange(tile)[:, None]
    rb = rel_bias.astype(F32)

    def lookup(bucket):
        return jnp.einsum("...b,bh->...h", jax.nn.one_hot(bucket, N_BUCKETS, dtype=F32), rb,
                          precision=lax.Precision.HIGHEST)

    far = lookup(_t5_bucket(jnp.asarray(-(2 * tile))))
    prev = lookup(_t5_bucket(c - r - tile)) - far
    diag = lookup(_t5_bucket(c - r)) - far
    diag = jnp.where(((c // CHUNK) <= (r // CHUNK))[..., None], diag, NEG_BIG)
    return jnp.transpose(jnp.stack([prev, diag], axis=0), (3, 0, 1, 2))


def diff_attention(proj, lam, bias_tiles, subln_g, *, out_scale):
    B, S, _ = proj.shape
    H = A_HEADS
    tq = ATTN_TILE
    return pl.pallas_call(
        functools.partial(_attn_kernel, out_scale=out_scale),
        out_shape=jax.ShapeDtypeStruct((B, S, H * A_V_DIM), BF16),
        grid=(B, H, S // tq),
        in_specs=[
            pl.BlockSpec(memory_space=pltpu.SMEM),
            pl.BlockSpec((1, tq, A_V_DIM), lambda b, h, i: (b, i, h)),
            pl.BlockSpec((1, S, A_V_DIM), lambda b, h, i: (b, 0, H + h)),
            pl.BlockSpec((1, S, A_V_DIM), lambda b, h, i: (b, 0, 2 * H + h)),
            pl.BlockSpec((1, 2, tq, tq), lambda b, h, i: (h, 0, 0, 0)),
            pl.BlockSpec((1, A_V_DIM), lambda b, h, i: (0, 0)),
        ],
        out_specs=pl.BlockSpec((1, tq, A_V_DIM), lambda b, h, i: (b, i, h)),
        scratch_shapes=[
            pltpu.VMEM((A_V_DIM, S), BF16),
            pltpu.VMEM((1, 2 * tq), F32),
            pltpu.VMEM((1, 2 * tq), F32),
            pltpu.VMEM((A_V_DIM, 2 * tq), F32),
            pltpu.VMEM((tq, 2 * tq), F32),
            pltpu.VMEM((tq, 2 * tq), F32),
            pltpu.VMEM((tq, 2 * tq), BF16),
            pltpu.VMEM((tq, 2 * tq), BF16),
        ],
        compiler_params=_cparams(("parallel", "parallel", "arbitrary")),
        name="diff_attention",
    )(lam.reshape(1).astype(F32), proj, proj, proj, bias_tiles, subln_g.reshape(1, A_V_DIM).astype(F32))


CONV_HALO = 32
CONV_ROWS = 64


def _conv_kernel(val_ref, gate_ref, hval_ref, hgate_ref, w_ref, b_ref, g_ref, beta_ref, o_ref, buf_scr):
    i = pl.program_id(1)
    ts = val_ref.shape[1]

    def glu(v, gt):
        return v.astype(F32) * jax.nn.sigmoid(gt.astype(F32))

    halo = glu(hval_ref[0], hgate_ref[0])
    buf_scr[0:CONV_HALO, :] = jnp.where(i > 0, halo, 0.0)
    buf_scr[CONV_HALO:, :] = glu(val_ref[0], gate_ref[0])

    first = CONV_HALO - (CONV_WIDTH - 1)
    for r in range(ts // CONV_ROWS):
        acc = None
        for w in range(CONV_WIDTH):
            lo = r * CONV_ROWS + first + w
            term = buf_scr[lo:lo + CONV_ROWS, :] * w_ref[w:w + 1, :]
            acc = term if acc is None else acc + term
        y = acc + b_ref[...]
        mu = jnp.mean(y, axis=-1, keepdims=True)
        d = y - mu
        var = jnp.mean(d * d, axis=-1, keepdims=True)
        y = d * lax.rsqrt(var + EPS) * g_ref[...] + beta_ref[...]
        y = y * jax.nn.sigmoid(y)
        o_ref[0, r * CONV_ROWS:(r + 1) * CONV_ROWS, :] = y.astype(o_ref.dtype)


def conv_module(proj, conv_w, conv_b, ln_g, ln_b, *, ts=512):
    B, S, _ = proj.shape
    C = conv_w.shape[1]
    hb = ts // CONV_HALO
    row = lambda a: a.reshape(1, C).astype(F32)
    cur = lambda col: pl.BlockSpec((1, ts, C), lambda b, i: (b, i, col))
    halo = lambda col: pl.BlockSpec((1, CONV_HALO, C), lambda b, i: (b, jnp.maximum(i * hb - 1, 0), col))
    vec = pl.BlockSpec((1, C), lambda b, i: (0, 0))
    return pl.pallas_call(
        _conv_kernel,
        out_shape=jax.ShapeDtypeStruct((B, S, C), BF16),
        grid=(B, S // ts),
        in_specs=[cur(3), cur(4), halo(3), halo(4),
                  pl.BlockSpec((CONV_WIDTH + 1, C), lambda b, i: (0, 0)), vec, vec, vec],
        out_specs=pl.BlockSpec((1, ts, C), lambda b, i: (b, i, 0)),
        scratch_shapes=[pltpu.VMEM((ts + CONV_HALO, C), F32)],
        compiler_params=_cparams(("parallel", "parallel")),
        name="conv_module",
    )(proj, proj, proj, proj, jnp.pad(conv_w.astype(F32), ((0, 1), (0, 0))), row(conv_b), row(ln_g), row(ln_b))


GLA_DK = 128
GLA_DV = 256
GLA_Z_COLS = 128


def _split_bf16(x):
    hi = x.astype(BF16)
    lo = (x - hi.astype(F32)).astype(BF16)
    return hi, lo


def _gla_kernel(q_ref, k_ref, v_ref, g_ref, z_ref, wg_ref, bg_ref, gn_ref, o_ref, state_scr):
    ts = q_ref.shape[0]
    n_chunks = ts // CHUNK

    @pl.when(pl.program_id(1) == 0)
    def _():
        state_scr[...] = jnp.zeros(state_scr.shape, F32)

    zz = jnp.dot(z_ref[...], wg_ref[...], preferred_element_type=F32) + bg_ref[...]
    log_a = (jnp.minimum(zz, 0.0) - jnp.log1p(jnp.exp(-jnp.abs(zz)))) / GATE_TAU

    row = lax.broadcasted_iota(jnp.int32, (CHUNK, CHUNK), 0)
    col = lax.broadcasted_iota(jnp.int32, (CHUNK, CHUNK), 1)
    tri = jnp.where(col <= row, 1.0, 0.0).astype(BF16)
    ones = jnp.ones((CHUNK, LANES), BF16)
    tn = (((0,), (0,)), ((), ()))
    q_scale = GLA_DK ** -0.5

    for c in range(n_chunks):
        rows = slice(c * CHUNK, (c + 1) * CHUNK)
        la_hi, la_lo = _split_bf16(log_a[rows, :])
        cum = (jnp.dot(tri, la_hi, preferred_element_type=F32)
               + jnp.dot(tri, la_lo, preferred_element_type=F32))
        k_dec = k_ref[rows, :].astype(F32) * jnp.exp(cum[CHUNK - 1:CHUNK, :] - cum)
        for h in range(GLA_HEADS):
            kc = slice(h * GLA_DK, (h + 1) * GLA_DK)
            vc = slice(h * GLA_DV, (h + 1) * GLA_DV)
            tot = (lax.dot_general(la_hi[:, kc], ones, tn, preferred_element_type=F32)
                   + lax.dot_general(la_lo[:, kc], ones, tn, preferred_element_type=F32))
            decay = jnp.exp(tot)
            decay = jnp.concatenate([decay] * (GLA_DV // LANES), axis=1)
            kv = lax.dot_general(k_dec[:, kc].astype(BF16), v_ref[rows, vc], tn,
                                 preferred_element_type=F32)
            state = state_scr[h] * decay + kv
            state_scr[h] = state
            o = jnp.dot(q_ref[rows, kc], state.astype(BF16), preferred_element_type=F32) * q_scale
            o = o * lax.rsqrt(jnp.mean(o * o, axis=-1, keepdims=True) + EPS) * gn_ref[...]
            gt = g_ref[rows, vc].astype(F32)
            o_ref[rows, vc] = (o * (gt * jax.nn.sigmoid(gt))).astype(o_ref.dtype)


def gla_scan(proj, w_gate, b_gate, norm_g, *, batch, ts=512):
    T = proj.shape[0]
    S = T // batch
    nb = S // ts
    KD = GLA_HEADS * GLA_DK
    VD = GLA_HEADS * GLA_DV
    rows = lambda width, col: pl.BlockSpec((ts, width), lambda b, i: (b * nb + i, col))
    return pl.pallas_call(
        _gla_kernel,
        out_shape=jax.ShapeDtypeStruct((T, VD), BF16),
        grid=(batch, nb),
        in_specs=[
            rows(KD, 0), rows(KD, 1), rows(VD, 1), rows(VD, 2), rows(GLA_Z_COLS, (2 * KD + 2 * VD) // GLA_Z_COLS),
            pl.BlockSpec((GLA_Z_COLS, KD), lambda b, i: (0, 0)),
            pl.BlockSpec((1, KD), lambda b, i: (0, 0)),
            pl.BlockSpec((1, GLA_DV), lambda b, i: (0, 0)),
        ],
        out_specs=pl.BlockSpec((ts, VD), lambda b, i: (b * nb + i, 0)),
        scratch_shapes=[pltpu.VMEM((GLA_HEADS, GLA_DK, GLA_DV), F32)],
        compiler_params=_cparams(("parallel", "arbitrary")),
        name="gla_scan",
    )(proj, proj, proj, proj, proj, w_gate, b_gate.reshape(1, KD).astype(F32),
      norm_g.reshape(1, GLA_DV).astype(F32))


def kernel(x, rel_bias, ln_mix, ln_ffn, even_w_in, lam_q1, lam_k1, lam_q2, lam_k2, subln_g, conv_w, conv_b, conv_ln_g, conv_ln_b, even_w_out, odd_w_in, w_gate2, b_gate2, gla_norm_g, odd_w_out, peer_w_query, peer_sub_keys, peer_u, peer_v, ln_final):
    B, S, D = x.shape
    T = B * S
    xf = x.reshape(T, D)

    lam_init = 0.8 - 0.6 * math.exp(-0.3 * 0)
    lam = (jnp.exp(jnp.sum(lam_q1[0].astype(F32) * lam_k1[0].astype(F32)))
           - jnp.exp(jnp.sum(lam_q2[0].astype(F32) * lam_k2[0].astype(F32))) + lam_init)
    proj = norm_matmul(xf, ln_mix[0], even_w_in[0].astype(BF16)).reshape(B, S, -1)
    a_out = diff_attention(proj, lam, _attn_bias_tiles(rel_bias, ATTN_TILE), subln_g[0],
                           out_scale=1.0 - lam_init)
    c_out = conv_module(proj, conv_w[0], conv_b[0], conv_ln_g[0], conv_ln_b[0])
    a_width = A_HEADS * A_V_DIM
    w_out = even_w_out[0].astype(BF16)
    xf = proj_residual(xf, a_out.reshape(T, -1), c_out.reshape(T, -1), w_out[:a_width], w_out[a_width:])
    xf = peer_layer(xf, ln_ffn[0], peer_w_query[0], peer_sub_keys[0], peer_u[0], peer_v[0], ln_final,
                    final_norm=False)

    kd = GLA_HEADS * GLA_DK
    vd = GLA_HEADS * GLA_DV
    w_in = jnp.pad(odd_w_in[0], ((0, 0), (0, GLA_Z_COLS - GATE_RANK))).astype(BF16)
    w_gate = jnp.pad(w_gate2[0], ((0, GLA_Z_COLS - GATE_RANK), (0, 0))).astype(BF16)
    proj = norm_matmul(xf, ln_mix[1], w_in)
    og = gla_scan(proj, w_gate, b_gate2[0], gla_norm_g[0], batch=B)
    w_out = odd_w_out[0].astype(BF16)
    xf = proj_residual(xf, og, og, w_out[:vd // 2], w_out[vd // 2:], a_col=0, c_col=1)
    xf = peer_layer(xf, ln_ffn[1], peer_w_query[1], peer_sub_keys[1], peer_u[1], peer_v[1], ln_final,
                    final_norm=True)
    return xf.reshape(B, S, D)
```

```python
import functools
import math

import jax
import jax.numpy as jnp
import numpy as np
from jax import lax
from jax.experimental import pallas as pl
from jax.experimental.pallas import tpu as pltpu

F32 = jnp.float32
BF16 = jnp.bfloat16

EPS = 1e-6
LANES = 128
VMEM_LIMIT = 56 * 1024 * 1024

CHUNK = 64
N_BUCKETS = 32
MAX_DISTANCE = 128
A_HEADS = 4
A_QK_DIM = 64
A_V_DIM = 128
CONV_WIDTH = 31
GLA_HEADS = 4
GATE_RANK = 16
GATE_TAU = 16.0
PEER_HEADS = 8
N_KEYS = 128
PEER_TOPK = 16
PEER_HALF = 128
NEG_BIG = -1e30


def _cparams(sem):
    return pltpu.CompilerParams(dimension_semantics=sem, vmem_limit_bytes=VMEM_LIMIT)


def _rms_normed(x, g):
    ms = jnp.mean(x * x, axis=-1, keepdims=True)
    return x * lax.rsqrt(ms + EPS) * g


def _norm_matmul_kernel(x_ref, g_ref, w_ref, o_ref):
    xn = _rms_normed(x_ref[...], g_ref[...]).astype(BF16)
    o_ref[...] = jnp.dot(xn, w_ref[...], preferred_element_type=F32).astype(o_ref.dtype)


def norm_matmul(x, g, w, *, tm=512, out_dtype=BF16):
    T, D = x.shape
    N = w.shape[1]
    return pl.pallas_call(
        _norm_matmul_kernel,
        out_shape=jax.ShapeDtypeStruct((T, N), out_dtype),
        grid=(T // tm,),
        in_specs=[
            pl.BlockSpec((tm, D), lambda i: (i, 0)),
            pl.BlockSpec((1, D), lambda i: (0, 0)),
            pl.BlockSpec((D, N), lambda i: (0, 0)),
        ],
        out_specs=pl.BlockSpec((tm, N), lambda i: (i, 0)),
        compiler_params=_cparams(("parallel",)),
        name="norm_matmul",
    )(x, g.reshape(1, D).astype(F32), w)


def _proj_residual_kernel(x_ref, a_ref, c_ref, wa_ref, wc_ref, o_ref):
    acc = jnp.dot(a_ref[...], wa_ref[...], preferred_element_type=F32)
    acc = acc + jnp.dot(c_ref[...], wc_ref[...], preferred_element_type=F32)
    o_ref[...] = x_ref[...] + acc


def proj_residual(x, a, c, wa, wc, *, a_col=0, c_col=0, tm=512):
    T, D = x.shape
    Ka, Kc = wa.shape[0], wc.shape[0]
    return pl.pallas_call(
        _proj_residual_kernel,
        out_shape=jax.ShapeDtypeStruct((T, D), F32),
        grid=(T // tm,),
        in_specs=[
            pl.BlockSpec((tm, D), lambda i: (i, 0)),
            pl.BlockSpec((tm, Ka), lambda i: (i, a_col)),
            pl.BlockSpec((tm, Kc), lambda i: (i, c_col)),
            pl.BlockSpec((Ka, D), lambda i: (0, 0)),
            pl.BlockSpec((Kc, D), lambda i: (0, 0)),
        ],
        out_specs=pl.BlockSpec((tm, D), lambda i: (i, 0)),
        compiler_params=_cparams(("parallel",)),
        name="proj_residual",
    )(x, a, c, wa, wc)


def _pair_candidates():
    return [(a, b) for a in range(PEER_TOPK) for b in range(PEER_TOPK) if (a + 1) * (b + 1) <= PEER_TOPK]


def _dup_bf16_bits(v):
    hi = pltpu.bitcast(v.astype(BF16).astype(F32), jnp.uint32)
    return hi | (hi >> 16)


def _pack_bf16_pairs(even, odd):
    be = pltpu.bitcast(even.astype(BF16).astype(F32), jnp.uint32)
    bo = pltpu.bitcast(odd.astype(BF16).astype(F32), jnp.uint32)
    return (be >> 16) | bo


def _route_kernel(x_ref, g_ref, wqT_ref, sk_ref, rank2_ref, e2_ref, r1_ref, e1z_ref,
                  xn_scr, s_scr, rank_scr, top_scr, r1top_scr, stat_scr):
    tm = x_ref.shape[0]
    n_groups = tm // LANES
    xn_scr[...] = _rms_normed(x_ref[...], g_ref[...]).astype(BF16)

    def score_body(h, carry):
        w_h = wqT_ref[pl.ds(pl.multiple_of(h * 2 * PEER_HALF, 2 * PEER_HALF), 2 * PEER_HALF), :]
        qT = lax.dot_general(w_h, xn_scr[...], (((1,), (1,)), ((), ())),
                             preferred_element_type=F32).astype(BF16)
        for p in range(2):
            s = jnp.dot(sk_ref[2 * h + p], qT[p * PEER_HALF:(p + 1) * PEER_HALF, :],
                        preferred_element_type=F32)
            for grp in range(n_groups):
                s_scr[2 * h + p, grp] = s[:, grp * LANES:(grp + 1) * LANES]
        return carry

    lax.fori_loop(0, PEER_HEADS, score_body, 0)

    def top_body(hp, carry):
        for grp in range(n_groups):
            work = s_scr[hp, grp]
            rank = jnp.full(work.shape, float(PEER_TOPK), F32)
            for k in range(PEER_TOPK):
                m = jnp.max(work, axis=0, keepdims=True)
                eq = work == m
                rank = jnp.where(eq, float(k), rank)
                work = jnp.where(eq, -jnp.inf, work)
                top_scr[hp, k, grp:grp + 1, :] = m
            rank_scr[hp, grp] = rank
        return carry

    lax.fori_loop(0, 2 * PEER_HEADS, top_body, 0)

    cands = _pair_candidates()

    def pair_body(hh, carry):
        h = 2 * hh

        def both(scr, idx, k):
            return jnp.concatenate([scr[idx(h), k], scr[idx(h + 1), k]], axis=0)

        v1 = [both(top_scr, lambda x: 2 * x, a) for a in range(PEER_TOPK)]
        v2 = [both(top_scr, lambda x: 2 * x + 1, b) for b in range(PEER_TOPK)]
        sums = [v1[a] + v2[b] for (a, b) in cands]
        work = list(sums)
        tau = None
        for k in range(PEER_TOPK):
            tau = functools.reduce(jnp.maximum, work)
            if k + 1 < PEER_TOPK:
                work = [jnp.where(w == tau, -jnp.inf, w) for w in work]
        cmax = v1[0] + v2[0]
        z = jnp.zeros_like(tau)
        r1 = [jnp.zeros_like(tau) for _ in range(PEER_TOPK)]
        for (a, b), c in zip(cands, sums):
            sel = c >= tau
            z = z + jnp.where(sel, jnp.exp(c - cmax), 0.0)
            r1[a] = r1[a] + jnp.where(sel, 1.0, 0.0)
        inv_z = 1.0 / z
        for d in range(2):
            part = slice(d * n_groups, (d + 1) * n_groups)
            for a in range(PEER_TOPK):
                r1top_scr[h + d, a] = r1[a][part]
            stat_scr[h + d, 0] = v1[0][part]
            stat_scr[h + d, 1] = v2[0][part]
            stat_scr[h + d, 2] = inv_z[part]
        return carry

    lax.fori_loop(0, PEER_HEADS // 2, pair_body, 0)

    even_rows = pl.ds(0, N_KEYS // 2, stride=2)
    odd_rows = pl.ds(1, N_KEYS // 2, stride=2)
    def table_body(h, carry):
        for grp in range(n_groups):
            lanes = slice(grp * LANES, (grp + 1) * LANES)
            max1 = stat_scr[h, 0, grp:grp + 1, :]
            max2 = stat_scr[h, 1, grp:grp + 1, :]
            inv_z = stat_scr[h, 2, grp:grp + 1, :]
            rank1 = rank_scr[2 * h, grp]
            r1_key = jnp.zeros(rank1.shape, F32)
            for a in range(PEER_TOPK):
                r1_key = jnp.where(rank1 == float(a), r1top_scr[h, a, grp:grp + 1, :], r1_key)
            r1_ref[h, :, lanes] = _dup_bf16_bits(r1_key)
            e1z = jnp.exp(s_scr[2 * h, grp] - max1) * inv_z
            e1z_ref[h, :, lanes] = _dup_bf16_bits(e1z)
            rk_e = rank_scr[2 * h + 1, grp, even_rows, :]
            rk_o = rank_scr[2 * h + 1, grp, odd_rows, :]
            rank2_ref[h, :, lanes] = _pack_bf16_pairs(rk_e, rk_o)
            e2_e = jnp.exp(jnp.minimum(s_scr[2 * h + 1, grp, even_rows, :] - max2, 0.0))
            e2_o = jnp.exp(jnp.minimum(s_scr[2 * h + 1, grp, odd_rows, :] - max2, 0.0))
            e2_ref[h, :, lanes] = _pack_bf16_pairs(e2_e, e2_o)
        return carry

    lax.fori_loop(0, PEER_HEADS, table_body, 0)


def peer_route(x, g, wqT, sk, *, tm=512):
    T, D = x.shape
    H = PEER_HEADS
    n_groups = tm // LANES
    tab = lambda rows: jax.ShapeDtypeStruct((H, rows, T), jnp.uint32)
    tab_spec = lambda rows: pl.BlockSpec((H, rows, tm), lambda i: (0, 0, i))
    return pl.pallas_call(
        _route_kernel,
        out_shape=(tab(N_KEYS // 2), tab(N_KEYS // 2), tab(N_KEYS), tab(N_KEYS)),
        grid=(T // tm,),
        in_specs=[
            pl.BlockSpec((tm, D), lambda i: (i, 0)),
            pl.BlockSpec((1, D), lambda i: (0, 0)),
            pl.BlockSpec((H * 2 * PEER_HALF, D), lambda i: (0, 0)),
            pl.BlockSpec((2 * H, N_KEYS, PEER_HALF), lambda i: (0, 0, 0)),
        ],
        out_specs=(tab_spec(N_KEYS // 2), tab_spec(N_KEYS // 2), tab_spec(N_KEYS), tab_spec(N_KEYS)),
        scratch_shapes=[
            pltpu.VMEM((tm, D), BF16),
            pltpu.VMEM((2 * H, n_groups, N_KEYS, LANES), F32),
            pltpu.VMEM((2 * H, n_groups, N_KEYS, LANES), F32),
            pltpu.VMEM((2 * H, PEER_TOPK, n_groups, LANES), F32),
            pltpu.VMEM((H, PEER_TOPK, n_groups, LANES), F32),
            pltpu.VMEM((H, 3, n_groups, LANES), F32),
        ],
        compiler_params=_cparams(("parallel",)),
        name="peer_route",
    )(x, g.reshape(1, D).astype(F32), wqT, sk)


GELU_C0 = math.sqrt(2.0 / math.pi)
GELU_C1 = 0.044715


def _gelu_tanh(a):
    inner = GELU_C0 * (a + GELU_C1 * (a * a * a))
    return 0.5 * a * (1.0 + jnp.tanh(inner))


PEER_SUB = 512


def _peer_main_kernel(x_ref, g_ref, rank2_ref, e2_ref, r1_ref, e1z_ref, u_ref, vt_ref, gf_ref, o_ref,
                      xn_scr, acc_scr, act_scr, coef_scr, *, final_norm):
    j = pl.program_id(1)
    tm = x_ref.shape[0]
    n_i1 = r1_ref.shape[1]
    rows = 16
    words = rows * jnp.dtype(BF16).itemsize // 4

    @pl.when(j == 0)
    def _():
        xn_scr[...] = _rms_normed(x_ref[...], g_ref[...]).astype(BF16)
        acc_scr[...] = jnp.zeros_like(acc_scr)

    eb = 2 * u_ref.shape[0]
    d_model = 2 * vt_ref.shape[0]
    tn = 2 * LANES

    def stage_a(n, q):
        r = slice(q * PEER_SUB, (q + 1) * PEER_SUB)
        rw = slice(q * PEER_SUB // 2, (q + 1) * PEER_SUB // 2)
        c = slice(n * tn, (n + 1) * tn)
        act_scr[n, r, :] = lax.dot_general(pltpu.bitcast(u_ref[rw, :], BF16), xn_scr[c, :],
                                           (((1,), (1,)), ((), ())), preferred_element_type=F32)

    def stage_b(n, q):
        for sub in range(tn // LANES):
            lanes = slice(n * tn + sub * LANES, n * tn + (sub + 1) * LANES)
            sl = slice(sub * LANES, (sub + 1) * LANES)
            n_blk = N_KEYS // rows
            for i1 in range(q * PEER_SUB // N_KEYS, (q + 1) * PEER_SUB // N_KEYS, 2):
                gate = [[None] * n_blk, [None] * n_blk]
                for h in range(PEER_HEADS):
                    r1b, e1b = [], []
                    for d in range(2):
                        r1row = jnp.broadcast_to(r1_ref[h, i1 + d:i1 + d + 1, lanes], (words, LANES))
                        e1row = jnp.broadcast_to(e1z_ref[h, i1 + d:i1 + d + 1, lanes], (words, LANES))
                        r1b.append(pltpu.bitcast(r1row, BF16))
                        e1b.append(pltpu.bitcast(e1row, BF16))
                    for blk in range(n_blk):
                        i2w = slice(blk * words, (blk + 1) * words)
                        rk = pltpu.bitcast(rank2_ref[h, i2w, lanes], BF16)
                        ev = pltpu.bitcast(e2_ref[h, i2w, lanes], BF16)
                        for d in range(2):
                            term = jnp.where(rk < r1b[d], ev, jnp.zeros((), BF16)) * e1b[d]
                            gate[d][blk] = term if gate[d][blk] is None else gate[d][blk] + term
                for d in range(2):
                    for blk in range(n_blk):
                        r0 = (i1 + d) * N_KEYS + blk * rows
                        a = act_scr[n, r0:r0 + rows, sl].astype(BF16)
                        coef_scr[n, r0:r0 + rows, sl] = _gelu_tanh(a) * gate[d][blk]

    def stage_c(n):
        for half in range(2):
            r = slice(half * (d_model // 2), (half + 1) * (d_model // 2))
            rw = slice(half * (d_model // 4), (half + 1) * (d_model // 4))
            acc_scr[n, r, :] += jnp.dot(pltpu.bitcast(vt_ref[rw, :], BF16), coef_scr[n],
                                        preferred_element_type=F32)

    for n in range(tm // tn):
        for q in range(eb // PEER_SUB):
            stage_a(n, q)
            stage_b(n, q)
        stage_c(n)

    @pl.when(j == pl.num_programs(1) - 1)
    def _():
        acc_t = jnp.concatenate([acc_scr[n] for n in range(tm // tn)], axis=1)
        y = x_ref[...] + acc_t.T
        if final_norm:
            y = _rms_normed(y, gf_ref[...])
        o_ref[...] = y


def _pack_row_pairs(w):
    r, c = w.shape
    return lax.bitcast_convert_type(w.reshape(r // 2, 2, c).transpose(0, 2, 1), jnp.uint32)


def peer_main(x, g, tables, u, vt, g_final, *, final_norm, tm=512, eb=1024):
    T, D = x.shape
    E = 2 * u.shape[0]
    H = PEER_HEADS
    rank2, e2, r1, e1z = tables
    n_i1 = eb // N_KEYS
    full_tab = pl.BlockSpec((H, N_KEYS // 2, tm), lambda i, j: (0, 0, i))
    i1_tab = pl.BlockSpec((H, n_i1, tm), lambda i, j: (0, j, i))
    return pl.pallas_call(
        functools.partial(_peer_main_kernel, final_norm=final_norm),
        out_shape=jax.ShapeDtypeStruct((T, D), F32),
        grid=(T // tm, E // eb),
        in_specs=[
            pl.BlockSpec((tm, D), lambda i, j: (i, 0)),
            pl.BlockSpec((1, D), lambda i, j: (0, 0)),
            full_tab, full_tab, i1_tab, i1_tab,
            pl.BlockSpec((eb // 2, D), lambda i, j: (j, 0)),
            pl.BlockSpec((D // 2, eb), lambda i, j: (0, j)),
            pl.BlockSpec((1, D), lambda i, j: (0, 0)),
        ],
        out_specs=pl.BlockSpec((tm, D), lambda i, j: (i, 0)),
        scratch_shapes=[
            pltpu.VMEM((tm, D), BF16),
            pltpu.VMEM((tm // (2 * LANES), D, 2 * LANES), F32),
            pltpu.VMEM((tm // (2 * LANES), eb, 2 * LANES), F32),
            pltpu.VMEM((tm // (2 * LANES), eb, 2 * LANES), BF16),
        ],
        compiler_params=_cparams(("parallel", "arbitrary")),
        name="peer_main",
    )(x, g.reshape(1, D).astype(F32), rank2, e2, r1, e1z, u, vt, g_final.reshape(1, D).astype(F32))


def peer_layer(x, g, w_query, sub_keys, expert_u, expert_v, g_final, *, final_norm):
    wqT = w_query.T.astype(BF16)
    sk = sub_keys.reshape(2 * PEER_HEADS, N_KEYS, PEER_HALF).astype(BF16)
    tables = peer_route(x, g, wqT, sk)
    return peer_main(x, g, tables, _pack_row_pairs(expert_u.astype(BF16)),
                     _pack_row_pairs(expert_v.T.astype(BF16)), g_final, final_norm=final_norm)


ATTN_TILE = 256


def _attn_kernel(lam_ref, q_ref, k_ref, v_ref, bias_ref, gsub_ref, o_ref, vt_scr, m_scr, l_scr, acc_scr,
                 s0_scr, s1_scr, p0_scr, p1_scr, *, out_scale):
    i = pl.program_id(2)
    tq = q_ref.shape[1]
    tk = tq

    @pl.when(i == 0)
    def _():
        vt_scr[...] = v_ref[0].T

    q = q_ref[0]
    lane = lax.broadcasted_iota(jnp.int32, q.shape, 1)
    scale = jnp.asarray(A_QK_DIM ** -0.5, BF16)
    zero = jnp.zeros((), BF16)
    qs = jnp.concatenate([jnp.where(lane < A_QK_DIM, q, zero) * scale,
                          jnp.where(lane >= A_QK_DIM, q, zero) * scale], axis=0)

    m_scr[...] = jnp.full(m_scr.shape, NEG_BIG, F32)
    l_scr[...] = jnp.zeros(l_scr.shape, F32)
    acc_scr[...] = jnp.zeros(acc_scr.shape, F32)

    def scores(j):
        k = k_ref[0, pl.ds(pl.multiple_of(j * tk, tk), tk), :]
        return lax.dot_general(k, qs, (((1,), (1,)), ((), ())), preferred_element_type=F32)

    def pv(j, p_ref):
        vt = vt_scr[:, pl.ds(pl.multiple_of(j * tk, tk), tk)]
        return jnp.dot(vt, p_ref[...], preferred_element_type=F32)

    def consume(j, bias, cur, nxt, prefetch):
        s_cur, p_cur = cur
        s_nxt, p_prev = nxt
        if prefetch:
            s_nxt[...] = scores(j + 1)
        pv_prev = pv(jnp.maximum(j - 1, 0), p_prev)
        s = s_cur[...]
        if bias is not None:
            s = s + jnp.concatenate([bias, bias], axis=1)
        m_prev = m_scr[...]
        m_new = jnp.maximum(m_prev, jnp.max(s, axis=0, keepdims=True))
        alpha = jnp.exp(m_prev - m_new)
        p = jnp.exp(s - m_new)
        p_cur[...] = p.astype(BF16)
        l_scr[...] = alpha * l_scr[...] + jnp.sum(p, axis=0, keepdims=True)
        acc_scr[...] = alpha * (acc_scr[...] + pv_prev)
        m_scr[...] = m_new

    def by_parity(j, fn):
        @pl.when(j % 2 == 0)
        def _():
            fn((s0_scr, p0_scr), (s1_scr, p1_scr))

        @pl.when(j % 2 == 1)
        def _():
            fn((s1_scr, p1_scr), (s0_scr, p0_scr))

    s0_scr[...] = scores(0)
    p1_scr[...] = jnp.zeros(p1_scr.shape, BF16)

    def far_body(j, carry):
        by_parity(j, lambda cur, nxt: consume(j, None, cur, nxt, True))
        return carry

    lax.fori_loop(0, i - 1, far_body, 0)

    @pl.when(i >= 1)
    def _():
        by_parity(i - 1, lambda cur, nxt: consume(i - 1, bias_ref[0, 0], cur, nxt, True))

    def last(cur, nxt):
        consume(i, bias_ref[0, 1], cur, nxt, False)
        acc_scr[...] += pv(i, cur[1])

    by_parity(i, last)

    ot = acc_scr[...] / l_scr[...]
    ot = ot[:, :tq] - lam_ref[0] * ot[:, tq:]
    ot = ot * lax.rsqrt(jnp.mean(ot * ot, axis=0, keepdims=True) + EPS)
    o_ref[0] = (ot.T * (gsub_ref[...] * out_scale)).astype(o_ref.dtype)


def _t5_bucket(rel):
    nb = N_BUCKETS // 2
    max_exact = nb // 2
    ret = jnp.where(rel > 0, nb, 0)
    n = jnp.abs(rel)
    nf = jnp.maximum(n, 1).astype(F32)
    large = max_exact + (jnp.log(nf / max_exact) / math.log(MAX_DISTANCE / max_exact)
                         * (nb - max_exact)).astype(jnp.int32)
    large = jnp.minimum(large, nb - 1)
    return ret + jnp.where(n < max_exact, n, large)


def _attn_bias_tiles(rel_bias, tile):
    r = jnp.arange(tile)[None, :]
    c = jnp.arange(tile)[:, None]
    rb = rel_bias.astype(F32)

    def lookup(bucket):
        return jnp.einsum("...b,bh->...h", jax.nn.one_hot(bucket, N_BUCKETS, dtype=F32), rb,
                          precision=lax.Precision.HIGHEST)

    far = lookup(_t5_bucket(jnp.asarray(-(2 * tile))))
    prev = lookup(_t5_bucket(c - r - tile)) - far
    diag = lookup(_t5_bucket(c - r)) - far
    diag = jnp.where(((c // CHUNK) <= (r // CHUNK))[..., None], diag, NEG_BIG)
    return jnp.transpose(jnp.stack([prev, diag], axis=0), (3, 0, 1, 2))


def diff_attention(proj, lam, bias_tiles, subln_g, *, out_scale):
    B, S, _ = proj.shape
    H = A_HEADS
    tq = ATTN_TILE
    return pl.pallas_call(
        functools.partial(_attn_kernel, out_scale=out_scale),
        out_shape=jax.ShapeDtypeStruct((B, S, H * A_V_DIM), BF16),
        grid=(B, H, S // tq),
        in_specs=[
            pl.BlockSpec(memory_space=pltpu.SMEM),
            pl.BlockSpec((1, tq, A_V_DIM), lambda b, h, i: (b, i, h)),
            pl.BlockSpec((1, S, A_V_DIM), lambda b, h, i: (b, 0, H + h)),
            pl.BlockSpec((1, S, A_V_DIM), lambda b, h, i: (b, 0, 2 * H + h)),
            pl.BlockSpec((1, 2, tq, tq), lambda b, h, i: (h, 0, 0, 0)),
            pl.BlockSpec((1, A_V_DIM), lambda b, h, i: (0, 0)),
        ],
        out_specs=pl.BlockSpec((1, tq, A_V_DIM), lambda b, h, i: (b, i, h)),
        scratch_shapes=[
            pltpu.VMEM((A_V_DIM, S), BF16),
            pltpu.VMEM((1, 2 * tq), F32),
            pltpu.VMEM((1, 2 * tq), F32),
            pltpu.VMEM((A_V_DIM, 2 * tq), F32),
            pltpu.VMEM((tq, 2 * tq), F32),
            pltpu.VMEM((tq, 2 * tq), F32),
            pltpu.VMEM((tq, 2 * tq), BF16),
            pltpu.VMEM((tq, 2 * tq), BF16),
        ],
        compiler_params=_cparams(("parallel", "parallel", "arbitrary")),
        name="diff_attention",
    )(lam.reshape(1).astype(F32), proj, proj, proj, bias_tiles, subln_g.reshape(1, A_V_DIM).astype(F32))


CONV_HALO = 32
CONV_ROWS = 64


def _conv_kernel(val_ref, gate_ref, hval_ref, hgate_ref, w_ref, b_ref, g_ref, beta_ref, o_ref, buf_scr):
    i = pl.program_id(1)
    ts = val_ref.shape[1]

    def glu(v, gt):
        return v.astype(F32) * jax.nn.sigmoid(gt.astype(F32))

    halo = glu(hval_ref[0], hgate_ref[0])
    buf_scr[0:CONV_HALO, :] = jnp.where(i > 0, halo, 0.0)
    buf_scr[CONV_HALO:, :] = glu(val_ref[0], gate_ref[0])

    first = CONV_HALO - (CONV_WIDTH - 1)
    for r in range(ts // CONV_ROWS):
        acc = None
        for w in range(CONV_WIDTH):
            lo = r * CONV_ROWS + first + w
            term = buf_scr[lo:lo + CONV_ROWS, :] * w_ref[w:w + 1, :]
            acc = term if acc is None else acc + term
        y = acc + b_ref[...]
        mu = jnp.mean(y, axis=-1, keepdims=True)
        d = y - mu
        var = jnp.mean(d * d, axis=-1, keepdims=True)
        y = d * lax.rsqrt(var + EPS) * g_ref[...] + beta_ref[...]
        y = y * jax.nn.sigmoid(y)
        o_ref[0, r * CONV_ROWS:(r + 1) * CONV_ROWS, :] = y.astype(o_ref.dtype)


def conv_module(proj, conv_w, conv_b, ln_g, ln_b, *, ts=512):
    B, S, _ = proj.shape
    C = conv_w.shape[1]
    hb = ts // CONV_HALO
    row = lambda a: a.reshape(1, C).astype(F32)
    cur = lambda col: pl.BlockSpec((1, ts, C), lambda b, i: (b, i, col))
    halo = lambda col: pl.BlockSpec((1, CONV_HALO, C), lambda b, i: (b, jnp.maximum(i * hb - 1, 0), col))
    vec = pl.BlockSpec((1, C), lambda b, i: (0, 0))
    return pl.pallas_call(
        _conv_kernel,
        out_shape=jax.ShapeDtypeStruct((B, S, C), BF16),
        grid=(B, S // ts),
        in_specs=[cur(3), cur(4), halo(3), halo(4),
                  pl.BlockSpec((CONV_WIDTH + 1, C), lambda b, i: (0, 0)), vec, vec, vec],
        out_specs=pl.BlockSpec((1, ts, C), lambda b, i: (b, i, 0)),
        scratch_shapes=[pltpu.VMEM((ts + CONV_HALO, C), F32)],
        compiler_params=_cparams(("parallel", "parallel")),
        name="conv_module",
    )(proj, proj, proj, proj, jnp.pad(conv_w.astype(F32), ((0, 1), (0, 0))), row(conv_b), row(ln_g), row(ln_b))


GLA_DK = 128
GLA_DV = 256
GLA_Z_COLS = 128


def _split_bf16(x):
    hi = x.astype(BF16)
    lo = (x - hi.astype(F32)).astype(BF16)
    return hi, lo


def _gla_kernel(q_ref, k_ref, v_ref, g_ref, z_ref, wg_ref, bg_ref, gn_ref, o_ref, state_scr):
    ts = q_ref.shape[0]
    n_chunks = ts // CHUNK

    @pl.when(pl.program_id(1) == 0)
    def _():
        state_scr[...] = jnp.zeros(state_scr.shape, F32)

    zz = jnp.dot(z_ref[...], wg_ref[...], preferred_element_type=F32) + bg_ref[...]
    log_a = (jnp.minimum(zz, 0.0) - jnp.log1p(jnp.exp(-jnp.abs(zz)))) / GATE_TAU

    row = lax.broadcasted_iota(jnp.int32, (CHUNK, CHUNK), 0)
    col = lax.broadcasted_iota(jnp.int32, (CHUNK, CHUNK), 1)
    tri = jnp.where(col <= row, 1.0, 0.0).astype(BF16)
    ones = jnp.ones((CHUNK, LANES), BF16)
    tn = (((0,), (0,)), ((), ()))
    q_scale = GLA_DK ** -0.5

    for c in range(n_chunks):
        rows = slice(c * CHUNK, (c + 1) * CHUNK)
        la_hi, la_lo = _split_bf16(log_a[rows, :])
        cum = (jnp.dot(tri, la_hi, preferred_element_type=F32)
               + jnp.dot(tri, la_lo, preferred_element_type=F32))
        k_dec = k_ref[rows, :].astype(F32) * jnp.exp(cum[CHUNK - 1:CHUNK, :] - cum)
        for h in range(GLA_HEADS):
            kc = slice(h * GLA_DK, (h + 1) * GLA_DK)
            vc = slice(h * GLA_DV, (h + 1) * GLA_DV)
            tot = (lax.dot_general(la_hi[:, kc], ones, tn, preferred_element_type=F32)
                   + lax.dot_general(la_lo[:, kc], ones, tn, preferred_element_type=F32))
            decay = jnp.exp(tot)
            decay = jnp.concatenate([decay] * (GLA_DV // LANES), axis=1)
            kv = lax.dot_general(k_dec[:, kc].astype(BF16), v_ref[rows, vc], tn,
                                 preferred_element_type=F32)
            state = state_scr[h] * decay + kv
            state_scr[h] = state
            o = jnp.dot(q_ref[rows, kc], state.astype(BF16), preferred_element_type=F32) * q_scale
            o = o * lax.rsqrt(jnp.mean(o * o, axis=-1, keepdims=True) + EPS) * gn_ref[...]
            gt = g_ref[rows, vc].astype(F32)
            o_ref[rows, vc] = (o * (gt * jax.nn.sigmoid(gt))).astype(o_ref.dtype)


def gla_scan(proj, w_gate, b_gate, norm_g, *, batch, ts=512):
    T = proj.shape[0]
    S = T // batch
    nb = S // ts
    KD = GLA_HEADS * GLA_DK
    VD = GLA_HEADS * GLA_DV
    rows = lambda width, col: pl.BlockSpec((ts, width), lambda b, i: (b * nb + i, col))
    return pl.pallas_call(
        _gla_kernel,
        out_shape=jax.ShapeDtypeStruct((T, VD), BF16),
        grid=(batch, nb),
        in_specs=[
            rows(KD, 0), rows(KD, 1), rows(VD, 1), rows(VD, 2), rows(GLA_Z_COLS, (2 * KD + 2 * VD) // GLA_Z_COLS),
            pl.BlockSpec((GLA_Z_COLS, KD), lambda b, i: (0, 0)),
            pl.BlockSpec((1, KD), lambda b, i: (0, 0)),
            pl.BlockSpec((1, GLA_DV), lambda b, i: (0, 0)),
        ],
        out_specs=pl.BlockSpec((ts, VD), lambda b, i: (b * nb + i, 0)),
        scratch_shapes=[pltpu.VMEM((GLA_HEADS, GLA_DK, GLA_DV), F32)],
        compiler_params=_cparams(("parallel", "arbitrary")),
        name="gla_scan",
    )(proj, proj, proj, proj, proj, w_gate, b_gate.reshape(1, KD).astype(F32),
      norm_g.reshape(1, GLA_DV).astype(F32))


def kernel(x, rel_bias, ln_mix, ln_ffn, even_w_in, lam_q1, lam_k1, lam_q2, lam_k2, subln_g, conv_w, conv_b, conv_ln_g, conv_ln_b, even_w_out, odd_w_in, w_gate2, b_gate2, gla_norm_g, odd_w_out, peer_w_query, peer_sub_keys, peer_u, peer_v, ln_final):
    B, S, D = x.shape
    T = B * S
    xf = x.reshape(T, D)

    lam_init = 0.8 - 0.6 * math.exp(-0.3 * 0)
    lam = (jnp.exp(jnp.sum(lam_q1[0].astype(F32) * lam_k1[0].astype(F32)))
           - jnp.exp(jnp.sum(lam_q2[0].astype(F32) * lam_k2[0].astype(F32))) + lam_init)
    proj = norm_matmul(xf, ln_mix[0], even_w_in[0].astype(BF16)).reshape(B, S, -1)
    a_out = diff_attention(proj, lam, _attn_bias_tiles(rel_bias, ATTN_TILE), subln_g[0],
                           out_scale=1.0 - lam_init)
    c_out = conv_module(proj, conv_w[0], conv_b[0], conv_ln_g[0], conv_ln_b[0])
    a_width = A_HEADS * A_V_DIM
    w_out = even_w_out[0].astype(BF16)
    xf = proj_residual(xf, a_out.reshape(T, -1), c_out.reshape(T, -1), w_out[:a_width], w_out[a_width:])
    xf = peer_layer(xf, ln_ffn[0], peer_w_query[0], peer_sub_keys[0], peer_u[0], peer_v[0], ln_final,
                    final_norm=False)

    kd = GLA_HEADS * GLA_DK
    vd = GLA_HEADS * GLA_DV
    w_in = jnp.pad(odd_w_in[0], ((0, 0), (0, GLA_Z_COLS - GATE_RANK))).astype(BF16)
    w_gate = jnp.pad(w_gate2[0], ((0, GLA_Z_COLS - GATE_RANK), (0, 0))).astype(BF16)
    proj = norm_matmul(xf, ln_mix[1], w_in)
    og = gla_scan(proj, w_gate, b_gate2[0], gla_norm_g[0], batch=B)
    w_out = odd_w_out[0].astype(BF16)
    xf = proj_residual(xf, og, og, w_out[:vd // 2], w_out[vd // 2:], a_col=0, c_col=1)
    xf = peer_layer(xf, ln_ffn[1], peer_w_query[1], peer_sub_keys[1], peer_u[1], peer_v[1], ln_final,
                    final_norm=True)
    return xf.reshape(B, S, D)
```

```python
import functools
import math

import jax
import jax.numpy as jnp
import numpy as np
from jax import lax
from jax.experimental import pallas as pl
from jax.experimental.pallas import tpu as pltpu

F32 = jnp.float32
BF16 = jnp.bfloat16

EPS = 1e-6
LANES = 128
VMEM_LIMIT = 56 * 1024 * 1024

CHUNK = 64
N_BUCKETS = 32
MAX_DISTANCE = 128
A_HEADS = 4
A_QK_DIM = 64
A_V_DIM = 128
CONV_WIDTH = 31
GLA_HEADS = 4
GATE_RANK = 16
GATE_TAU = 16.0
PEER_HEADS = 8
N_KEYS = 128
PEER_TOPK = 16
PEER_HALF = 128
NEG_BIG = -1e30


def _cparams(sem):
    return pltpu.CompilerParams(dimension_semantics=sem, vmem_limit_bytes=VMEM_LIMIT)


def _rms_normed(x, g):
    ms = jnp.mean(x * x, axis=-1, keepdims=True)
    return x * lax.rsqrt(ms + EPS) * g


def _norm_matmul_kernel(x_ref, g_ref, w_ref, o_ref):
    xn = _rms_normed(x_ref[...], g_ref[...]).astype(BF16)
    o_ref[...] = jnp.dot(xn, w_ref[...], preferred_element_type=F32).astype(o_ref.dtype)


def norm_matmul(x, g, w, *, tm=512, out_dtype=BF16):
    T, D = x.shape
    N = w.shape[1]
    return pl.pallas_call(
        _norm_matmul_kernel,
        out_shape=jax.ShapeDtypeStruct((T, N), out_dtype),
        grid=(T // tm,),
        in_specs=[
            pl.BlockSpec((tm, D), lambda i: (i, 0)),
            pl.BlockSpec((1, D), lambda i: (0, 0)),
            pl.BlockSpec((D, N), lambda i: (0, 0)),
        ],
        out_specs=pl.BlockSpec((tm, N), lambda i: (i, 0)),
        compiler_params=_cparams(("parallel",)),
        name="norm_matmul",
    )(x, g.reshape(1, D).astype(F32), w)


def _proj_residual_kernel(x_ref, a_ref, c_ref, wa_ref, wc_ref, o_ref):
    acc = jnp.dot(a_ref[...], wa_ref[...], preferred_element_type=F32)
    acc = acc + jnp.dot(c_ref[...], wc_ref[...], preferred_element_type=F32)
    o_ref[...] = x_ref[...] + acc


def proj_residual(x, a, c, wa, wc, *, a_col=0, c_col=0, tm=512):
    T, D = x.shape
    Ka, Kc = wa.shape[0], wc.shape[0]
    return pl.pallas_call(
        _proj_residual_kernel,
        out_shape=jax.ShapeDtypeStruct((T, D), F32),
        grid=(T // tm,),
        in_specs=[
            pl.BlockSpec((tm, D), lambda i: (i, 0)),
            pl.BlockSpec((tm, Ka), lambda i: (i, a_col)),
            pl.BlockSpec((tm, Kc), lambda i: (i, c_col)),
            pl.BlockSpec((Ka, D), lambda i: (0, 0)),
            pl.BlockSpec((Kc, D), lambda i: (0, 0)),
        ],
        out_specs=pl.BlockSpec((tm, D), lambda i: (i, 0)),
        compiler_params=_cparams(("parallel",)),
        name="proj_residual",
    )(x, a, c, wa, wc)


def _pair_candidates():
    return [(a, b) for a in range(PEER_TOPK) for b in range(PEER_TOPK) if (a + 1) * (b + 1) <= PEER_TOPK]


def _dup_bf16_bits(v):
    hi = pltpu.bitcast(v.astype(BF16).astype(F32), jnp.uint32)
    return hi | (hi >> 16)


PAIR = 8


def _pack_bf16_pairs(lo, hi):
    bl = pltpu.bitcast(lo.astype(BF16).astype(F32), jnp.uint32)
    bh = pltpu.bitcast(hi.astype(BF16).astype(F32), jnp.uint32)
    return (bl >> 16) | bh


def _pair_rows(scr, idx, grp, half):
    n_rows = scr.shape[2]
    return jnp.concatenate([scr[idx, grp, g + PAIR * half:g + PAIR * (half + 1), :]
                            for g in range(0, n_rows, 2 * PAIR)], axis=0)


def _route_kernel(x_ref, g_ref, wqT_ref, sk_ref, rank2_ref, e2_ref, r1_ref, e1z_ref,
                  xn_scr, s_scr, rank_scr, top_scr, r1top_scr, stat_scr):
    tm = x_ref.shape[0]
    n_groups = tm // LANES
    xn_scr[...] = _rms_normed(x_ref[...], g_ref[...]).astype(BF16)

    def score_body(h, carry):
        w_h = wqT_ref[pl.ds(pl.multiple_of(h * 2 * PEER_HALF, 2 * PEER_HALF), 2 * PEER_HALF), :]
        qT = lax.dot_general(w_h, xn_scr[...], (((1,), (1,)), ((), ())),
                             preferred_element_type=F32).astype(BF16)
        for p in range(2):
            s = jnp.dot(sk_ref[2 * h + p], qT[p * PEER_HALF:(p + 1) * PEER_HALF, :],
                        preferred_element_type=F32)
            for grp in range(n_groups):
                s_scr[2 * h + p, grp] = s[:, grp * LANES:(grp + 1) * LANES]
        return carry

    lax.fori_loop(0, PEER_HEADS, score_body, 0)

    def top_body(hp, carry):
        for grp in range(n_groups):
            work = s_scr[hp, grp]
            rank = jnp.full(work.shape, float(PEER_TOPK), F32)
            for k in range(PEER_TOPK):
                m = jnp.max(work, axis=0, keepdims=True)
                eq = work == m
                rank = jnp.where(eq, float(k), rank)
                work = jnp.where(eq, -jnp.inf, work)
                top_scr[hp, k, grp:grp + 1, :] = m
            rank_scr[hp, grp] = rank
        return carry

    lax.fori_loop(0, 2 * PEER_HEADS, top_body, 0)

    cands = _pair_candidates()

    def pair_body(hh, carry):
        h = 2 * hh

        def both(scr, idx, k):
            return jnp.concatenate([scr[idx(h), k], scr[idx(h + 1), k]], axis=0)

        v1 = [both(top_scr, lambda x: 2 * x, a) for a in range(PEER_TOPK)]
        v2 = [both(top_scr, lambda x: 2 * x + 1, b) for b in range(PEER_TOPK)]
        sums = [v1[a] + v2[b] for (a, b) in cands]
        work = list(sums)
        tau = None
        for k in range(PEER_TOPK):
            tau = functools.reduce(jnp.maximum, work)
            if k + 1 < PEER_TOPK:
                work = [jnp.where(w == tau, -jnp.inf, w) for w in work]
        cmax = v1[0] + v2[0]
        z = jnp.zeros_like(tau)
        r1 = [jnp.zeros_like(tau) for _ in range(PEER_TOPK)]
        for (a, b), c in zip(cands, sums):
            sel = c >= tau
            z = z + jnp.where(sel, jnp.exp(c - cmax), 0.0)
            r1[a] = r1[a] + jnp.where(sel, 1.0, 0.0)
        inv_z = 1.0 / z
        for d in range(2):
            part = slice(d * n_groups, (d + 1) * n_groups)
            for a in range(PEER_TOPK):
                r1top_scr[h + d, a] = r1[a][part]
            stat_scr[h + d, 0] = v1[0][part]
            stat_scr[h + d, 1] = v2[0][part]
            stat_scr[h + d, 2] = inv_z[part]
        return carry

    lax.fori_loop(0, PEER_HEADS // 2, pair_body, 0)

    def table_body(h, carry):
        for grp in range(n_groups):
            lanes = slice(grp * LANES, (grp + 1) * LANES)
            max1 = stat_scr[h, 0, grp:grp + 1, :]
            max2 = stat_scr[h, 1, grp:grp + 1, :]
            inv_z = stat_scr[h, 2, grp:grp + 1, :]
            rank1 = rank_scr[2 * h, grp]
            r1_key = jnp.zeros(rank1.shape, F32)
            for a in range(PEER_TOPK):
                r1_key = jnp.where(rank1 == float(a), r1top_scr[h, a, grp:grp + 1, :], r1_key)
            r1_ref[h, :, lanes] = _dup_bf16_bits(r1_key)
            e1z = jnp.exp(s_scr[2 * h, grp] - max1) * inv_z
            e1z_ref[h, :, lanes] = _dup_bf16_bits(e1z)
            rk_lo = _pair_rows(rank_scr, 2 * h + 1, grp, 0)
            rk_hi = _pair_rows(rank_scr, 2 * h + 1, grp, 1)
            rank2_ref[h, :, lanes] = _pack_bf16_pairs(rk_lo, rk_hi)
            e2_lo = jnp.exp(jnp.minimum(_pair_rows(s_scr, 2 * h + 1, grp, 0) - max2, 0.0))
            e2_hi = jnp.exp(jnp.minimum(_pair_rows(s_scr, 2 * h + 1, grp, 1) - max2, 0.0))
            e2_ref[h, :, lanes] = _pack_bf16_pairs(e2_lo, e2_hi)
        return carry

    lax.fori_loop(0, PEER_HEADS, table_body, 0)


def peer_route(x, g, wqT, sk, *, tm=512):
    T, D = x.shape
    H = PEER_HEADS
    n_groups = tm // LANES
    tab = lambda rows: jax.ShapeDtypeStruct((H, rows, T), jnp.uint32)
    tab_spec = lambda rows: pl.BlockSpec((H, rows, tm), lambda i: (0, 0, i))
    return pl.pallas_call(
        _route_kernel,
        out_shape=(tab(N_KEYS // 2), tab(N_KEYS // 2), tab(N_KEYS), tab(N_KEYS)),
        grid=(T // tm,),
        in_specs=[
            pl.BlockSpec((tm, D), lambda i: (i, 0)),
            pl.BlockSpec((1, D), lambda i: (0, 0)),
            pl.BlockSpec((H * 2 * PEER_HALF, D), lambda i: (0, 0)),
            pl.BlockSpec((2 * H, N_KEYS, PEER_HALF), lambda i: (0, 0, 0)),
        ],
        out_specs=(tab_spec(N_KEYS // 2), tab_spec(N_KEYS // 2), tab_spec(N_KEYS), tab_spec(N_KEYS)),
        scratch_shapes=[
            pltpu.VMEM((tm, D), BF16),
            pltpu.VMEM((2 * H, n_groups, N_KEYS, LANES), F32),
            pltpu.VMEM((2 * H, n_groups, N_KEYS, LANES), F32),
            pltpu.VMEM((2 * H, PEER_TOPK, n_groups, LANES), F32),
            pltpu.VMEM((H, PEER_TOPK, n_groups, LANES), F32),
            pltpu.VMEM((H, 3, n_groups, LANES), F32),
        ],
        compiler_params=_cparams(("parallel",)),
        name="peer_route",
    )(x, g.reshape(1, D).astype(F32), wqT, sk)


GELU_C0 = math.sqrt(2.0 / math.pi)
GELU_C1 = 0.044715


def _gelu_tanh(a):
    inner = GELU_C0 * (a + GELU_C1 * (a * a * a))
    return 0.5 * a * (1.0 + jnp.tanh(inner))


PEER_SUB = 512


def _peer_main_kernel(x_ref, g_ref, rank2_ref, e2_ref, r1_ref, e1z_ref, u_ref, vt_ref, gf_ref, o_ref,
                      xn_scr, acc_scr, act_scr, coef_scr, *, final_norm):
    j = pl.program_id(1)
    tm = x_ref.shape[0]
    n_i1 = r1_ref.shape[1]
    rows = 16
    words = rows * jnp.dtype(BF16).itemsize // 4

    @pl.when(j == 0)
    def _():
        xn_scr[...] = _rms_normed(x_ref[...], g_ref[...]).astype(BF16)
        acc_scr[...] = jnp.zeros_like(acc_scr)

    eb = 2 * u_ref.shape[0]
    d_model = 2 * vt_ref.shape[0]
    tn = 2 * LANES

    def stage_a(n, q):
        r = slice(q * PEER_SUB, (q + 1) * PEER_SUB)
        rw = slice(q * PEER_SUB // 2, (q + 1) * PEER_SUB // 2)
        c = slice(n * tn, (n + 1) * tn)
        act_scr[n, r, :] = lax.dot_general(pltpu.bitcast(u_ref[rw, :], BF16), xn_scr[c, :],
                                           (((1,), (1,)), ((), ())), preferred_element_type=F32)

    def stage_b(n, q):
        for sub in range(tn // LANES):
            lanes = slice(n * tn + sub * LANES, n * tn + (sub + 1) * LANES)
            sl = slice(sub * LANES, (sub + 1) * LANES)
            n_blk = N_KEYS // rows
            for i1 in range(q * PEER_SUB // N_KEYS, (q + 1) * PEER_SUB // N_KEYS, 2):
                gate = [[None] * n_blk, [None] * n_blk]
                for h in range(PEER_HEADS):
                    r1b, e1b = [], []
                    for d in range(2):
                        r1row = jnp.broadcast_to(r1_ref[h, i1 + d:i1 + d + 1, lanes], (words, LANES))
                        e1row = jnp.broadcast_to(e1z_ref[h, i1 + d:i1 + d + 1, lanes], (words, LANES))
                        r1b.append(pltpu.bitcast(r1row, BF16))
                        e1b.append(pltpu.bitcast(e1row, BF16))
                    for blk in range(n_blk):
                        i2w = slice(blk * words, (blk + 1) * words)
                        rk = pltpu.bitcast(rank2_ref[h, i2w, lanes], BF16)
                        ev = pltpu.bitcast(e2_ref[h, i2w, lanes], BF16)
                        for d in range(2):
                            term = jnp.where(rk < r1b[d], ev, jnp.zeros((), BF16)) * e1b[d]
                            gate[d][blk] = term if gate[d][blk] is None else gate[d][blk] + term
                for d in range(2):
                    for blk in range(n_blk):
                        r0 = (i1 + d) * N_KEYS + blk * rows
                        a = act_scr[n, r0:r0 + rows, sl].astype(BF16)
                        coef_scr[n, r0:r0 + rows, sl] = _gelu_tanh(a) * gate[d][blk]

    def stage_c(n):
        for half in range(2):
            r = slice(half * (d_model // 2), (half + 1) * (d_model // 2))
            rw = slice(half * (d_model // 4), (half + 1) * (d_model // 4))
            res = jnp.dot(pltpu.bitcast(vt_ref[rw, :], BF16), coef_scr[n], preferred_element_type=F32)
            for sub in range(tn // LANES):
                acc_scr[n * (tn // LANES) + sub, r, :] += res[:, sub * LANES:(sub + 1) * LANES]

    for n in range(tm // tn):
        for q in range(eb // PEER_SUB):
            stage_a(n, q)
            stage_b(n, q)
        stage_c(n)

    @pl.when(j == pl.num_programs(1) - 1)
    def _():
        lo_rows = pl.ds(0, d_model // 2, stride=2)
        hi_rows = pl.ds(1, d_model // 2, stride=2)
        peer_out = jnp.concatenate(
            [jnp.concatenate([acc_scr[grp, lo_rows, :], acc_scr[grp, hi_rows, :]], axis=0).T
             for grp in range(tm // LANES)], axis=0)
        y = x_ref[...] + peer_out
        if final_norm:
            y = _rms_normed(y, gf_ref[...])
        o_ref[...] = y


def _bf16_words(lo, hi):
    bits = lambda a: lax.bitcast_convert_type(a.astype(BF16), jnp.uint16).astype(jnp.uint32)
    return bits(lo) | (bits(hi) << 16)


def _pack_expert_rows(u):
    e, d = u.shape
    u4 = u.reshape(e // (2 * PAIR), 2, PAIR, d)
    return _bf16_words(u4[:, 0], u4[:, 1]).reshape(e // 2, d)


def _pack_vt(v):
    e, d = v.shape
    v_perm = v.reshape(e // (2 * PAIR), 2, PAIR, d).transpose(0, 2, 1, 3).reshape(e, d)
    vt = v_perm.T
    return _bf16_words(vt[:d // 2], vt[d // 2:])


def peer_main(x, g, tables, u, vt, g_final, *, final_norm, tm=512, eb=2048):
    T, D = x.shape
    E = 2 * u.shape[0]
    H = PEER_HEADS
    rank2, e2, r1, e1z = tables
    n_i1 = eb // N_KEYS
    full_tab = pl.BlockSpec((H, N_KEYS // 2, tm), lambda i, j: (0, 0, i))
    i1_tab = pl.BlockSpec((H, n_i1, tm), lambda i, j: (0, j, i))
    return pl.pallas_call(
        functools.partial(_peer_main_kernel, final_norm=final_norm),
        out_shape=jax.ShapeDtypeStruct((T, D), F32),
        grid=(T // tm, E // eb),
        in_specs=[
            pl.BlockSpec((tm, D), lambda i, j: (i, 0)),
            pl.BlockSpec((1, D), lambda i, j: (0, 0)),
            full_tab, full_tab, i1_tab, i1_tab,
            pl.BlockSpec((eb // 2, D), lambda i, j: (j, 0)),
            pl.BlockSpec((D // 2, eb), lambda i, j: (0, j)),
            pl.BlockSpec((1, D), lambda i, j: (0, 0)),
        ],
        out_specs=pl.BlockSpec((tm, D), lambda i, j: (i, 0)),
        scratch_shapes=[
            pltpu.VMEM((tm, D), BF16),
            pltpu.VMEM((tm // LANES, D, LANES), F32),
            pltpu.VMEM((tm // (2 * LANES), eb, 2 * LANES), F32),
            pltpu.VMEM((tm // (2 * LANES), eb, 2 * LANES), BF16),
        ],
        compiler_params=_cparams(("parallel", "arbitrary")),
        name="peer_main",
    )(x, g.reshape(1, D).astype(F32), rank2, e2, r1, e1z, u, vt, g_final.reshape(1, D).astype(F32))


def peer_layer(x, g, w_query, sub_keys, expert_u, expert_v, g_final, *, final_norm):
    wqT = w_query.T.astype(BF16)
    sk = sub_keys.reshape(2 * PEER_HEADS, N_KEYS, PEER_HALF).astype(BF16)
    tables = peer_route(x, g, wqT, sk)
    return peer_main(x, g, tables, _pack_expert_rows(expert_u), _pack_vt(expert_v), g_final,
                     final_norm=final_norm)


ATTN_TILE = 256


def _attn_kernel(lam_ref, q_ref, k_ref, v_ref, bias_ref, gsub_ref, o_ref, vt_scr, m_scr, l_scr, acc_scr,
                 s0_scr, s1_scr, p0_scr, p1_scr, *, out_scale):
    i = pl.program_id(2)
    tq = q_ref.shape[1]
    tk = tq

    @pl.when(i == 0)
    def _():
        vt_scr[...] = v_ref[0].T

    q = q_ref[0]
    lane = lax.broadcasted_iota(jnp.int32, q.shape, 1)
    scale = jnp.asarray(A_QK_DIM ** -0.5, BF16)
    zero = jnp.zeros((), BF16)
    qs = jnp.concatenate([jnp.where(lane < A_QK_DIM, q, zero) * scale,
                          jnp.where(lane >= A_QK_DIM, q, zero) * scale], axis=0)

    m_scr[...] = jnp.full(m_scr.shape, NEG_BIG, F32)
    l_scr[...] = jnp.zeros(l_scr.shape, F32)
    acc_scr[...] = jnp.zeros(acc_scr.shape, F32)

    def scores(j):
        k = k_ref[0, pl.ds(pl.multiple_of(j * tk, tk), tk), :]
        return lax.dot_general(k, qs, (((1,), (1,)), ((), ())), preferred_element_type=F32)

    def pv(j, p_ref):
        vt = vt_scr[:, pl.ds(pl.multiple_of(j * tk, tk), tk)]
        return jnp.dot(vt, p_ref[...], preferred_element_type=F32)

    def consume(j, bias, cur, nxt, prefetch):
        s_cur, p_cur = cur
        s_nxt, p_prev = nxt
        if prefetch:
            s_nxt[...] = scores(j + 1)
        pv_prev = pv(jnp.maximum(j - 1, 0), p_prev)
        s = s_cur[...]
        if bias is not None:
            s = s + jnp.concatenate([bias, bias], axis=1)
        m_prev = m_scr[...]
        m_new = jnp.maximum(m_prev, jnp.max(s, axis=0, keepdims=True))
        alpha = jnp.exp(m_prev - m_new)
        p = jnp.exp(s - m_new)
        p_cur[...] = p.astype(BF16)
        l_scr[...] = alpha * l_scr[...] + jnp.sum(p, axis=0, keepdims=True)
        acc_scr[...] = alpha * (acc_scr[...] + pv_prev)
        m_scr[...] = m_new

    def by_parity(j, fn):
        @pl.when(j % 2 == 0)
        def _():
            fn((s0_scr, p0_scr), (s1_scr, p1_scr))

        @pl.when(j % 2 == 1)
        def _():
            fn((s1_scr, p1_scr), (s0_scr, p0_scr))

    s0_scr[...] = scores(0)
    p1_scr[...] = jnp.zeros(p1_scr.shape, BF16)

    def far_body(j, carry):
        by_parity(j, lambda cur, nxt: consume(j, None, cur, nxt, True))
        return carry

    lax.fori_loop(0, i - 1, far_body, 0)

    @pl.when(i >= 1)
    def _():
        by_parity(i - 1, lambda cur, nxt: consume(i - 1, bias_ref[0, 0], cur, nxt, True))

    def last(cur, nxt):
        consume(i, bias_ref[0, 1], cur, nxt, False)
        acc_scr[...] += pv(i, cur[1])

    by_parity(i, last)

    ot = acc_scr[...] / l_scr[...]
    ot = ot[:, :tq] - lam_ref[0] * ot[:, tq:]
    ot = ot * lax.rsqrt(jnp.mean(ot * ot, axis=0, keepdims=True) + EPS)
    o_ref[0] = (ot.T * (gsub_ref[...] * out_scale)).astype(o_ref.dtype)


def _t5_bucket(rel):
    nb = N_BUCKETS // 2
    max_exact = nb // 2
    ret = jnp.where(rel > 0, nb, 0)
    n = jnp.abs(rel)
    nf = jnp.maximum(n, 1).astype(F32)
    large = max_exact + (jnp.log(nf / max_exact) / math.log(MAX_DISTANCE / max_exact)
                         * (nb - max_exact)).astype(jnp.int32)
    large = jnp.minimum(large, nb - 1)
    return ret + jnp.where(n < max_exact, n, large)


def _attn_bias_tiles(rel_bias, tile):
    r = jnp.arange(tile)[None, :]
    c = jnp.arange(tile)[:, None]
    rb = rel_bias.astype(F32)

    def lookup(bucket):
        return jnp.einsum("...b,bh->...h", jax.nn.one_hot(bucket, N_BUCKETS, dtype=F32), rb,
                          precision=lax.Precision.HIGHEST)

    far = lookup(_t5_bucket(jnp.asarray(-(2 * tile))))
    prev = lookup(_t5_bucket(c - r - tile)) - far
    diag = lookup(_t5_bucket(c - r)) - far
    diag = jnp.where(((c // CHUNK) <= (r // CHUNK))[..., None], diag, NEG_BIG)
    return jnp.transpose(jnp.stack([prev, diag], axis=0), (3, 0, 1, 2))


def diff_attention(proj, lam, bias_tiles, subln_g, *, out_scale):
    B, S, _ = proj.shape
    H = A_HEADS
    tq = ATTN_TILE
    return pl.pallas_call(
        functools.partial(_attn_kernel, out_scale=out_scale),
        out_shape=jax.ShapeDtypeStruct((B, S, H * A_V_DIM), BF16),
        grid=(B, H, S // tq),
        in_specs=[
            pl.BlockSpec(memory_space=pltpu.SMEM),
            pl.BlockSpec((1, tq, A_V_DIM), lambda b, h, i: (b, i, h)),
            pl.BlockSpec((1, S, A_V_DIM), lambda b, h, i: (b, 0, H + h)),
            pl.BlockSpec((1, S, A_V_DIM), lambda b, h, i: (b, 0, 2 * H + h)),
            pl.BlockSpec((1, 2, tq, tq), lambda b, h, i: (h, 0, 0, 0)),
            pl.BlockSpec((1, A_V_DIM), lambda b, h, i: (0, 0)),
        ],
        out_specs=pl.BlockSpec((1, tq, A_V_DIM), lambda b, h, i: (b, i, h)),
        scratch_shapes=[
            pltpu.VMEM((A_V_DIM, S), BF16),
            pltpu.VMEM((1, 2 * tq), F32),
            pltpu.VMEM((1, 2 * tq), F32),
            pltpu.VMEM((A_V_DIM, 2 * tq), F32),
            pltpu.VMEM((tq, 2 * tq), F32),
            pltpu.VMEM((tq, 2 * tq), F32),
            pltpu.VMEM((tq, 2 * tq), BF16),
            pltpu.VMEM((tq, 2 * tq), BF16),
        ],
        compiler_params=_cparams(("parallel", "parallel", "arbitrary")),
        name="diff_attention",
    )(lam.reshape(1).astype(F32), proj, proj, proj, bias_tiles, subln_g.reshape(1, A_V_DIM).astype(F32))


CONV_HALO = 32
CONV_ROWS = 64


def _conv_kernel(val_ref, gate_ref, hval_ref, hgate_ref, w_ref, b_ref, g_ref, beta_ref, o_ref, buf_scr):
    i = pl.program_id(1)
    ts = val_ref.shape[1]

    def glu(v, gt):
        return v.astype(F32) * jax.nn.sigmoid(gt.astype(F32))

    halo = glu(hval_ref[0], hgate_ref[0])
    buf_scr[0:CONV_HALO, :] = jnp.where(i > 0, halo, 0.0)
    buf_scr[CONV_HALO:, :] = glu(val_ref[0], gate_ref[0])

    first = CONV_HALO - (CONV_WIDTH - 1)
    for r in range(ts // CONV_ROWS):
        acc = None
        for w in range(CONV_WIDTH):
            lo = r * CONV_ROWS + first + w
            term = buf_scr[lo:lo + CONV_ROWS, :] * w_ref[w:w + 1, :]
            acc = term if acc is None else acc + term
        y = acc + b_ref[...]
        mu = jnp.mean(y, axis=-1, keepdims=True)
        d = y - mu
        var = jnp.mean(d * d, axis=-1, keepdims=True)
        y = d * lax.rsqrt(var + EPS) * g_ref[...] + beta_ref[...]
        y = y * jax.nn.sigmoid(y)
        o_ref[0, r * CONV_ROWS:(r + 1) * CONV_ROWS, :] = y.astype(o_ref.dtype)


def conv_module(proj, conv_w, conv_b, ln_g, ln_b, *, ts=512):
    B, S, _ = proj.shape
    C = conv_w.shape[1]
    hb = ts // CONV_HALO
    row = lambda a: a.reshape(1, C).astype(F32)
    cur = lambda col: pl.BlockSpec((1, ts, C), lambda b, i: (b, i, col))
    halo = lambda col: pl.BlockSpec((1, CONV_HALO, C), lambda b, i: (b, jnp.maximum(i * hb - 1, 0), col))
    vec = pl.BlockSpec((1, C), lambda b, i: (0, 0))
    return pl.pallas_call(
        _conv_kernel,
        out_shape=jax.ShapeDtypeStruct((B, S, C), BF16),
        grid=(B, S // ts),
        in_specs=[cur(3), cur(4), halo(3), halo(4),
                  pl.BlockSpec((CONV_WIDTH + 1, C), lambda b, i: (0, 0)), vec, vec, vec],
        out_specs=pl.BlockSpec((1, ts, C), lambda b, i: (b, i, 0)),
        scratch_shapes=[pltpu.VMEM((ts + CONV_HALO, C), F32)],
        compiler_params=_cparams(("parallel", "parallel")),
        name="conv_module",
    )(proj, proj, proj, proj, jnp.pad(conv_w.astype(F32), ((0, 1), (0, 0))), row(conv_b), row(ln_g), row(ln_b))


GLA_DK = 128
GLA_DV = 256
GLA_Z_COLS = 128


def _split_bf16(x):
    hi = x.astype(BF16)
    lo = (x - hi.astype(F32)).astype(BF16)
    return hi, lo


def _gla_kernel(q_ref, k_ref, v_ref, g_ref, z_ref, wg_ref, bg_ref, gn_ref, o_ref, state_scr):
    ts = q_ref.shape[0]
    n_chunks = ts // CHUNK

    @pl.when(pl.program_id(1) == 0)
    def _():
        state_scr[...] = jnp.zeros(state_scr.shape, F32)

    zz = jnp.dot(z_ref[...], wg_ref[...], preferred_element_type=F32) + bg_ref[...]
    log_a = (jnp.minimum(zz, 0.0) - jnp.log1p(jnp.exp(-jnp.abs(zz)))) / GATE_TAU

    row = lax.broadcasted_iota(jnp.int32, (CHUNK, CHUNK), 0)
    col = lax.broadcasted_iota(jnp.int32, (CHUNK, CHUNK), 1)
    tri = jnp.where(col <= row, 1.0, 0.0).astype(BF16)
    ones = jnp.ones((CHUNK, LANES), BF16)
    tn = (((0,), (0,)), ((), ()))
    q_scale = GLA_DK ** -0.5

    for c in range(n_chunks):
        rows = slice(c * CHUNK, (c + 1) * CHUNK)
        la_hi, la_lo = _split_bf16(log_a[rows, :])
        cum = (jnp.dot(tri, la_hi, preferred_element_type=F32)
               + jnp.dot(tri, la_lo, preferred_element_type=F32))
        k_dec = k_ref[rows, :].astype(F32) * jnp.exp(cum[CHUNK - 1:CHUNK, :] - cum)
        for h in range(GLA_HEADS):
            kc = slice(h * GLA_DK, (h + 1) * GLA_DK)
            vc = slice(h * GLA_DV, (h + 1) * GLA_DV)
            tot = (lax.dot_general(la_hi[:, kc], ones, tn, preferred_element_type=F32)
                   + lax.dot_general(la_lo[:, kc], ones, tn, preferred_element_type=F32))
            decay = jnp.exp(tot)
            decay = jnp.concatenate([decay] * (GLA_DV // LANES), axis=1)
            kv = lax.dot_general(k_dec[:, kc].astype(BF16), v_ref[rows, vc], tn,
                                 preferred_element_type=F32)
            state = state_scr[h] * decay + kv
            state_scr[h] = state
            o = jnp.dot(q_ref[rows, kc], state.astype(BF16), preferred_element_type=F32) * q_scale
            o = o * lax.rsqrt(jnp.mean(o * o, axis=-1, keepdims=True) + EPS) * gn_ref[...]
            gt = g_ref[rows, vc].astype(F32)
            o_ref[rows, vc] = (o * (gt * jax.nn.sigmoid(gt))).astype(o_ref.dtype)


def gla_scan(proj, w_gate, b_gate, norm_g, *, batch, ts=512):
    T = proj.shape[0]
    S = T // batch
    nb = S // ts
    KD = GLA_HEADS * GLA_DK
    VD = GLA_HEADS * GLA_DV
    rows = lambda width, col: pl.BlockSpec((ts, width), lambda b, i: (b * nb + i, col))
    return pl.pallas_call(
        _gla_kernel,
        out_shape=jax.ShapeDtypeStruct((T, VD), BF16),
        grid=(batch, nb),
        in_specs=[
            rows(KD, 0), rows(KD, 1), rows(VD, 1), rows(VD, 2), rows(GLA_Z_COLS, (2 * KD + 2 * VD) // GLA_Z_COLS),
            pl.BlockSpec((GLA_Z_COLS, KD), lambda b, i: (0, 0)),
            pl.BlockSpec((1, KD), lambda b, i: (0, 0)),
            pl.BlockSpec((1, GLA_DV), lambda b, i: (0, 0)),
        ],
        out_specs=pl.BlockSpec((ts, VD), lambda b, i: (b * nb + i, 0)),
        scratch_shapes=[pltpu.VMEM((GLA_HEADS, GLA_DK, GLA_DV), F32)],
        compiler_params=_cparams(("parallel", "arbitrary")),
        name="gla_scan",
    )(proj, proj, proj, proj, proj, w_gate, b_gate.reshape(1, KD).astype(F32),
      norm_g.reshape(1, GLA_DV).astype(F32))


def kernel(x, rel_bias, ln_mix, ln_ffn, even_w_in, lam_q1, lam_k1, lam_q2, lam_k2, subln_g, conv_w, conv_b, conv_ln_g, conv_ln_b, even_w_out, odd_w_in, w_gate2, b_gate2, gla_norm_g, odd_w_out, peer_w_query, peer_sub_keys, peer_u, peer_v, ln_final):
    B, S, D = x.shape
    T = B * S
    xf = x.reshape(T, D)

    lam_init = 0.8 - 0.6 * math.exp(-0.3 * 0)
    lam = (jnp.exp(jnp.sum(lam_q1[0].astype(F32) * lam_k1[0].astype(F32)))
           - jnp.exp(jnp.sum(lam_q2[0].astype(F32) * lam_k2[0].astype(F32))) + lam_init)
    proj = norm_matmul(xf, ln_mix[0], even_w_in[0].astype(BF16)).reshape(B, S, -1)
    a_out = diff_attention(proj, lam, _attn_bias_tiles(rel_bias, ATTN_TILE), subln_g[0],
                           out_scale=1.0 - lam_init)
    c_out = conv_module(proj, conv_w[0], conv_b[0], conv_ln_g[0], conv_ln_b[0])
    a_width = A_HEADS * A_V_DIM
    w_out = even_w_out[0].astype(BF16)
    xf = proj_residual(xf, a_out.reshape(T, -1), c_out.reshape(T, -1), w_out[:a_width], w_out[a_width:])
    xf = peer_layer(xf, ln_ffn[0], peer_w_query[0], peer_sub_keys[0], peer_u[0], peer_v[0], ln_final,
                    final_norm=False)

    kd = GLA_HEADS * GLA_DK
    vd = GLA_HEADS * GLA_DV
    w_in = jnp.pad(odd_w_in[0], ((0, 0), (0, GLA_Z_COLS - GATE_RANK))).astype(BF16)
    w_gate = jnp.pad(w_gate2[0], ((0, GLA_Z_COLS - GATE_RANK), (0, 0))).astype(BF16)
    proj = norm_matmul(xf, ln_mix[1], w_in)
    og = gla_scan(proj, w_gate, b_gate2[0], gla_norm_g[0], batch=B)
    w_out = odd_w_out[0].astype(BF16)
    xf = proj_residual(xf, og, og, w_out[:vd // 2], w_out[vd // 2:], a_col=0, c_col=1)
    xf = peer_layer(xf, ln_ffn[1], peer_w_query[1], peer_sub_keys[1], peer_u[1], peer_v[1], ln_final,
                    final_norm=True)
    return xf.reshape(B, S, D)
```

```python
import functools
import math

import jax
import jax.numpy as jnp
import numpy as np
from jax import lax
from jax.experimental import pallas as pl
from jax.experimental.pallas import tpu as pltpu

F32 = jnp.float32
BF16 = jnp.bfloat16

EPS = 1e-6
LANES = 128
VMEM_LIMIT = 56 * 1024 * 1024

CHUNK = 64
N_BUCKETS = 32
MAX_DISTANCE = 128
A_HEADS = 4
A_QK_DIM = 64
A_V_DIM = 128
CONV_WIDTH = 31
GLA_HEADS = 4
GATE_RANK = 16
GATE_TAU = 16.0
PEER_HEADS = 8
N_KEYS = 128
PEER_TOPK = 16
PEER_HALF = 128
NEG_BIG = -1e30


def _cparams(sem):
    return pltpu.CompilerParams(dimension_semantics=sem, vmem_limit_bytes=VMEM_LIMIT)


def _rms_normed(x, g):
    ms = jnp.mean(x * x, axis=-1, keepdims=True)
    return x * lax.rsqrt(ms + EPS) * g


def _norm_matmul_kernel(x_ref, g_ref, w_ref, o_ref):
    xn = _rms_normed(x_ref[...], g_ref[...]).astype(BF16)
    o_ref[...] = jnp.dot(xn, w_ref[...], preferred_element_type=F32).astype(o_ref.dtype)


def norm_matmul(x, g, w, *, tm=512, out_dtype=BF16):
    T, D = x.shape
    N = w.shape[1]
    return pl.pallas_call(
        _norm_matmul_kernel,
        out_shape=jax.ShapeDtypeStruct((T, N), out_dtype),
        grid=(T // tm,),
        in_specs=[
            pl.BlockSpec((tm, D), lambda i: (i, 0)),
            pl.BlockSpec((1, D), lambda i: (0, 0)),
            pl.BlockSpec((D, N), lambda i: (0, 0)),
        ],
        out_specs=pl.BlockSpec((tm, N), lambda i: (i, 0)),
        compiler_params=_cparams(("parallel",)),
        name="norm_matmul",
    )(x, g.reshape(1, D).astype(F32), w)


def _proj_residual_kernel(x_ref, a_ref, c_ref, wa_ref, wc_ref, o_ref):
    acc = jnp.dot(a_ref[...], wa_ref[...], preferred_element_type=F32)
    acc = acc + jnp.dot(c_ref[...], wc_ref[...], preferred_element_type=F32)
    o_ref[...] = x_ref[...] + acc


def proj_residual(x, a, c, wa, wc, *, a_col=0, c_col=0, tm=512):
    T, D = x.shape
    Ka, Kc = wa.shape[0], wc.shape[0]
    return pl.pallas_call(
        _proj_residual_kernel,
        out_shape=jax.ShapeDtypeStruct((T, D), F32),
        grid=(T // tm,),
        in_specs=[
            pl.BlockSpec((tm, D), lambda i: (i, 0)),
            pl.BlockSpec((tm, Ka), lambda i: (i, a_col)),
            pl.BlockSpec((tm, Kc), lambda i: (i, c_col)),
            pl.BlockSpec((Ka, D), lambda i: (0, 0)),
            pl.BlockSpec((Kc, D), lambda i: (0, 0)),
        ],
        out_specs=pl.BlockSpec((tm, D), lambda i: (i, 0)),
        compiler_params=_cparams(("parallel",)),
        name="proj_residual",
    )(x, a, c, wa, wc)


def _pair_candidates():
    return [(a, b) for a in range(PEER_TOPK) for b in range(PEER_TOPK) if (a + 1) * (b + 1) <= PEER_TOPK]


def _dup_bf16_bits(v):
    hi = pltpu.bitcast(v.astype(BF16).astype(F32), jnp.uint32)
    return hi | (hi >> 16)


PAIR = 8


def _pack_bf16_pairs(lo, hi):
    bl = pltpu.bitcast(lo.astype(BF16).astype(F32), jnp.uint32)
    bh = pltpu.bitcast(hi.astype(BF16).astype(F32), jnp.uint32)
    return (bl >> 16) | bh


def _pair_rows(scr, idx, grp, half):
    n_rows = scr.shape[2]
    return jnp.concatenate([scr[idx, grp, g + PAIR * half:g + PAIR * (half + 1), :]
                            for g in range(0, n_rows, 2 * PAIR)], axis=0)


def _route_kernel(x_ref, g_ref, wqT_ref, sk_ref, rank2_ref, e2_ref, r1_ref, e1z_ref,
                  xn_scr, s_scr, rank_scr, top_scr, r1top_scr, stat_scr):
    tm = x_ref.shape[0]
    n_groups = tm // LANES
    xn_scr[...] = _rms_normed(x_ref[...], g_ref[...]).astype(BF16)

    def score_body(h, carry):
        w_h = wqT_ref[pl.ds(pl.multiple_of(h * 2 * PEER_HALF, 2 * PEER_HALF), 2 * PEER_HALF), :]
        qT = lax.dot_general(w_h, xn_scr[...], (((1,), (1,)), ((), ())),
                             preferred_element_type=F32).astype(BF16)
        for p in range(2):
            s = jnp.dot(sk_ref[2 * h + p], qT[p * PEER_HALF:(p + 1) * PEER_HALF, :],
                        preferred_element_type=F32)
            for grp in range(n_groups):
                s_scr[2 * h + p, grp] = s[:, grp * LANES:(grp + 1) * LANES]
        return carry

    lax.fori_loop(0, PEER_HEADS, score_body, 0)

    def top_body(hp, carry):
        for grp in range(n_groups):
            work = s_scr[hp, grp]
            rank = jnp.full(work.shape, float(PEER_TOPK), F32)
            for k in range(PEER_TOPK):
                m = jnp.max(work, axis=0, keepdims=True)
                eq = work == m
                rank = jnp.where(eq, float(k), rank)
                work = jnp.where(eq, -jnp.inf, work)
                top_scr[hp, k, grp:grp + 1, :] = m
            rank_scr[hp, grp] = rank
        return carry

    lax.fori_loop(0, 2 * PEER_HEADS, top_body, 0)

    cands = _pair_candidates()

    def pair_body(hh, carry):
        h = 2 * hh

        def both(scr, idx, k):
            return jnp.concatenate([scr[idx(h), k], scr[idx(h + 1), k]], axis=0)

        v1 = [both(top_scr, lambda x: 2 * x, a) for a in range(PEER_TOPK)]
        v2 = [both(top_scr, lambda x: 2 * x + 1, b) for b in range(PEER_TOPK)]
        sums = [v1[a] + v2[b] for (a, b) in cands]
        work = list(sums)
        tau = None
        for k in range(PEER_TOPK):
            tau = functools.reduce(jnp.maximum, work)
            if k + 1 < PEER_TOPK:
                work = [jnp.where(w == tau, -jnp.inf, w) for w in work]
        cmax = v1[0] + v2[0]
        z = jnp.zeros_like(tau)
        r1 = [jnp.zeros_like(tau) for _ in range(PEER_TOPK)]
        for (a, b), c in zip(cands, sums):
            sel = c >= tau
            z = z + jnp.where(sel, jnp.exp(c - cmax), 0.0)
            r1[a] = r1[a] + jnp.where(sel, 1.0, 0.0)
        inv_z = 1.0 / z
        for d in range(2):
            part = slice(d * n_groups, (d + 1) * n_groups)
            for a in range(PEER_TOPK):
                r1top_scr[h + d, a] = r1[a][part]
            stat_scr[h + d, 0] = v1[0][part]
            stat_scr[h + d, 1] = v2[0][part]
            stat_scr[h + d, 2] = inv_z[part]
        return carry

    lax.fori_loop(0, PEER_HEADS // 2, pair_body, 0)

    def table_body(h, carry):
        for grp in range(n_groups):
            lanes = slice(grp * LANES, (grp + 1) * LANES)
            max1 = stat_scr[h, 0, grp:grp + 1, :]
            max2 = stat_scr[h, 1, grp:grp + 1, :]
            inv_z = stat_scr[h, 2, grp:grp + 1, :]
            rank1 = rank_scr[2 * h, grp]
            r1_key = jnp.zeros(rank1.shape, F32)
            for a in range(PEER_TOPK):
                r1_key = jnp.where(rank1 == float(a), r1top_scr[h, a, grp:grp + 1, :], r1_key)
            r1_ref[h, :, lanes] = _dup_bf16_bits(r1_key)
            e1z = jnp.exp(s_scr[2 * h, grp] - max1) * inv_z
            e1z_ref[h, :, lanes] = _dup_bf16_bits(e1z)
            rk_lo = _pair_rows(rank_scr, 2 * h + 1, grp, 0)
            rk_hi = _pair_rows(rank_scr, 2 * h + 1, grp, 1)
            rank2_ref[h, :, lanes] = _pack_bf16_pairs(rk_lo, rk_hi)
            e2_lo = jnp.exp(jnp.minimum(_pair_rows(s_scr, 2 * h + 1, grp, 0) - max2, 0.0))
            e2_hi = jnp.exp(jnp.minimum(_pair_rows(s_scr, 2 * h + 1, grp, 1) - max2, 0.0))
            e2_ref[h, :, lanes] = _pack_bf16_pairs(e2_lo, e2_hi)
        return carry

    lax.fori_loop(0, PEER_HEADS, table_body, 0)


def peer_route(x, g, wqT, sk, *, tm=512):
    T, D = x.shape
    H = PEER_HEADS
    n_groups = tm // LANES
    tab = lambda rows: jax.ShapeDtypeStruct((H, rows, T), jnp.uint32)
    tab_spec = lambda rows: pl.BlockSpec((H, rows, tm), lambda i: (0, 0, i))
    return pl.pallas_call(
        _route_kernel,
        out_shape=(tab(N_KEYS // 2), tab(N_KEYS // 2), tab(N_KEYS), tab(N_KEYS)),
        grid=(T // tm,),
        in_specs=[
            pl.BlockSpec((tm, D), lambda i: (i, 0)),
            pl.BlockSpec((1, D), lambda i: (0, 0)),
            pl.BlockSpec((H * 2 * PEER_HALF, D), lambda i: (0, 0)),
            pl.BlockSpec((2 * H, N_KEYS, PEER_HALF), lambda i: (0, 0, 0)),
        ],
        out_specs=(tab_spec(N_KEYS // 2), tab_spec(N_KEYS // 2), tab_spec(N_KEYS), tab_spec(N_KEYS)),
        scratch_shapes=[
            pltpu.VMEM((tm, D), BF16),
            pltpu.VMEM((2 * H, n_groups, N_KEYS, LANES), F32),
            pltpu.VMEM((2 * H, n_groups, N_KEYS, LANES), F32),
            pltpu.VMEM((2 * H, PEER_TOPK, n_groups, LANES), F32),
            pltpu.VMEM((H, PEER_TOPK, n_groups, LANES), F32),
            pltpu.VMEM((H, 3, n_groups, LANES), F32),
        ],
        compiler_params=_cparams(("parallel",)),
        name="peer_route",
    )(x, g.reshape(1, D).astype(F32), wqT, sk)


GELU_C0 = math.sqrt(2.0 / math.pi)
GELU_C1 = 0.044715


def _gelu_tanh(a):
    inner = GELU_C0 * (a + GELU_C1 * (a * a * a))
    return 0.5 * a * (1.0 + jnp.tanh(inner))


PEER_SUB = 512


def _peer_main_kernel(x_ref, g_ref, rank2_ref, e2_ref, r1_ref, e1z_ref, u_ref, vt_ref, gf_ref, o_ref,
                      xn_scr, acc_scr, act_scr, coef_scr, *, final_norm):
    j = pl.program_id(1)
    tm = x_ref.shape[0]
    n_i1 = r1_ref.shape[1]
    rows = 16
    words = rows * jnp.dtype(BF16).itemsize // 4

    @pl.when(j == 0)
    def _():
        xn_scr[...] = _rms_normed(x_ref[...], g_ref[...]).astype(BF16)
        acc_scr[...] = jnp.zeros_like(acc_scr)

    eb = 2 * u_ref.shape[0]
    d_model = 2 * vt_ref.shape[0]
    tn = 2 * LANES

    def stage_a(n, q):
        r = slice(q * PEER_SUB, (q + 1) * PEER_SUB)
        rw = slice(q * PEER_SUB // 2, (q + 1) * PEER_SUB // 2)
        c = slice(n * tn, (n + 1) * tn)
        act_scr[n, r, :] = lax.dot_general(pltpu.bitcast(u_ref[rw, :], BF16), xn_scr[c, :],
                                           (((1,), (1,)), ((), ())), preferred_element_type=F32)

    def stage_b(n, q):
        for sub in range(tn // LANES):
            lanes = slice(n * tn + sub * LANES, n * tn + (sub + 1) * LANES)
            sl = slice(sub * LANES, (sub + 1) * LANES)
            n_blk = N_KEYS // rows
            for i1 in range(q * PEER_SUB // N_KEYS, (q + 1) * PEER_SUB // N_KEYS, 2):
                gate = [[None] * n_blk, [None] * n_blk]
                for h in range(PEER_HEADS):
                    r1b, e1b = [], []
                    for d in range(2):
                        r1row = jnp.broadcast_to(r1_ref[h, i1 + d:i1 + d + 1, lanes], (words, LANES))
                        e1row = jnp.broadcast_to(e1z_ref[h, i1 + d:i1 + d + 1, lanes], (words, LANES))
                        r1b.append(pltpu.bitcast(r1row, BF16))
                        e1b.append(pltpu.bitcast(e1row, BF16))
                    for blk in range(n_blk):
                        i2w = slice(blk * words, (blk + 1) * words)
                        rk = pltpu.bitcast(rank2_ref[h, i2w, lanes], BF16)
                        ev = pltpu.bitcast(e2_ref[h, i2w, lanes], BF16)
                        for d in range(2):
                            term = jnp.where(rk < r1b[d], ev, jnp.zeros((), BF16)) * e1b[d]
                            gate[d][blk] = term if gate[d][blk] is None else gate[d][blk] + term
                for d in range(2):
                    for blk in range(n_blk):
                        r0 = (i1 + d) * N_KEYS + blk * rows
                        a = act_scr[n, r0:r0 + rows, sl].astype(BF16)
                        coef_scr[n, r0:r0 + rows, sl] = _gelu_tanh(a) * gate[d][blk]

    def stage_c(n):
        for half in range(2):
            r = slice(half * (d_model // 2), (half + 1) * (d_model // 2))
            rw = slice(half * (d_model // 4), (half + 1) * (d_model // 4))
            res = jnp.dot(pltpu.bitcast(vt_ref[rw, :], BF16), coef_scr[n], preferred_element_type=F32)
            for sub in range(tn // LANES):
                acc_scr[n * (tn // LANES) + sub, r, :] += res[:, sub * LANES:(sub + 1) * LANES]

    for n in range(tm // tn):
        for q in range(eb // PEER_SUB):
            stage_a(n, q)
            stage_b(n, q)
        stage_c(n)

    @pl.when(j == pl.num_programs(1) - 1)
    def _():
        lo_rows = pl.ds(0, d_model // 2, stride=2)
        hi_rows = pl.ds(1, d_model // 2, stride=2)
        peer_out = jnp.concatenate(
            [jnp.concatenate([acc_scr[grp, lo_rows, :], acc_scr[grp, hi_rows, :]], axis=0).T
             for grp in range(tm // LANES)], axis=0)
        y = x_ref[...] + peer_out
        if final_norm:
            y = _rms_normed(y, gf_ref[...])
        o_ref[...] = y


def _bf16_words(lo, hi):
    bits = lambda a: lax.bitcast_convert_type(a.astype(BF16), jnp.uint16).astype(jnp.uint32)
    return bits(lo) | (bits(hi) << 16)


def _pack_expert_rows(u):
    e, d = u.shape
    u4 = u.reshape(e // (2 * PAIR), 2, PAIR, d)
    return _bf16_words(u4[:, 0], u4[:, 1]).reshape(e // 2, d)


def _pack_vt(v):
    e, d = v.shape
    v_perm = v.reshape(e // (2 * PAIR), 2, PAIR, d).transpose(0, 2, 1, 3).reshape(e, d)
    vt = v_perm.T
    return _bf16_words(vt[:d // 2], vt[d // 2:])


def peer_main(x, g, tables, u, vt, g_final, *, final_norm, tm=512, eb=2048):
    T, D = x.shape
    E = 2 * u.shape[0]
    H = PEER_HEADS
    rank2, e2, r1, e1z = tables
    n_i1 = eb // N_KEYS
    full_tab = pl.BlockSpec((H, N_KEYS // 2, tm), lambda i, j: (0, 0, i))
    i1_tab = pl.BlockSpec((H, n_i1, tm), lambda i, j: (0, j, i))
    return pl.pallas_call(
        functools.partial(_peer_main_kernel, final_norm=final_norm),
        out_shape=jax.ShapeDtypeStruct((T, D), F32),
        grid=(T // tm, E // eb),
        in_specs=[
            pl.BlockSpec((tm, D), lambda i, j: (i, 0)),
            pl.BlockSpec((1, D), lambda i, j: (0, 0)),
            full_tab, full_tab, i1_tab, i1_tab,
            pl.BlockSpec((eb // 2, D), lambda i, j: (j, 0)),
            pl.BlockSpec((D // 2, eb), lambda i, j: (0, j)),
            pl.BlockSpec((1, D), lambda i, j: (0, 0)),
        ],
        out_specs=pl.BlockSpec((tm, D), lambda i, j: (i, 0)),
        scratch_shapes=[
            pltpu.VMEM((tm, D), BF16),
            pltpu.VMEM((tm // LANES, D, LANES), F32),
            pltpu.VMEM((tm // (2 * LANES), eb, 2 * LANES), F32),
            pltpu.VMEM((tm // (2 * LANES), eb, 2 * LANES), BF16),
        ],
        compiler_params=_cparams(("parallel", "arbitrary")),
        name="peer_main",
    )(x, g.reshape(1, D).astype(F32), rank2, e2, r1, e1z, u, vt, g_final.reshape(1, D).astype(F32))


def peer_layer(x, g, w_query, sub_keys, expert_u, expert_v, g_final, *, final_norm):
    wqT = w_query.T.astype(BF16)
    sk = sub_keys.reshape(2 * PEER_HEADS, N_KEYS, PEER_HALF).astype(BF16)
    tables = peer_route(x, g, wqT, sk)
    return peer_main(x, g, tables, _pack_expert_rows(expert_u), _pack_vt(expert_v), g_final,
                     final_norm=final_norm)


ATTN_TILE = 256
ATTN_GROUP = 1
ATTN_UNROLL = 4
ATTN_ONES = 16


def _attn_kernel(lam_ref, q_ref, k_ref, v_ref, bias_ref, gsub_ref, o_ref, vt_scr, m_scr, acc_scr,
                 s0_scr, s1_scr, p0_scr, p1_scr, *, out_scale):
    i = pl.program_id(2)
    tq = q_ref.shape[1]
    tk = tq
    heads = range(q_ref.shape[2] // A_V_DIM)
    cols = lambda g: slice(g * A_V_DIM, (g + 1) * A_V_DIM)

    @pl.when(i == 0)
    def _():
        for g in heads:
            vt_scr[g, :A_V_DIM, :] = v_ref[0, :, cols(g)].T
            vt_scr[g, A_V_DIM:, :] = jnp.ones((ATTN_ONES, vt_scr.shape[2]), BF16)

    scale = jnp.asarray(A_QK_DIM ** -0.5, BF16)
    zero = jnp.zeros((), BF16)
    qs = []
    for g in heads:
        q = q_ref[0, :, cols(g)]
        lane = lax.broadcasted_iota(jnp.int32, q.shape, 1)
        qs.append(jnp.concatenate([jnp.where(lane < A_QK_DIM, q, zero) * scale,
                                   jnp.where(lane >= A_QK_DIM, q, zero) * scale], axis=0))

    m_scr[...] = jnp.full(m_scr.shape, NEG_BIG, F32)
    acc_scr[...] = jnp.zeros(acc_scr.shape, F32)

    def scores(j, g):
        k = k_ref[0, pl.ds(pl.multiple_of(j * tk, tk), tk), cols(g)]
        return lax.dot_general(k, qs[g], (((1,), (1,)), ((), ())), preferred_element_type=F32)

    def pv(j, p_ref, g):
        vt = vt_scr[g, :, pl.ds(pl.multiple_of(j * tk, tk), tk)]
        return jnp.dot(vt, p_ref[g], preferred_element_type=F32)

    def consume(j, bias_idx, cur, nxt, prefetch):
        s_cur, p_cur = cur
        s_nxt, p_prev = nxt
        for g in heads:
            if prefetch:
                s_nxt[g] = scores(j + 1, g)
            pv_prev = pv(jnp.maximum(j - 1, 0), p_prev, g)
            s = s_cur[g]
            if bias_idx is not None:
                bias = bias_ref[g, bias_idx]
                s = s + jnp.concatenate([bias, bias], axis=1)
            m_prev = m_scr[g]
            m_new = jnp.maximum(m_prev, jnp.max(s, axis=0, keepdims=True))
            alpha = jnp.exp(m_prev - m_new)
            p_cur[g] = jnp.exp((s - m_new).astype(BF16))
            acc_scr[g] = alpha * (acc_scr[g] + pv_prev)
            m_scr[g] = m_new

    def by_parity(j, fn):
        @pl.when(j % 2 == 0)
        def _():
            fn((s0_scr, p0_scr), (s1_scr, p1_scr))

        @pl.when(j % 2 == 1)
        def _():
            fn((s1_scr, p1_scr), (s0_scr, p0_scr))

    for g in heads:
        s0_scr[g] = scores(0, g)
    p1_scr[...] = jnp.zeros(p1_scr.shape, BF16)

    n_far = jnp.maximum(i - 1, 0)
    even_bufs, odd_bufs = (s0_scr, p0_scr), (s1_scr, p1_scr)

    def far_tiles(first, count):
        for t in range(count):
            cur, nxt = (even_bufs, odd_bufs) if t % 2 == 0 else (odd_bufs, even_bufs)
            consume(first + t, None, cur, nxt, True)

    def far_body(jj, carry):
        far_tiles(ATTN_UNROLL * jj, ATTN_UNROLL)
        return carry

    lax.fori_loop(0, n_far // ATTN_UNROLL, far_body, 0)
    rest = n_far % ATTN_UNROLL
    done = n_far - rest

    @pl.when(rest >= 2)
    def _():
        far_tiles(done, 2)

    @pl.when(rest % 2 == 1)
    def _():
        far_tiles(n_far - 1, 1)

    @pl.when(i >= 1)
    def _():
        by_parity(i - 1, lambda cur, nxt: consume(i - 1, 0, cur, nxt, True))

    def last(cur, nxt):
        consume(i, 1, cur, nxt, False)
        for g in heads:
            acc_scr[g] += pv(i, cur[1], g)

    by_parity(i, last)

    for g in heads:
        ot = acc_scr[g, :A_V_DIM, :] / acc_scr[g, A_V_DIM:A_V_DIM + 1, :]
        ot = ot[:, :tq] - lam_ref[0] * ot[:, tq:]
        ot = ot * lax.rsqrt(jnp.mean(ot * ot, axis=0, keepdims=True) + EPS)
        o_ref[0, :, cols(g)] = (ot.T * (gsub_ref[...] * out_scale)).astype(o_ref.dtype)


def _t5_bucket(rel):
    nb = N_BUCKETS // 2
    max_exact = nb // 2
    ret = jnp.where(rel > 0, nb, 0)
    n = jnp.abs(rel)
    nf = jnp.maximum(n, 1).astype(F32)
    large = max_exact + (jnp.log(nf / max_exact) / math.log(MAX_DISTANCE / max_exact)
                         * (nb - max_exact)).astype(jnp.int32)
    large = jnp.minimum(large, nb - 1)
    return ret + jnp.where(n < max_exact, n, large)


def _attn_bias_tiles(rel_bias, tile):
    r = jnp.arange(tile)[None, :]
    c = jnp.arange(tile)[:, None]
    rb = rel_bias.astype(F32)

    def lookup(bucket):
        return jnp.einsum("...b,bh->...h", jax.nn.one_hot(bucket, N_BUCKETS, dtype=F32), rb,
                          precision=lax.Precision.HIGHEST)

    far = lookup(_t5_bucket(jnp.asarray(-(2 * tile))))
    prev = lookup(_t5_bucket(c - r - tile)) - far
    diag = lookup(_t5_bucket(c - r)) - far
    diag = jnp.where(((c // CHUNK) <= (r // CHUNK))[..., None], diag, NEG_BIG)
    return jnp.transpose(jnp.stack([prev, diag], axis=0), (3, 0, 1, 2))


def diff_attention(proj, lam, bias_tiles, subln_g, *, out_scale):
    B, S, _ = proj.shape
    H = A_HEADS
    G = ATTN_GROUP
    tq = ATTN_TILE
    width = G * A_V_DIM
    return pl.pallas_call(
        functools.partial(_attn_kernel, out_scale=out_scale),
        out_shape=jax.ShapeDtypeStruct((B, S, H * A_V_DIM), BF16),
        grid=(B, H // G, S // tq),
        in_specs=[
            pl.BlockSpec(memory_space=pltpu.SMEM),
            pl.BlockSpec((1, tq, width), lambda b, h, i: (b, i, h)),
            pl.BlockSpec((1, S, width), lambda b, h, i: (b, 0, H // G + h)),
            pl.BlockSpec((1, S, width), lambda b, h, i: (b, 0, 2 * (H // G) + h)),
            pl.BlockSpec((G, 2, tq, tq), lambda b, h, i: (h, 0, 0, 0)),
            pl.BlockSpec((1, A_V_DIM), lambda b, h, i: (0, 0)),
        ],
        out_specs=pl.BlockSpec((1, tq, width), lambda b, h, i: (b, i, h)),
        scratch_shapes=[
            pltpu.VMEM((G, A_V_DIM + ATTN_ONES, S), BF16),
            pltpu.VMEM((G, 1, 2 * tq), F32),
            pltpu.VMEM((G, A_V_DIM + ATTN_ONES, 2 * tq), F32),
            pltpu.VMEM((G, tq, 2 * tq), F32),
            pltpu.VMEM((G, tq, 2 * tq), F32),
            pltpu.VMEM((G, tq, 2 * tq), BF16),
            pltpu.VMEM((G, tq, 2 * tq), BF16),
        ],
        compiler_params=_cparams(("parallel", "parallel", "arbitrary")),
        name="diff_attention",
    )(lam.reshape(1).astype(F32), proj, proj, proj, bias_tiles, subln_g.reshape(1, A_V_DIM).astype(F32))


CONV_HALO = 32
CONV_ROWS = 64


def _conv_kernel(val_ref, gate_ref, hval_ref, hgate_ref, w_ref, b_ref, g_ref, beta_ref, o_ref, buf_scr):
    i = pl.program_id(1)
    ts = val_ref.shape[1]

    def glu(v, gt):
        return v.astype(F32) * jax.nn.sigmoid(gt.astype(F32))

    halo = glu(hval_ref[0], hgate_ref[0])
    buf_scr[0:CONV_HALO, :] = jnp.where(i > 0, halo, 0.0)
    buf_scr[CONV_HALO:, :] = glu(val_ref[0], gate_ref[0])

    first = CONV_HALO - (CONV_WIDTH - 1)
    for r in range(ts // CONV_ROWS):
        acc = None
        for w in range(CONV_WIDTH):
            lo = r * CONV_ROWS + first + w
            term = buf_scr[lo:lo + CONV_ROWS, :] * w_ref[w:w + 1, :]
            acc = term if acc is None else acc + term
        y = acc + b_ref[...]
        mu = jnp.mean(y, axis=-1, keepdims=True)
        d = y - mu
        var = jnp.mean(d * d, axis=-1, keepdims=True)
        y = d * lax.rsqrt(var + EPS) * g_ref[...] + beta_ref[...]
        y = y * jax.nn.sigmoid(y)
        o_ref[0, r * CONV_ROWS:(r + 1) * CONV_ROWS, :] = y.astype(o_ref.dtype)


def conv_module(proj, conv_w, conv_b, ln_g, ln_b, *, ts=512):
    B, S, _ = proj.shape
    C = conv_w.shape[1]
    hb = ts // CONV_HALO
    row = lambda a: a.reshape(1, C).astype(F32)
    cur = lambda col: pl.BlockSpec((1, ts, C), lambda b, i: (b, i, col))
    halo = lambda col: pl.BlockSpec((1, CONV_HALO, C), lambda b, i: (b, jnp.maximum(i * hb - 1, 0), col))
    vec = pl.BlockSpec((1, C), lambda b, i: (0, 0))
    return pl.pallas_call(
        _conv_kernel,
        out_shape=jax.ShapeDtypeStruct((B, S, C), BF16),
        grid=(B, S // ts),
        in_specs=[cur(3), cur(4), halo(3), halo(4),
                  pl.BlockSpec((CONV_WIDTH + 1, C), lambda b, i: (0, 0)), vec, vec, vec],
        out_specs=pl.BlockSpec((1, ts, C), lambda b, i: (b, i, 0)),
        scratch_shapes=[pltpu.VMEM((ts + CONV_HALO, C), F32)],
        compiler_params=_cparams(("parallel", "parallel")),
        name="conv_module",
    )(proj, proj, proj, proj, jnp.pad(conv_w.astype(F32), ((0, 1), (0, 0))), row(conv_b), row(ln_g), row(ln_b))


GLA_DK = 128
GLA_DV = 256
GLA_Z_COLS = 128


def _split_bf16(x):
    hi = x.astype(BF16)
    lo = (x - hi.astype(F32)).astype(BF16)
    return hi, lo


def _gla_kernel(q_ref, k_ref, v_ref, g_ref, z_ref, wg_ref, bg_ref, gn_ref, o_ref, state_scr):
    ts = q_ref.shape[0]
    n_chunks = ts // CHUNK

    @pl.when(pl.program_id(1) == 0)
    def _():
        state_scr[...] = jnp.zeros(state_scr.shape, F32)

    zz = jnp.dot(z_ref[...], wg_ref[...], preferred_element_type=F32) + bg_ref[...]
    log_a = (jnp.minimum(zz, 0.0) - jnp.log1p(jnp.exp(-jnp.abs(zz)))) / GATE_TAU

    row = lax.broadcasted_iota(jnp.int32, (CHUNK, CHUNK), 0)
    col = lax.broadcasted_iota(jnp.int32, (CHUNK, CHUNK), 1)
    tri = jnp.where(col <= row, 1.0, 0.0).astype(BF16)
    ones = jnp.ones((CHUNK, LANES), BF16)
    tn = (((0,), (0,)), ((), ()))
    q_scale = GLA_DK ** -0.5

    for c in range(n_chunks):
        rows = slice(c * CHUNK, (c + 1) * CHUNK)
        la_hi, la_lo = _split_bf16(log_a[rows, :])
        cum = (jnp.dot(tri, la_hi, preferred_element_type=F32)
               + jnp.dot(tri, la_lo, preferred_element_type=F32))
        k_dec = k_ref[rows, :].astype(F32) * jnp.exp(cum[CHUNK - 1:CHUNK, :] - cum)
        for h in range(GLA_HEADS):
            kc = slice(h * GLA_DK, (h + 1) * GLA_DK)
            vc = slice(h * GLA_DV, (h + 1) * GLA_DV)
            tot = (lax.dot_general(la_hi[:, kc], ones, tn, preferred_element_type=F32)
                   + lax.dot_general(la_lo[:, kc], ones, tn, preferred_element_type=F32))
            decay = jnp.exp(tot)
            decay = jnp.concatenate([decay] * (GLA_DV // LANES), axis=1)
            kv = lax.dot_general(k_dec[:, kc].astype(BF16), v_ref[rows, vc], tn,
                                 preferred_element_type=F32)
            state = state_scr[h] * decay + kv
            state_scr[h] = state
            o = jnp.dot(q_ref[rows, kc], state.astype(BF16), preferred_element_type=F32) * q_scale
            o = o * lax.rsqrt(jnp.mean(o * o, axis=-1, keepdims=True) + EPS) * gn_ref[...]
            gt = g_ref[rows, vc].astype(F32)
            o_ref[rows, vc] = (o * (gt * jax.nn.sigmoid(gt))).astype(o_ref.dtype)


def gla_scan(proj, w_gate, b_gate, norm_g, *, batch, ts=512):
    T = proj.shape[0]
    S = T // batch
    nb = S // ts
    KD = GLA_HEADS * GLA_DK
    VD = GLA_HEADS * GLA_DV
    rows = lambda width, col: pl.BlockSpec((ts, width), lambda b, i: (b * nb + i, col))
    return pl.pallas_call(
        _gla_kernel,
        out_shape=jax.ShapeDtypeStruct((T, VD), BF16),
        grid=(batch, nb),
        in_specs=[
            rows(KD, 0), rows(KD, 1), rows(VD, 1), rows(VD, 2), rows(GLA_Z_COLS, (2 * KD + 2 * VD) // GLA_Z_COLS),
            pl.BlockSpec((GLA_Z_COLS, KD), lambda b, i: (0, 0)),
            pl.BlockSpec((1, KD), lambda b, i: (0, 0)),
            pl.BlockSpec((1, GLA_DV), lambda b, i: (0, 0)),
        ],
        out_specs=pl.BlockSpec((ts, VD), lambda b, i: (b * nb + i, 0)),
        scratch_shapes=[pltpu.VMEM((GLA_HEADS, GLA_DK, GLA_DV), F32)],
        compiler_params=_cparams(("parallel", "arbitrary")),
        name="gla_scan",
    )(proj, proj, proj, proj, proj, w_gate, b_gate.reshape(1, KD).astype(F32),
      norm_g.reshape(1, GLA_DV).astype(F32))


def kernel(x, rel_bias, ln_mix, ln_ffn, even_w_in, lam_q1, lam_k1, lam_q2, lam_k2, subln_g, conv_w, conv_b, conv_ln_g, conv_ln_b, even_w_out, odd_w_in, w_gate2, b_gate2, gla_norm_g, odd_w_out, peer_w_query, peer_sub_keys, peer_u, peer_v, ln_final):
    B, S, D = x.shape
    T = B * S
    xf = x.reshape(T, D)

    lam_init = 0.8 - 0.6 * math.exp(-0.3 * 0)
    lam = (jnp.exp(jnp.sum(lam_q1[0].astype(F32) * lam_k1[0].astype(F32)))
           - jnp.exp(jnp.sum(lam_q2[0].astype(F32) * lam_k2[0].astype(F32))) + lam_init)
    proj = norm_matmul(xf, ln_mix[0], even_w_in[0].astype(BF16)).reshape(B, S, -1)
    a_out = diff_attention(proj, lam, _attn_bias_tiles(rel_bias, ATTN_TILE), subln_g[0],
                           out_scale=1.0 - lam_init)
    c_out = conv_module(proj, conv_w[0], conv_b[0], conv_ln_g[0], conv_ln_b[0])
    a_width = A_HEADS * A_V_DIM
    w_out = even_w_out[0].astype(BF16)
    xf = proj_residual(xf, a_out.reshape(T, -1), c_out.reshape(T, -1), w_out[:a_width], w_out[a_width:])
    xf = peer_layer(xf, ln_ffn[0], peer_w_query[0], peer_sub_keys[0], peer_u[0], peer_v[0], ln_final,
                    final_norm=False)

    kd = GLA_HEADS * GLA_DK
    vd = GLA_HEADS * GLA_DV
    w_in = jnp.pad(odd_w_in[0], ((0, 0), (0, GLA_Z_COLS - GATE_RANK))).astype(BF16)
    w_gate = jnp.pad(w_gate2[0], ((0, GLA_Z_COLS - GATE_RANK), (0, 0))).astype(BF16)
    proj = norm_matmul(xf, ln_mix[1], w_in)
    og = gla_scan(proj, w_gate, b_gate2[0], gla_norm_g[0], batch=B)
    w_out = odd_w_out[0].astype(BF16)
    xf = proj_residual(xf, og, og, w_out[:vd // 2], w_out[vd // 2:], a_col=0, c_col=1)
    xf = peer_layer(xf, ln_ffn[1], peer_w_query[1], peer_sub_keys[1], peer_u[1], peer_v[1], ln_final,
                    final_norm=True)
    return xf.reshape(B, S, D)
```

```python
import functools
import math

import jax
import jax.numpy as jnp
import numpy as np
from jax import lax
from jax.experimental import pallas as pl
from jax.experimental.pallas import tpu as pltpu

F32 = jnp.float32
BF16 = jnp.bfloat16

EPS = 1e-6
LANES = 128
VMEM_LIMIT = 56 * 1024 * 1024

CHUNK = 64
N_BUCKETS = 32
MAX_DISTANCE = 128
A_HEADS = 4
A_QK_DIM = 64
A_V_DIM = 128
CONV_WIDTH = 31
GLA_HEADS = 4
GATE_RANK = 16
GATE_TAU = 16.0
PEER_HEADS = 8
N_KEYS = 128
PEER_TOPK = 16
PEER_HALF = 128
NEG_BIG = -1e30


def _cparams(sem):
    return pltpu.CompilerParams(dimension_semantics=sem, vmem_limit_bytes=VMEM_LIMIT)


def _rms_normed(x, g):
    ms = jnp.mean(x * x, axis=-1, keepdims=True)
    return x * lax.rsqrt(ms + EPS) * g


def _norm_matmul_kernel(x_ref, g_ref, w_ref, o_ref):
    xn = _rms_normed(x_ref[...], g_ref[...]).astype(BF16)
    o_ref[...] = jnp.dot(xn, w_ref[...], preferred_element_type=F32).astype(o_ref.dtype)


def norm_matmul(x, g, w, *, tm=512, out_dtype=BF16):
    T, D = x.shape
    N = w.shape[1]
    return pl.pallas_call(
        _norm_matmul_kernel,
        out_shape=jax.ShapeDtypeStruct((T, N), out_dtype),
        grid=(T // tm,),
        in_specs=[
            pl.BlockSpec((tm, D), lambda i: (i, 0)),
            pl.BlockSpec((1, D), lambda i: (0, 0)),
            pl.BlockSpec((D, N), lambda i: (0, 0)),
        ],
        out_specs=pl.BlockSpec((tm, N), lambda i: (i, 0)),
        compiler_params=_cparams(("parallel",)),
        name="norm_matmul",
    )(x, g.reshape(1, D).astype(F32), w)


def _proj_residual_kernel(x_ref, a_ref, c_ref, wa_ref, wc_ref, o_ref):
    acc = jnp.dot(a_ref[...], wa_ref[...], preferred_element_type=F32)
    acc = acc + jnp.dot(c_ref[...], wc_ref[...], preferred_element_type=F32)
    o_ref[...] = x_ref[...] + acc


def proj_residual(x, a, c, wa, wc, *, a_col=0, c_col=0, tm=512):
    T, D = x.shape
    Ka, Kc = wa.shape[0], wc.shape[0]
    return pl.pallas_call(
        _proj_residual_kernel,
        out_shape=jax.ShapeDtypeStruct((T, D), F32),
        grid=(T // tm,),
        in_specs=[
            pl.BlockSpec((tm, D), lambda i: (i, 0)),
            pl.BlockSpec((tm, Ka), lambda i: (i, a_col)),
            pl.BlockSpec((tm, Kc), lambda i: (i, c_col)),
            pl.BlockSpec((Ka, D), lambda i: (0, 0)),
            pl.BlockSpec((Kc, D), lambda i: (0, 0)),
        ],
        out_specs=pl.BlockSpec((tm, D), lambda i: (i, 0)),
        compiler_params=_cparams(("parallel",)),
        name="proj_residual",
    )(x, a, c, wa, wc)


def _pair_candidates():
    return [(a, b) for a in range(PEER_TOPK) for b in range(PEER_TOPK) if (a + 1) * (b + 1) <= PEER_TOPK]


def _dup_bf16_bits(v):
    hi = pltpu.bitcast(v.astype(BF16).astype(F32), jnp.uint32)
    return hi | (hi >> 16)


PAIR = 8


def _pack_bf16_pairs(lo, hi):
    bl = pltpu.bitcast(lo.astype(BF16).astype(F32), jnp.uint32)
    bh = pltpu.bitcast(hi.astype(BF16).astype(F32), jnp.uint32)
    return (bl >> 16) | bh


def _pair_rows(scr, idx, grp, half):
    n_rows = scr.shape[2]
    return jnp.concatenate([scr[idx, grp, g + PAIR * half:g + PAIR * (half + 1), :]
                            for g in range(0, n_rows, 2 * PAIR)], axis=0)


def _route_kernel(x_ref, g_ref, wqT_ref, sk_ref, rank2_ref, e2_ref, r1_ref, e1z_ref,
                  xn_scr, s_scr, rank_scr, top_scr, r1top_scr, stat_scr):
    tm = x_ref.shape[0]
    n_groups = tm // LANES
    xn_scr[...] = _rms_normed(x_ref[...], g_ref[...]).astype(BF16)

    def score_body(hh, carry):
        for h in (2 * hh, 2 * hh + 1):
            w_h = wqT_ref[pl.ds(pl.multiple_of(h * 2 * PEER_HALF, 2 * PEER_HALF), 2 * PEER_HALF), :]
            qT = lax.dot_general(w_h, xn_scr[...], (((1,), (1,)), ((), ())),
                                 preferred_element_type=F32).astype(BF16)
            for p in range(2):
                s = jnp.dot(sk_ref[2 * h + p], qT[p * PEER_HALF:(p + 1) * PEER_HALF, :],
                            preferred_element_type=F32)
                for grp in range(n_groups):
                    s_scr[2 * h + p, grp] = s[:, grp * LANES:(grp + 1) * LANES]
        return carry

    lax.fori_loop(0, PEER_HEADS // 2, score_body, 0)

    def top_body(hp, carry):
        for grp in range(n_groups):
            work = s_scr[hp, grp]
            rank = jnp.full(work.shape, float(PEER_TOPK), F32)
            for k in range(PEER_TOPK):
                m = jnp.max(work, axis=0, keepdims=True)
                eq = work == m
                rank = jnp.where(eq, float(k), rank)
                work = jnp.where(eq, -jnp.inf, work)
                top_scr[hp, k, grp:grp + 1, :] = m
            rank_scr[hp, grp] = rank
        return carry

    lax.fori_loop(0, 2 * PEER_HEADS, top_body, 0)

    cands = _pair_candidates()

    def pair_body(hq, carry):
        pair_heads(4 * hq)
        pair_heads(4 * hq + 2)
        return carry

    def pair_heads(h):
        def both(scr, idx, k):
            return jnp.concatenate([scr[idx(h), k], scr[idx(h + 1), k]], axis=0)

        v1 = [both(top_scr, lambda x: 2 * x, a) for a in range(PEER_TOPK)]
        v2 = [both(top_scr, lambda x: 2 * x + 1, b) for b in range(PEER_TOPK)]
        sums = [v1[a] + v2[b] for (a, b) in cands]
        work = list(sums)
        tau = None
        for k in range(PEER_TOPK):
            tau = functools.reduce(jnp.maximum, work)
            if k + 1 < PEER_TOPK:
                work = [jnp.where(w == tau, -jnp.inf, w) for w in work]
        cmax = v1[0] + v2[0]
        z = jnp.zeros_like(tau)
        r1 = [jnp.zeros_like(tau) for _ in range(PEER_TOPK)]
        for (a, b), c in zip(cands, sums):
            sel = c >= tau
            z = z + jnp.where(sel, jnp.exp(c - cmax), 0.0)
            r1[a] = r1[a] + jnp.where(sel, 1.0, 0.0)
        inv_z = 1.0 / z
        for d in range(2):
            part = slice(d * n_groups, (d + 1) * n_groups)
            for a in range(PEER_TOPK):
                r1top_scr[h + d, a] = r1[a][part]
            stat_scr[h + d, 0] = v1[0][part]
            stat_scr[h + d, 1] = v2[0][part]
            stat_scr[h + d, 2] = inv_z[part]

    lax.fori_loop(0, PEER_HEADS // 4, pair_body, 0)

    def table_body(h, carry):
        for grp in range(n_groups):
            lanes = slice(grp * LANES, (grp + 1) * LANES)
            max1 = stat_scr[h, 0, grp:grp + 1, :]
            max2 = stat_scr[h, 1, grp:grp + 1, :]
            inv_z = stat_scr[h, 2, grp:grp + 1, :]
            rank1 = rank_scr[2 * h, grp]
            r1_key = jnp.zeros(rank1.shape, F32)
            for a in range(PEER_TOPK):
                r1_key = jnp.where(rank1 == float(a), r1top_scr[h, a, grp:grp + 1, :], r1_key)
            r1_ref[h, :, lanes] = _dup_bf16_bits(r1_key)
            e1z = jnp.exp(s_scr[2 * h, grp] - max1) * inv_z
            e1z_ref[h, :, lanes] = _dup_bf16_bits(e1z)
            rk_lo = _pair_rows(rank_scr, 2 * h + 1, grp, 0)
            rk_hi = _pair_rows(rank_scr, 2 * h + 1, grp, 1)
            rank2_ref[h, :, lanes] = _pack_bf16_pairs(rk_lo, rk_hi)
            e2_lo = jnp.exp(jnp.minimum(_pair_rows(s_scr, 2 * h + 1, grp, 0) - max2, 0.0))
            e2_hi = jnp.exp(jnp.minimum(_pair_rows(s_scr, 2 * h + 1, grp, 1) - max2, 0.0))
            e2_ref[h, :, lanes] = _pack_bf16_pairs(e2_lo, e2_hi)
        return carry

    lax.fori_loop(0, PEER_HEADS, table_body, 0)


def peer_route(x, g, wqT, sk, *, tm=512):
    T, D = x.shape
    H = PEER_HEADS
    n_groups = tm // LANES
    tab = lambda rows: jax.ShapeDtypeStruct((H, rows, T), jnp.uint32)
    tab_spec = lambda rows: pl.BlockSpec((H, rows, tm), lambda i: (0, 0, i))
    return pl.pallas_call(
        _route_kernel,
        out_shape=(tab(N_KEYS // 2), tab(N_KEYS // 2), tab(N_KEYS), tab(N_KEYS)),
        grid=(T // tm,),
        in_specs=[
            pl.BlockSpec((tm, D), lambda i: (i, 0)),
            pl.BlockSpec((1, D), lambda i: (0, 0)),
            pl.BlockSpec((H * 2 * PEER_HALF, D), lambda i: (0, 0)),
            pl.BlockSpec((2 * H, N_KEYS, PEER_HALF), lambda i: (0, 0, 0)),
        ],
        out_specs=(tab_spec(N_KEYS // 2), tab_spec(N_KEYS // 2), tab_spec(N_KEYS), tab_spec(N_KEYS)),
        scratch_shapes=[
            pltpu.VMEM((tm, D), BF16),
            pltpu.VMEM((2 * H, n_groups, N_KEYS, LANES), F32),
            pltpu.VMEM((2 * H, n_groups, N_KEYS, LANES), F32),
            pltpu.VMEM((2 * H, PEER_TOPK, n_groups, LANES), F32),
            pltpu.VMEM((H, PEER_TOPK, n_groups, LANES), F32),
            pltpu.VMEM((H, 3, n_groups, LANES), F32),
        ],
        compiler_params=_cparams(("parallel",)),
        name="peer_route",
    )(x, g.reshape(1, D).astype(F32), wqT, sk)


GELU_C0 = math.sqrt(2.0 / math.pi)
GELU_C1 = 0.044715


def _gelu_tanh(a):
    inner = GELU_C0 * (a + GELU_C1 * (a * a * a))
    return 0.5 * a * (1.0 + jnp.tanh(inner))


PEER_SUB = 512


def _peer_main_kernel(x_ref, g_ref, rank2_ref, e2_ref, r1_ref, e1z_ref, u_ref, vt_ref, gf_ref, o_ref,
                      xn_scr, acc_scr, act_scr, coef_scr, *, final_norm):
    j = pl.program_id(1)
    tm = x_ref.shape[0]
    n_i1 = r1_ref.shape[1]
    rows = 16
    words = rows * jnp.dtype(BF16).itemsize // 4

    @pl.when(j == 0)
    def _():
        xn_scr[...] = _rms_normed(x_ref[...], g_ref[...]).astype(BF16)
        acc_scr[...] = jnp.zeros_like(acc_scr)

    eb = 2 * u_ref.shape[0]
    d_model = 2 * vt_ref.shape[0]
    tn = 2 * LANES

    def stage_a(n, q):
        r = slice(q * PEER_SUB, (q + 1) * PEER_SUB)
        rw = slice(q * PEER_SUB // 2, (q + 1) * PEER_SUB // 2)
        c = slice(n * tn, (n + 1) * tn)
        act_scr[n, r, :] = lax.dot_general(pltpu.bitcast(u_ref[rw, :], BF16), xn_scr[c, :],
                                           (((1,), (1,)), ((), ())), preferred_element_type=F32)

    def stage_b(n, q):
        for sub in range(tn // LANES):
            lanes = slice(n * tn + sub * LANES, n * tn + (sub + 1) * LANES)
            sl = slice(sub * LANES, (sub + 1) * LANES)
            n_blk = N_KEYS // rows
            for i1 in range(q * PEER_SUB // N_KEYS, (q + 1) * PEER_SUB // N_KEYS, 2):
                gate = [[None] * n_blk, [None] * n_blk]
                for h in range(PEER_HEADS):
                    r1b, e1b = [], []
                    for d in range(2):
                        r1row = jnp.broadcast_to(r1_ref[h, i1 + d:i1 + d + 1, lanes], (words, LANES))
                        e1row = jnp.broadcast_to(e1z_ref[h, i1 + d:i1 + d + 1, lanes], (words, LANES))
                        r1b.append(pltpu.bitcast(r1row, BF16))
                        e1b.append(pltpu.bitcast(e1row, BF16))
                    for blk in range(n_blk):
                        i2w = slice(blk * words, (blk + 1) * words)
                        rk = pltpu.bitcast(rank2_ref[h, i2w, lanes], BF16)
                        ev = pltpu.bitcast(e2_ref[h, i2w, lanes], BF16)
                        for d in range(2):
                            term = jnp.where(rk < r1b[d], ev, jnp.zeros((), BF16)) * e1b[d]
                            gate[d][blk] = term if gate[d][blk] is None else gate[d][blk] + term
                for d in range(2):
                    for blk in range(n_blk):
                        r0 = (i1 + d) * N_KEYS + blk * rows
                        a = act_scr[n, r0:r0 + rows, sl].astype(BF16)
                        coef_scr[n, r0:r0 + rows, sl] = _gelu_tanh(a) * gate[d][blk]

    def stage_c(n):
        for half in range(2):
            r = slice(half * (d_model // 2), (half + 1) * (d_model // 2))
            rw = slice(half * (d_model // 4), (half + 1) * (d_model // 4))
            res = jnp.dot(pltpu.bitcast(vt_ref[rw, :], BF16), coef_scr[n], preferred_element_type=F32)
            for sub in range(tn // LANES):
                acc_scr[n * (tn // LANES) + sub, r, :] += res[:, sub * LANES:(sub + 1) * LANES]

    for n in range(tm // tn):
        for q in range(eb // PEER_SUB):
            stage_a(n, q)
            stage_b(n, q)
        stage_c(n)

    @pl.when(j == pl.num_programs(1) - 1)
    def _():
        lo_rows = pl.ds(0, d_model // 2, stride=2)
        hi_rows = pl.ds(1, d_model // 2, stride=2)
        peer_out = jnp.concatenate(
            [jnp.concatenate([acc_scr[grp, lo_rows, :], acc_scr[grp, hi_rows, :]], axis=0).T
             for grp in range(tm // LANES)], axis=0)
        y = x_ref[...] + peer_out
        if final_norm:
            y = _rms_normed(y, gf_ref[...])
        o_ref[...] = y


def _bf16_words(lo, hi):
    bits = lambda a: lax.bitcast_convert_type(a.astype(BF16), jnp.uint16).astype(jnp.uint32)
    return bits(lo) | (bits(hi) << 16)


def _pack_expert_rows(u):
    e, d = u.shape
    u4 = u.reshape(e // (2 * PAIR), 2, PAIR, d)
    return _bf16_words(u4[:, 0], u4[:, 1]).reshape(e // 2, d)


def _pack_vt(v):
    e, d = v.shape
    v_perm = v.reshape(e // (2 * PAIR), 2, PAIR, d).transpose(0, 2, 1, 3).reshape(e, d)
    vt = v_perm.T
    return _bf16_words(vt[:d // 2], vt[d // 2:])


def peer_main(x, g, tables, u, vt, g_final, *, final_norm, tm=512, eb=2048):
    T, D = x.shape
    E = 2 * u.shape[0]
    H = PEER_HEADS
    rank2, e2, r1, e1z = tables
    n_i1 = eb // N_KEYS
    full_tab = pl.BlockSpec((H, N_KEYS // 2, tm), lambda i, j: (0, 0, i))
    i1_tab = pl.BlockSpec((H, n_i1, tm), lambda i, j: (0, j, i))
    return pl.pallas_call(
        functools.partial(_peer_main_kernel, final_norm=final_norm),
        out_shape=jax.ShapeDtypeStruct((T, D), F32),
        grid=(T // tm, E // eb),
        in_specs=[
            pl.BlockSpec((tm, D), lambda i, j: (i, 0)),
            pl.BlockSpec((1, D), lambda i, j: (0, 0)),
            full_tab, full_tab, i1_tab, i1_tab,
            pl.BlockSpec((eb // 2, D), lambda i, j: (j, 0)),
            pl.BlockSpec((D // 2, eb), lambda i, j: (0, j)),
            pl.BlockSpec((1, D), lambda i, j: (0, 0)),
        ],
        out_specs=pl.BlockSpec((tm, D), lambda i, j: (i, 0)),
        scratch_shapes=[
            pltpu.VMEM((tm, D), BF16),
            pltpu.VMEM((tm // LANES, D, LANES), F32),
            pltpu.VMEM((tm // (2 * LANES), eb, 2 * LANES), F32),
            pltpu.VMEM((tm // (2 * LANES), eb, 2 * LANES), BF16),
        ],
        compiler_params=_cparams(("parallel", "arbitrary")),
        name="peer_main",
    )(x, g.reshape(1, D).astype(F32), rank2, e2, r1, e1z, u, vt, g_final.reshape(1, D).astype(F32))


def peer_layer(x, g, w_query, sub_keys, expert_u, expert_v, g_final, *, final_norm):
    wqT = w_query.T.astype(BF16)
    sk = sub_keys.reshape(2 * PEER_HEADS, N_KEYS, PEER_HALF).astype(BF16)
    tables = peer_route(x, g, wqT, sk)
    return peer_main(x, g, tables, _pack_expert_rows(expert_u), _pack_vt(expert_v), g_final,
                     final_norm=final_norm)


ATTN_TILE = 256
ATTN_GROUP = 1
ATTN_UNROLL = 4
ATTN_ONES = 16


def _attn_kernel(lam_ref, q_ref, k_ref, v_ref, bias_ref, gsub_ref, o_ref, vt_scr, m_scr, acc_scr,
                 s0_scr, s1_scr, p0_scr, p1_scr, *, out_scale):
    i = pl.program_id(2)
    tq = q_ref.shape[1]
    tk = tq
    heads = range(q_ref.shape[2] // A_V_DIM)
    cols = lambda g: slice(g * A_V_DIM, (g + 1) * A_V_DIM)

    @pl.when(i == 0)
    def _():
        for g in heads:
            vt_scr[g, :A_V_DIM, :] = v_ref[0, :, cols(g)].T
            vt_scr[g, A_V_DIM:, :] = jnp.ones((ATTN_ONES, vt_scr.shape[2]), BF16)

    scale = jnp.asarray(A_QK_DIM ** -0.5, BF16)
    zero = jnp.zeros((), BF16)
    qs = []
    for g in heads:
        q = q_ref[0, :, cols(g)]
        lane = lax.broadcasted_iota(jnp.int32, q.shape, 1)
        qs.append(jnp.concatenate([jnp.where(lane < A_QK_DIM, q, zero) * scale,
                                   jnp.where(lane >= A_QK_DIM, q, zero) * scale], axis=0))

    m_scr[...] = jnp.full(m_scr.shape, NEG_BIG, F32)
    acc_scr[...] = jnp.zeros(acc_scr.shape, F32)

    def scores(j, g):
        k = k_ref[0, pl.ds(pl.multiple_of(j * tk, tk), tk), cols(g)]
        return lax.dot_general(k, qs[g], (((1,), (1,)), ((), ())), preferred_element_type=F32)

    def pv(j, p_ref, g):
        vt = vt_scr[g, :, pl.ds(pl.multiple_of(j * tk, tk), tk)]
        return jnp.dot(vt, p_ref[g], preferred_element_type=F32)

    def consume(j, bias_idx, cur, nxt, prefetch):
        s_cur, p_cur = cur
        s_nxt, p_prev = nxt
        for g in heads:
            if prefetch:
                s_nxt[g] = scores(j + 1, g)
            pv_prev = pv(jnp.maximum(j - 1, 0), p_prev, g)
            s = s_cur[g]
            if bias_idx is not None:
                bias = bias_ref[g, bias_idx]
                s = s + jnp.concatenate([bias, bias], axis=1)
            m_prev = m_scr[g]
            m_new = jnp.maximum(m_prev, jnp.max(s, axis=0, keepdims=True))
            alpha = jnp.exp(m_prev - m_new)
            p_cur[g] = jnp.exp((s - m_new).astype(BF16))
            acc_scr[g] = alpha * (acc_scr[g] + pv_prev)
            m_scr[g] = m_new

    def by_parity(j, fn):
        @pl.when(j % 2 == 0)
        def _():
            fn((s0_scr, p0_scr), (s1_scr, p1_scr))

        @pl.when(j % 2 == 1)
        def _():
            fn((s1_scr, p1_scr), (s0_scr, p0_scr))

    for g in heads:
        s0_scr[g] = scores(0, g)
    p1_scr[...] = jnp.zeros(p1_scr.shape, BF16)

    n_far = jnp.maximum(i - 1, 0)
    even_bufs, odd_bufs = (s0_scr, p0_scr), (s1_scr, p1_scr)

    def far_tiles(first, count):
        for t in range(count):
            cur, nxt = (even_bufs, odd_bufs) if t % 2 == 0 else (odd_bufs, even_bufs)
            consume(first + t, None, cur, nxt, True)

    def far_body(jj, carry):
        far_tiles(ATTN_UNROLL * jj, ATTN_UNROLL)
        return carry

    lax.fori_loop(0, n_far // ATTN_UNROLL, far_body, 0)
    rest = n_far % ATTN_UNROLL
    done = n_far - rest

    @pl.when(rest >= 2)
    def _():
        far_tiles(done, 2)

    @pl.when(rest % 2 == 1)
    def _():
        far_tiles(n_far - 1, 1)

    @pl.when(i >= 1)
    def _():
        by_parity(i - 1, lambda cur, nxt: consume(i - 1, 0, cur, nxt, True))

    def last(cur, nxt):
        consume(i, 1, cur, nxt, False)
        for g in heads:
            acc_scr[g] += pv(i, cur[1], g)

    by_parity(i, last)

    for g in heads:
        ot = acc_scr[g, :A_V_DIM, :] / acc_scr[g, A_V_DIM:A_V_DIM + 1, :]
        ot = ot[:, :tq] - lam_ref[0] * ot[:, tq:]
        ot = ot * lax.rsqrt(jnp.mean(ot * ot, axis=0, keepdims=True) + EPS)
        o_ref[0, :, cols(g)] = (ot.T * (gsub_ref[...] * out_scale)).astype(o_ref.dtype)


def _t5_bucket(rel):
    nb = N_BUCKETS // 2
    max_exact = nb // 2
    ret = jnp.where(rel > 0, nb, 0)
    n = jnp.abs(rel)
    nf = jnp.maximum(n, 1).astype(F32)
    large = max_exact + (jnp.log(nf / max_exact) / math.log(MAX_DISTANCE / max_exact)
                         * (nb - max_exact)).astype(jnp.int32)
    large = jnp.minimum(large, nb - 1)
    return ret + jnp.where(n < max_exact, n, large)


def _attn_bias_tiles(rel_bias, tile):
    r = jnp.arange(tile)[None, :]
    c = jnp.arange(tile)[:, None]
    rb = rel_bias.astype(F32)

    def lookup(bucket):
        return jnp.einsum("...b,bh->...h", jax.nn.one_hot(bucket, N_BUCKETS, dtype=F32), rb,
                          precision=lax.Precision.HIGHEST)

    far = lookup(_t5_bucket(jnp.asarray(-(2 * tile))))
    prev = lookup(_t5_bucket(c - r - tile)) - far
    diag = lookup(_t5_bucket(c - r)) - far
    diag = jnp.where(((c // CHUNK) <= (r // CHUNK))[..., None], diag, NEG_BIG)
    return jnp.transpose(jnp.stack([prev, diag], axis=0), (3, 0, 1, 2))


def diff_attention(proj, lam, bias_tiles, subln_g, *, out_scale):
    B, S, _ = proj.shape
    H = A_HEADS
    G = ATTN_GROUP
    tq = ATTN_TILE
    width = G * A_V_DIM
    return pl.pallas_call(
        functools.partial(_attn_kernel, out_scale=out_scale),
        out_shape=jax.ShapeDtypeStruct((B, S, H * A_V_DIM), BF16),
        grid=(B, H // G, S // tq),
        in_specs=[
            pl.BlockSpec(memory_space=pltpu.SMEM),
            pl.BlockSpec((1, tq, width), lambda b, h, i: (b, i, h)),
            pl.BlockSpec((1, S, width), lambda b, h, i: (b, 0, H // G + h)),
            pl.BlockSpec((1, S, width), lambda b, h, i: (b, 0, 2 * (H // G) + h)),
            pl.BlockSpec((G, 2, tq, tq), lambda b, h, i: (h, 0, 0, 0)),
            pl.BlockSpec((1, A_V_DIM), lambda b, h, i: (0, 0)),
        ],
        out_specs=pl.BlockSpec((1, tq, width), lambda b, h, i: (b, i, h)),
        scratch_shapes=[
            pltpu.VMEM((G, A_V_DIM + ATTN_ONES, S), BF16),
            pltpu.VMEM((G, 1, 2 * tq), F32),
            pltpu.VMEM((G, A_V_DIM + ATTN_ONES, 2 * tq), F32),
            pltpu.VMEM((G, tq, 2 * tq), F32),
            pltpu.VMEM((G, tq, 2 * tq), F32),
            pltpu.VMEM((G, tq, 2 * tq), BF16),
            pltpu.VMEM((G, tq, 2 * tq), BF16),
        ],
        compiler_params=_cparams(("parallel", "parallel", "arbitrary")),
        name="diff_attention",
    )(lam.reshape(1).astype(F32), proj, proj, proj, bias_tiles, subln_g.reshape(1, A_V_DIM).astype(F32))


CONV_HALO = 32
CONV_ROWS = 64


def _conv_kernel(val_ref, gate_ref, hval_ref, hgate_ref, w_ref, b_ref, g_ref, beta_ref, o_ref, buf_scr,
                 win_scr):
    i = pl.program_id(1)
    ts = val_ref.shape[1]

    def glu(v, gt):
        return v.astype(F32) * jax.nn.sigmoid(gt.astype(F32))

    halo = glu(hval_ref[0], hgate_ref[0])
    buf_scr[0:CONV_HALO, :] = jnp.where(i > 0, halo, 0.0)
    buf_scr[CONV_HALO:, :] = glu(val_ref[0], gate_ref[0])

    first = CONV_HALO - (CONV_WIDTH - 1)
    sub = 8
    n_ch = buf_scr.shape[1]
    for r in range(ts // CONV_ROWS):
        halves = []
        for c0 in range(0, n_ch, n_ch // 2):
            ch = slice(c0, c0 + n_ch // 2)
            acc = None
            for b in range(sub):
                taps = range(b, CONV_WIDTH, sub)
                lo = r * CONV_ROWS + first + b
                rows_b = CONV_ROWS + sub * (len(taps) - 1)
                win_scr[b, 0:rows_b, :] = buf_scr[lo:lo + rows_b, ch]
                for a, w in enumerate(taps):
                    term = win_scr[b, sub * a:sub * a + CONV_ROWS, :] * w_ref[w:w + 1, ch]
                    acc = term if acc is None else acc + term
            halves.append(acc)
        y = jnp.concatenate(halves, axis=1) + b_ref[...]
        mu = jnp.mean(y, axis=-1, keepdims=True)
        d = y - mu
        var = jnp.mean(d * d, axis=-1, keepdims=True)
        y = d * lax.rsqrt(var + EPS) * g_ref[...] + beta_ref[...]
        y = y * jax.nn.sigmoid(y)
        o_ref[0, r * CONV_ROWS:(r + 1) * CONV_ROWS, :] = y.astype(o_ref.dtype)


def conv_module(proj, conv_w, conv_b, ln_g, ln_b, *, ts=512):
    B, S, _ = proj.shape
    C = conv_w.shape[1]
    hb = ts // CONV_HALO
    row = lambda a: a.reshape(1, C).astype(F32)
    cur = lambda col: pl.BlockSpec((1, ts, C), lambda b, i: (b, i, col))
    halo = lambda col: pl.BlockSpec((1, CONV_HALO, C), lambda b, i: (b, jnp.maximum(i * hb - 1, 0), col))
    vec = pl.BlockSpec((1, C), lambda b, i: (0, 0))
    return pl.pallas_call(
        _conv_kernel,
        out_shape=jax.ShapeDtypeStruct((B, S, C), BF16),
        grid=(B, S // ts),
        in_specs=[cur(3), cur(4), halo(3), halo(4),
                  pl.BlockSpec((CONV_WIDTH + 1, C), lambda b, i: (0, 0)), vec, vec, vec],
        out_specs=pl.BlockSpec((1, ts, C), lambda b, i: (b, i, 0)),
        scratch_shapes=[pltpu.VMEM((ts + CONV_HALO, C), F32),
                        pltpu.VMEM((8, CONV_ROWS + CONV_HALO, C // 2), F32)],
        compiler_params=_cparams(("parallel", "parallel")),
        name="conv_module",
    )(proj, proj, proj, proj, jnp.pad(conv_w.astype(F32), ((0, 1), (0, 0))), row(conv_b), row(ln_g), row(ln_b))


GLA_DK = 128
GLA_DV = 256
GLA_Z_COLS = 128


def _split_bf16(x):
    hi = x.astype(BF16)
    lo = (x - hi.astype(F32)).astype(BF16)
    return hi, lo


def _gla_kernel(q_ref, k_ref, v_ref, g_ref, z_ref, wg_ref, bg_ref, gn_ref, o_ref, state_scr):
    ts = q_ref.shape[0]
    n_chunks = ts // CHUNK

    @pl.when(pl.program_id(1) == 0)
    def _():
        state_scr[...] = jnp.zeros(state_scr.shape, F32)

    zz = jnp.dot(z_ref[...], wg_ref[...], preferred_element_type=F32) + bg_ref[...]
    log_a = (jnp.minimum(zz, 0.0) - jnp.log1p(jnp.exp(-jnp.abs(zz)))) / GATE_TAU

    row = lax.broadcasted_iota(jnp.int32, (CHUNK, CHUNK), 0)
    col = lax.broadcasted_iota(jnp.int32, (CHUNK, CHUNK), 1)
    tri = jnp.where(col <= row, 1.0, 0.0).astype(BF16)
    ones = jnp.ones((CHUNK, LANES), BF16)
    tn = (((0,), (0,)), ((), ()))
    q_scale = GLA_DK ** -0.5

    for c in range(n_chunks):
        rows = slice(c * CHUNK, (c + 1) * CHUNK)
        la_hi, la_lo = _split_bf16(log_a[rows, :])
        cum = (jnp.dot(tri, la_hi, preferred_element_type=F32)
               + jnp.dot(tri, la_lo, preferred_element_type=F32))
        k_dec = k_ref[rows, :].astype(F32) * jnp.exp(cum[CHUNK - 1:CHUNK, :] - cum)
        for h in range(GLA_HEADS):
            kc = slice(h * GLA_DK, (h + 1) * GLA_DK)
            vc = slice(h * GLA_DV, (h + 1) * GLA_DV)
            tot = (lax.dot_general(la_hi[:, kc], ones, tn, preferred_element_type=F32)
                   + lax.dot_general(la_lo[:, kc], ones, tn, preferred_element_type=F32))
            decay = jnp.exp(tot)
            decay = jnp.concatenate([decay] * (GLA_DV // LANES), axis=1)
            kv = lax.dot_general(k_dec[:, kc].astype(BF16), v_ref[rows, vc], tn,
                                 preferred_element_type=F32)
            state = state_scr[h] * decay + kv
            state_scr[h] = state
            o = jnp.dot(q_ref[rows, kc], state.astype(BF16), preferred_element_type=F32) * q_scale
            o = o * lax.rsqrt(jnp.mean(o * o, axis=-1, keepdims=True) + EPS) * gn_ref[...]
            gt = g_ref[rows, vc].astype(F32)
            o_ref[rows, vc] = (o * (gt * jax.nn.sigmoid(gt))).astype(o_ref.dtype)


def gla_scan(proj, w_gate, b_gate, norm_g, *, batch, ts=512):
    T = proj.shape[0]
    S = T // batch
    nb = S // ts
    KD = GLA_HEADS * GLA_DK
    VD = GLA_HEADS * GLA_DV
    rows = lambda width, col: pl.BlockSpec((ts, width), lambda b, i: (b * nb + i, col))
    return pl.pallas_call(
        _gla_kernel,
        out_shape=jax.ShapeDtypeStruct((T, VD), BF16),
        grid=(batch, nb),
        in_specs=[
            rows(KD, 0), rows(KD, 1), rows(VD, 1), rows(VD, 2), rows(GLA_Z_COLS, (2 * KD + 2 * VD) // GLA_Z_COLS),
            pl.BlockSpec((GLA_Z_COLS, KD), lambda b, i: (0, 0)),
            pl.BlockSpec((1, KD), lambda b, i: (0, 0)),
            pl.BlockSpec((1, GLA_DV), lambda b, i: (0, 0)),
        ],
        out_specs=pl.BlockSpec((ts, VD), lambda b, i: (b * nb + i, 0)),
        scratch_shapes=[pltpu.VMEM((GLA_HEADS, GLA_DK, GLA_DV), F32)],
        compiler_params=_cparams(("parallel", "arbitrary")),
        name="gla_scan",
    )(proj, proj, proj, proj, proj, w_gate, b_gate.reshape(1, KD).astype(F32),
      norm_g.reshape(1, GLA_DV).astype(F32))


def kernel(x, rel_bias, ln_mix, ln_ffn, even_w_in, lam_q1, lam_k1, lam_q2, lam_k2, subln_g, conv_w, conv_b, conv_ln_g, conv_ln_b, even_w_out, odd_w_in, w_gate2, b_gate2, gla_norm_g, odd_w_out, peer_w_query, peer_sub_keys, peer_u, peer_v, ln_final):
    B, S, D = x.shape
    T = B * S
    xf = x.reshape(T, D)

    lam_init = 0.8 - 0.6 * math.exp(-0.3 * 0)
    lam = (jnp.exp(jnp.sum(lam_q1[0].astype(F32) * lam_k1[0].astype(F32)))
           - jnp.exp(jnp.sum(lam_q2[0].astype(F32) * lam_k2[0].astype(F32))) + lam_init)
    proj = norm_matmul(xf, ln_mix[0], even_w_in[0].astype(BF16)).reshape(B, S, -1)
    a_out = diff_attention(proj, lam, _attn_bias_tiles(rel_bias, ATTN_TILE), subln_g[0],
                           out_scale=1.0 - lam_init)
    c_out = conv_module(proj, conv_w[0], conv_b[0], conv_ln_g[0], conv_ln_b[0])
    a_width = A_HEADS * A_V_DIM
    w_out = even_w_out[0].astype(BF16)
    xf = proj_residual(xf, a_out.reshape(T, -1), c_out.reshape(T, -1), w_out[:a_width], w_out[a_width:])
    xf = peer_layer(xf, ln_ffn[0], peer_w_query[0], peer_sub_keys[0], peer_u[0], peer_v[0], ln_final,
                    final_norm=False)

    kd = GLA_HEADS * GLA_DK
    vd = GLA_HEADS * GLA_DV
    w_in = jnp.pad(odd_w_in[0], ((0, 0), (0, GLA_Z_COLS - GATE_RANK))).astype(BF16)
    w_gate = jnp.pad(w_gate2[0], ((0, GLA_Z_COLS - GATE_RANK), (0, 0))).astype(BF16)
    proj = norm_matmul(xf, ln_mix[1], w_in)
    og = gla_scan(proj, w_gate, b_gate2[0], gla_norm_g[0], batch=B)
    w_out = odd_w_out[0].astype(BF16)
    xf = proj_residual(xf, og, og, w_out[:vd // 2], w_out[vd // 2:], a_col=0, c_col=1)
    xf = peer_layer(xf, ln_ffn[1], peer_w_query[1], peer_sub_keys[1], peer_u[1], peer_v[1], ln_final,
                    final_norm=True)
    return xf.reshape(B, S, D)
```

```python
import functools
import math

import jax
import jax.numpy as jnp
import numpy as np
from jax import lax
from jax.experimental import pallas as pl
from jax.experimental.pallas import tpu as pltpu

F32 = jnp.float32
BF16 = jnp.bfloat16

EPS = 1e-6
LANES = 128
VMEM_LIMIT = 56 * 1024 * 1024

CHUNK = 64
N_BUCKETS = 32
MAX_DISTANCE = 128
A_HEADS = 4
A_QK_DIM = 64
A_V_DIM = 128
CONV_WIDTH = 31
GLA_HEADS = 4
GATE_RANK = 16
GATE_TAU = 16.0
PEER_HEADS = 8
N_KEYS = 128
PEER_TOPK = 16
PEER_HALF = 128
NEG_BIG = -1e30


def _cparams(sem):
    return pltpu.CompilerParams(dimension_semantics=sem, vmem_limit_bytes=VMEM_LIMIT)


def _rms_normed(x, g):
    ms = jnp.mean(x * x, axis=-1, keepdims=True)
    return x * lax.rsqrt(ms + EPS) * g


def _norm_matmul_kernel(x_ref, g_ref, w_ref, o_ref):
    xn = _rms_normed(x_ref[...], g_ref[...]).astype(BF16)
    o_ref[...] = jnp.dot(xn, w_ref[...], preferred_element_type=F32).astype(o_ref.dtype)


def norm_matmul(x, g, w, *, tm=512, out_dtype=BF16):
    T, D = x.shape
    N = w.shape[1]
    return pl.pallas_call(
        _norm_matmul_kernel,
        out_shape=jax.ShapeDtypeStruct((T, N), out_dtype),
        grid=(T // tm,),
        in_specs=[
            pl.BlockSpec((tm, D), lambda i: (i, 0)),
            pl.BlockSpec((1, D), lambda i: (0, 0)),
            pl.BlockSpec((D, N), lambda i: (0, 0)),
        ],
        out_specs=pl.BlockSpec((tm, N), lambda i: (i, 0)),
        compiler_params=_cparams(("parallel",)),
        name="norm_matmul",
    )(x, g.reshape(1, D).astype(F32), w)


def _proj_residual_kernel(x_ref, a_ref, c_ref, wa_ref, wc_ref, o_ref):
    acc = jnp.dot(a_ref[...], wa_ref[...], preferred_element_type=F32)
    acc = acc + jnp.dot(c_ref[...], wc_ref[...], preferred_element_type=F32)
    o_ref[...] = x_ref[...] + acc


def proj_residual(x, a, c, wa, wc, *, a_col=0, c_col=0, tm=512):
    T, D = x.shape
    Ka, Kc = wa.shape[0], wc.shape[0]
    return pl.pallas_call(
        _proj_residual_kernel,
        out_shape=jax.ShapeDtypeStruct((T, D), F32),
        grid=(T // tm,),
        in_specs=[
            pl.BlockSpec((tm, D), lambda i: (i, 0)),
            pl.BlockSpec((tm, Ka), lambda i: (i, a_col)),
            pl.BlockSpec((tm, Kc), lambda i: (i, c_col)),
            pl.BlockSpec((Ka, D), lambda i: (0, 0)),
            pl.BlockSpec((Kc, D), lambda i: (0, 0)),
        ],
        out_specs=pl.BlockSpec((tm, D), lambda i: (i, 0)),
        compiler_params=_cparams(("parallel",)),
        name="proj_residual",
    )(x, a, c, wa, wc)


def _pair_candidates():
    return [(a, b) for a in range(PEER_TOPK) for b in range(PEER_TOPK) if (a + 1) * (b + 1) <= PEER_TOPK]


def _dup_bf16_bits(v):
    hi = pltpu.bitcast(v.astype(BF16).astype(F32), jnp.uint32)
    return hi | (hi >> 16)


PAIR = 8


def _pack_bf16_pairs(lo, hi):
    bl = pltpu.bitcast(lo.astype(BF16).astype(F32), jnp.uint32)
    bh = pltpu.bitcast(hi.astype(BF16).astype(F32), jnp.uint32)
    return (bl >> 16) | bh


def _pair_rows(scr, idx, grp, half):
    n_rows = scr.shape[2]
    return jnp.concatenate([scr[idx, grp, g + PAIR * half:g + PAIR * (half + 1), :]
                            for g in range(0, n_rows, 2 * PAIR)], axis=0)


def _route_kernel(x_ref, g_ref, wqT_ref, sk_ref, rank2_ref, e2_ref, r1_ref, e1z_ref,
                  xn_scr, s_scr, rank_scr, top_scr, r1top_scr, stat_scr):
    tm = x_ref.shape[0]
    n_groups = tm // LANES
    xn_scr[...] = _rms_normed(x_ref[...], g_ref[...]).astype(BF16)

    def score_body(hh, carry):
        for h in (2 * hh, 2 * hh + 1):
            w_h = wqT_ref[pl.ds(pl.multiple_of(h * 2 * PEER_HALF, 2 * PEER_HALF), 2 * PEER_HALF), :]
            qT = lax.dot_general(w_h, xn_scr[...], (((1,), (1,)), ((), ())),
                                 preferred_element_type=F32).astype(BF16)
            for p in range(2):
                s = jnp.dot(sk_ref[2 * h + p], qT[p * PEER_HALF:(p + 1) * PEER_HALF, :],
                            preferred_element_type=F32)
                for grp in range(n_groups):
                    s_scr[2 * h + p, grp] = s[:, grp * LANES:(grp + 1) * LANES]
        return carry

    lax.fori_loop(0, PEER_HEADS // 2, score_body, 0)

    def top_body(hp, carry):
        for grp in range(n_groups):
            work = s_scr[hp, grp]
            rank = jnp.full(work.shape, float(PEER_TOPK), F32)
            for k in range(PEER_TOPK):
                m = jnp.max(work, axis=0, keepdims=True)
                eq = work == m
                rank = jnp.where(eq, float(k), rank)
                work = jnp.where(eq, -jnp.inf, work)
                top_scr[hp, k, grp:grp + 1, :] = m
            rank_scr[hp, grp] = rank
        return carry

    lax.fori_loop(0, 2 * PEER_HEADS, top_body, 0)

    cands = _pair_candidates()

    def pair_body(hq, carry):
        pair_heads(4 * hq)
        pair_heads(4 * hq + 2)
        return carry

    def pair_heads(h):
        def both(scr, idx, k):
            return jnp.concatenate([scr[idx(h), k], scr[idx(h + 1), k]], axis=0)

        v1 = [both(top_scr, lambda x: 2 * x, a) for a in range(PEER_TOPK)]
        v2 = [both(top_scr, lambda x: 2 * x + 1, b) for b in range(PEER_TOPK)]
        sums = [v1[a] + v2[b] for (a, b) in cands]
        work = list(sums)
        tau = None
        for k in range(PEER_TOPK):
            tau = functools.reduce(jnp.maximum, work)
            if k + 1 < PEER_TOPK:
                work = [jnp.where(w == tau, -jnp.inf, w) for w in work]
        cmax = v1[0] + v2[0]
        z = jnp.zeros_like(tau)
        r1 = [jnp.zeros_like(tau) for _ in range(PEER_TOPK)]
        for (a, b), c in zip(cands, sums):
            sel = c >= tau
            z = z + jnp.where(sel, jnp.exp(c - cmax), 0.0)
            r1[a] = r1[a] + jnp.where(sel, 1.0, 0.0)
        inv_z = 1.0 / z
        for d in range(2):
            part = slice(d * n_groups, (d + 1) * n_groups)
            for a in range(PEER_TOPK):
                r1top_scr[h + d, a] = r1[a][part]
            stat_scr[h + d, 0] = v1[0][part]
            stat_scr[h + d, 1] = v2[0][part]
            stat_scr[h + d, 2] = inv_z[part]

    lax.fori_loop(0, PEER_HEADS // 4, pair_body, 0)

    def table_body(h, carry):
        for grp in range(n_groups):
            lanes = slice(grp * LANES, (grp + 1) * LANES)
            max1 = stat_scr[h, 0, grp:grp + 1, :]
            max2 = stat_scr[h, 1, grp:grp + 1, :]
            inv_z = stat_scr[h, 2, grp:grp + 1, :]
            rank1 = rank_scr[2 * h, grp]
            r1_key = jnp.zeros(rank1.shape, F32)
            for a in range(PEER_TOPK):
                r1_key = jnp.where(rank1 == float(a), r1top_scr[h, a, grp:grp + 1, :], r1_key)
            r1_ref[h, :, lanes] = _dup_bf16_bits(r1_key)
            e1z = jnp.exp(s_scr[2 * h, grp] - max1) * inv_z
            e1z_ref[h, :, lanes] = _dup_bf16_bits(e1z)
            rk_lo = _pair_rows(rank_scr, 2 * h + 1, grp, 0)
            rk_hi = _pair_rows(rank_scr, 2 * h + 1, grp, 1)
            rank2_ref[h, :, lanes] = _pack_bf16_pairs(rk_lo, rk_hi)
            e2_lo = jnp.exp(jnp.minimum(_pair_rows(s_scr, 2 * h + 1, grp, 0) - max2, 0.0))
            e2_hi = jnp.exp(jnp.minimum(_pair_rows(s_scr, 2 * h + 1, grp, 1) - max2, 0.0))
            e2_ref[h, :, lanes] = _pack_bf16_pairs(e2_lo, e2_hi)
        return carry

    lax.fori_loop(0, PEER_HEADS, table_body, 0)


def peer_route(x, g, wqT, sk, *, tm=512):
    T, D = x.shape
    H = PEER_HEADS
    n_groups = tm // LANES
    tab = lambda rows: jax.ShapeDtypeStruct((H, rows, T), jnp.uint32)
    tab_spec = lambda rows: pl.BlockSpec((H, rows, tm), lambda i: (0, 0, i))
    return pl.pallas_call(
        _route_kernel,
        out_shape=(tab(N_KEYS // 2), tab(N_KEYS // 2), tab(N_KEYS), tab(N_KEYS)),
        grid=(T // tm,),
        in_specs=[
            pl.BlockSpec((tm, D), lambda i: (i, 0)),
            pl.BlockSpec((1, D), lambda i: (0, 0)),
            pl.BlockSpec((H * 2 * PEER_HALF, D), lambda i: (0, 0)),
            pl.BlockSpec((2 * H, N_KEYS, PEER_HALF), lambda i: (0, 0, 0)),
        ],
        out_specs=(tab_spec(N_KEYS // 2), tab_spec(N_KEYS // 2), tab_spec(N_KEYS), tab_spec(N_KEYS)),
        scratch_shapes=[
            pltpu.VMEM((tm, D), BF16),
            pltpu.VMEM((2 * H, n_groups, N_KEYS, LANES), F32),
            pltpu.VMEM((2 * H, n_groups, N_KEYS, LANES), F32),
            pltpu.VMEM((2 * H, PEER_TOPK, n_groups, LANES), F32),
            pltpu.VMEM((H, PEER_TOPK, n_groups, LANES), F32),
            pltpu.VMEM((H, 3, n_groups, LANES), F32),
        ],
        compiler_params=_cparams(("parallel",)),
        name="peer_route",
    )(x, g.reshape(1, D).astype(F32), wqT, sk)


GELU_C0 = math.sqrt(2.0 / math.pi)
GELU_C1 = 0.044715


def _gelu_tanh(a):
    inner = GELU_C0 * (a + GELU_C1 * (a * a * a))
    return 0.5 * a * (1.0 + jnp.tanh(inner))


PEER_SUB = 512


def _peer_main_kernel(x_ref, g_ref, rank2_ref, e2_ref, r1_ref, e1z_ref, u_ref, vt_ref, gf_ref, o_ref,
                      xn_scr, acc_scr, act_scr, coef_scr, *, final_norm):
    j = pl.program_id(1)
    tm = x_ref.shape[0]
    n_i1 = r1_ref.shape[1]
    rows = 16
    words = rows * jnp.dtype(BF16).itemsize // 4

    @pl.when(j == 0)
    def _():
        xn_scr[...] = _rms_normed(x_ref[...], g_ref[...]).astype(BF16)
        acc_scr[...] = jnp.zeros_like(acc_scr)

    eb = 2 * u_ref.shape[0]
    d_model = 2 * vt_ref.shape[0]
    tn = 2 * LANES

    def stage_a(n, q):
        r = slice(q * PEER_SUB, (q + 1) * PEER_SUB)
        rw = slice(q * PEER_SUB // 2, (q + 1) * PEER_SUB // 2)
        c = slice(n * tn, (n + 1) * tn)
        return lax.dot_general(pltpu.bitcast(u_ref[rw, :], BF16), xn_scr[c, :],
                               (((1,), (1,)), ((), ())), preferred_element_type=F32)

    def stage_b(n, q, act):
        for sub in range(tn // LANES):
            lanes = slice(n * tn + sub * LANES, n * tn + (sub + 1) * LANES)
            sl = slice(sub * LANES, (sub + 1) * LANES)
            n_blk = N_KEYS // rows
            for i1 in range(q * PEER_SUB // N_KEYS, (q + 1) * PEER_SUB // N_KEYS, 2):
                gate = [[None] * n_blk, [None] * n_blk]
                for h in range(PEER_HEADS):
                    r1b, e1b = [], []
                    for d in range(2):
                        r1row = jnp.broadcast_to(r1_ref[h, i1 + d:i1 + d + 1, lanes], (words, LANES))
                        e1row = jnp.broadcast_to(e1z_ref[h, i1 + d:i1 + d + 1, lanes], (words, LANES))
                        r1b.append(pltpu.bitcast(r1row, BF16))
                        e1b.append(pltpu.bitcast(e1row, BF16))
                    for blk in range(n_blk):
                        i2w = slice(blk * words, (blk + 1) * words)
                        rk = pltpu.bitcast(rank2_ref[h, i2w, lanes], BF16)
                        ev = pltpu.bitcast(e2_ref[h, i2w, lanes], BF16)
                        for d in range(2):
                            term = jnp.where(rk < r1b[d], ev, jnp.zeros((), BF16)) * e1b[d]
                            gate[d][blk] = term if gate[d][blk] is None else gate[d][blk] + term
                for d in range(2):
                    for blk in range(n_blk):
                        r0 = (i1 + d) * N_KEYS + blk * rows
                        ra = r0 - q * PEER_SUB
                        a = act[ra:ra + rows, sl].astype(BF16)
                        coef_scr[n, r0:r0 + rows, sl] = _gelu_tanh(a) * gate[d][blk]

    def stage_c(n):
        for half in range(2):
            r = slice(half * (d_model // 2), (half + 1) * (d_model // 2))
            rw = slice(half * (d_model // 4), (half + 1) * (d_model // 4))
            res = jnp.dot(pltpu.bitcast(vt_ref[rw, :], BF16), coef_scr[n], preferred_element_type=F32)
            for sub in range(tn // LANES):
                acc_scr[n * (tn // LANES) + sub, r, :] += res[:, sub * LANES:(sub + 1) * LANES]

    for n in range(tm // tn):
        for q in range(eb // PEER_SUB):
            stage_b(n, q, stage_a(n, q))
        stage_c(n)

    @pl.when(j == pl.num_programs(1) - 1)
    def _():
        lo_rows = pl.ds(0, d_model // 2, stride=2)
        hi_rows = pl.ds(1, d_model // 2, stride=2)
        peer_out = jnp.concatenate(
            [jnp.concatenate([acc_scr[grp, lo_rows, :], acc_scr[grp, hi_rows, :]], axis=0).T
             for grp in range(tm // LANES)], axis=0)
        y = x_ref[...] + peer_out
        if final_norm:
            y = _rms_normed(y, gf_ref[...])
        o_ref[...] = y


def _bf16_words(lo, hi):
    bits = lambda a: lax.bitcast_convert_type(a.astype(BF16), jnp.uint16).astype(jnp.uint32)
    return bits(lo) | (bits(hi) << 16)


def _pack_expert_rows(u):
    e, d = u.shape
    u4 = u.reshape(e // (2 * PAIR), 2, PAIR, d)
    return _bf16_words(u4[:, 0], u4[:, 1]).reshape(e // 2, d)


def _pack_vt(v):
    e, d = v.shape
    v_perm = v.reshape(e // (2 * PAIR), 2, PAIR, d).transpose(0, 2, 1, 3).reshape(e, d)
    vt = v_perm.T
    return _bf16_words(vt[:d // 2], vt[d // 2:])


def peer_main(x, g, tables, u, vt, g_final, *, final_norm, tm=512, eb=2048):
    T, D = x.shape
    E = 2 * u.shape[0]
    H = PEER_HEADS
    rank2, e2, r1, e1z = tables
    n_i1 = eb // N_KEYS
    full_tab = pl.BlockSpec((H, N_KEYS // 2, tm), lambda i, j: (0, 0, i))
    i1_tab = pl.BlockSpec((H, n_i1, tm), lambda i, j: (0, j, i))
    return pl.pallas_call(
        functools.partial(_peer_main_kernel, final_norm=final_norm),
        out_shape=jax.ShapeDtypeStruct((T, D), F32),
        grid=(T // tm, E // eb),
        in_specs=[
            pl.BlockSpec((tm, D), lambda i, j: (i, 0)),
            pl.BlockSpec((1, D), lambda i, j: (0, 0)),
            full_tab, full_tab, i1_tab, i1_tab,
            pl.BlockSpec((eb // 2, D), lambda i, j: (j, 0)),
            pl.BlockSpec((D // 2, eb), lambda i, j: (0, j)),
            pl.BlockSpec((1, D), lambda i, j: (0, 0)),
        ],
        out_specs=pl.BlockSpec((tm, D), lambda i, j: (i, 0)),
        scratch_shapes=[
            pltpu.VMEM((tm, D), BF16),
            pltpu.VMEM((tm // LANES, D, LANES), F32),
            pltpu.VMEM((tm // (2 * LANES), eb, 2 * LANES), F32),
            pltpu.VMEM((tm // (2 * LANES), eb, 2 * LANES), BF16),
        ],
        compiler_params=_cparams(("parallel", "arbitrary")),
        name="peer_main",
    )(x, g.reshape(1, D).astype(F32), rank2, e2, r1, e1z, u, vt, g_final.reshape(1, D).astype(F32))


def peer_layer(x, g, w_query, sub_keys, expert_u, expert_v, g_final, *, final_norm):
    wqT = w_query.T.astype(BF16)
    sk = sub_keys.reshape(2 * PEER_HEADS, N_KEYS, PEER_HALF).astype(BF16)
    tables = peer_route(x, g, wqT, sk)
    return peer_main(x, g, tables, _pack_expert_rows(expert_u), _pack_vt(expert_v), g_final,
                     final_norm=final_norm)


ATTN_TILE = 256
ATTN_Q = 512
ATTN_GROUP = 1
ATTN_UNROLL = 4
ATTN_ONES = 16


def _attn_kernel(lam_ref, q_ref, k_ref, v_ref, bias_ref, gsub_ref, o_ref, vt_scr, m_scr, acc_scr,
                 s0_scr, s1_scr, p0_scr, p1_scr, *, out_scale):
    i = pl.program_id(2)
    tq = q_ref.shape[1]
    tk = ATTN_TILE
    ratio = tq // tk
    heads = range(q_ref.shape[2] // A_V_DIM)
    cols = lambda g: slice(g * A_V_DIM, (g + 1) * A_V_DIM)

    @pl.when(i == 0)
    def _():
        for g in heads:
            vt_scr[g, :A_V_DIM, :] = v_ref[0, :, cols(g)].T
            vt_scr[g, A_V_DIM:, :] = jnp.ones((ATTN_ONES, vt_scr.shape[2]), BF16)

    scale = jnp.asarray(A_QK_DIM ** -0.5, BF16)
    zero = jnp.zeros((), BF16)
    qs = []
    for g in heads:
        q = q_ref[0, :, cols(g)]
        lane = lax.broadcasted_iota(jnp.int32, q.shape, 1)
        qs.append(jnp.concatenate([jnp.where(lane < A_QK_DIM, q, zero) * scale,
                                   jnp.where(lane >= A_QK_DIM, q, zero) * scale], axis=0))

    m_scr[...] = jnp.full(m_scr.shape, NEG_BIG, F32)
    acc_scr[...] = jnp.zeros(acc_scr.shape, F32)

    def scores(j, g):
        k = k_ref[0, pl.ds(pl.multiple_of(j * tk, tk), tk), cols(g)]
        return lax.dot_general(k, qs[g], (((1,), (1,)), ((), ())), preferred_element_type=F32)

    def pv(j, p_ref, g):
        vt = vt_scr[g, :, pl.ds(pl.multiple_of(j * tk, tk), tk)]
        return jnp.dot(vt, p_ref[g], preferred_element_type=F32)

    def consume(j, bias_idx, cur, nxt, prefetch):
        s_cur, p_cur = cur
        s_nxt, p_prev = nxt
        for g in heads:
            if prefetch:
                s_nxt[g] = scores(j + 1, g)
            pv_prev = pv(jnp.maximum(j - 1, 0), p_prev, g)
            s = s_cur[g]
            if bias_idx is not None:
                bias = bias_ref[g, bias_idx]
                s = s + jnp.concatenate([bias, bias], axis=1)
            m_prev = m_scr[g]
            m_new = jnp.maximum(m_prev, jnp.max(s, axis=0, keepdims=True))
            alpha = jnp.exp(m_prev - m_new)
            p_cur[g] = jnp.exp((s - m_new).astype(BF16))
            acc_scr[g] = alpha * (acc_scr[g] + pv_prev)
            m_scr[g] = m_new

    for g in heads:
        s0_scr[g] = scores(0, g)
    p1_scr[...] = jnp.zeros(p1_scr.shape, BF16)

    n_far = jnp.maximum(ratio * i - 1, 0)
    even_bufs, odd_bufs = (s0_scr, p0_scr), (s1_scr, p1_scr)

    def far_tiles(first, count):
        for t in range(count):
            cur, nxt = (even_bufs, odd_bufs) if t % 2 == 0 else (odd_bufs, even_bufs)
            consume(first + t, None, cur, nxt, True)

    def far_body(jj, carry):
        far_tiles(ATTN_UNROLL * jj, ATTN_UNROLL)
        return carry

    lax.fori_loop(0, n_far // ATTN_UNROLL, far_body, 0)
    rest = n_far % ATTN_UNROLL
    done = n_far - rest

    @pl.when(rest >= 2)
    def _():
        far_tiles(done, 2)

    @pl.when(rest % 2 == 1)
    def _():
        far_tiles(n_far - 1, 1)

    def near_tiles(with_prev):
        if with_prev:
            consume(ratio * i - 1, 0, odd_bufs, even_bufs, True)
        consume(ratio * i, 1, even_bufs, odd_bufs, True)
        consume(ratio * i + 1, 2, odd_bufs, even_bufs, False)
        for g in heads:
            acc_scr[g] += pv(ratio * i + 1, odd_bufs[1], g)

    @pl.when(i >= 1)
    def _():
        near_tiles(True)

    @pl.when(i == 0)
    def _():
        near_tiles(False)

    for g in heads:
        ot = acc_scr[g, :A_V_DIM, :] / acc_scr[g, A_V_DIM:A_V_DIM + 1, :]
        ot = ot[:, :tq] - lam_ref[0] * ot[:, tq:]
        ot = ot * lax.rsqrt(jnp.mean(ot * ot, axis=0, keepdims=True) + EPS)
        o_ref[0, :, cols(g)] = (ot.T * (gsub_ref[...] * out_scale)).astype(o_ref.dtype)


def _t5_bucket(rel):
    nb = N_BUCKETS // 2
    max_exact = nb // 2
    ret = jnp.where(rel > 0, nb, 0)
    n = jnp.abs(rel)
    nf = jnp.maximum(n, 1).astype(F32)
    large = max_exact + (jnp.log(nf / max_exact) / math.log(MAX_DISTANCE / max_exact)
                         * (nb - max_exact)).astype(jnp.int32)
    large = jnp.minimum(large, nb - 1)
    return ret + jnp.where(n < max_exact, n, large)


def _attn_bias_tiles(rel_bias, tq, tk):
    r = jnp.arange(tq)[None, :]
    rb = rel_bias.astype(F32)

    def lookup(bucket):
        return jnp.einsum("...b,bh->...h", jax.nn.one_hot(bucket, N_BUCKETS, dtype=F32), rb,
                          precision=lax.Precision.HIGHEST)

    far = lookup(_t5_bucket(jnp.asarray(-(tq + tk))))
    tiles = []
    for start in (-tk, 0, tk):
        c = jnp.arange(tk)[:, None] + start
        bias = lookup(_t5_bucket(c - r)) - far
        tiles.append(jnp.where((jnp.floor_divide(c, CHUNK) <= r // CHUNK)[..., None], bias, NEG_BIG))
    return jnp.transpose(jnp.stack(tiles, axis=0), (3, 0, 1, 2))


def diff_attention(proj, lam, bias_tiles, subln_g, *, out_scale):
    B, S, _ = proj.shape
    H = A_HEADS
    G = ATTN_GROUP
    tq, tk = ATTN_Q, ATTN_TILE
    width = G * A_V_DIM
    return pl.pallas_call(
        functools.partial(_attn_kernel, out_scale=out_scale),
        out_shape=jax.ShapeDtypeStruct((B, S, H * A_V_DIM), BF16),
        grid=(B, H // G, S // tq),
        in_specs=[
            pl.BlockSpec(memory_space=pltpu.SMEM),
            pl.BlockSpec((1, tq, width), lambda b, h, i: (b, i, h)),
            pl.BlockSpec((1, S, width), lambda b, h, i: (b, 0, H // G + h)),
            pl.BlockSpec((1, S, width), lambda b, h, i: (b, 0, 2 * (H // G) + h)),
            pl.BlockSpec((G, 3, tk, tq), lambda b, h, i: (h, 0, 0, 0)),
            pl.BlockSpec((1, A_V_DIM), lambda b, h, i: (0, 0)),
        ],
        out_specs=pl.BlockSpec((1, tq, width), lambda b, h, i: (b, i, h)),
        scratch_shapes=[
            pltpu.VMEM((G, A_V_DIM + ATTN_ONES, S), BF16),
            pltpu.VMEM((G, 1, 2 * tq), F32),
            pltpu.VMEM((G, A_V_DIM + ATTN_ONES, 2 * tq), F32),
            pltpu.VMEM((G, tk, 2 * tq), F32),
            pltpu.VMEM((G, tk, 2 * tq), F32),
            pltpu.VMEM((G, tk, 2 * tq), BF16),
            pltpu.VMEM((G, tk, 2 * tq), BF16),
        ],
        compiler_params=_cparams(("parallel", "parallel", "arbitrary")),
        name="diff_attention",
    )(lam.reshape(1).astype(F32), proj, proj, proj, bias_tiles, subln_g.reshape(1, A_V_DIM).astype(F32))


CONV_HALO = 32
CONV_ROWS = 64


def _conv_kernel(val_ref, gate_ref, hval_ref, hgate_ref, w_ref, b_ref, g_ref, beta_ref, o_ref, buf_scr,
                 win_scr):
    i = pl.program_id(1)
    ts = val_ref.shape[1]

    def glu(v, gt):
        return v.astype(F32) * jax.nn.sigmoid(gt.astype(F32))

    halo = glu(hval_ref[0], hgate_ref[0])
    buf_scr[0:CONV_HALO, :] = jnp.where(i > 0, halo, 0.0)
    buf_scr[CONV_HALO:, :] = glu(val_ref[0], gate_ref[0])

    first = CONV_HALO - (CONV_WIDTH - 1)
    sub = 8
    n_ch = buf_scr.shape[1]
    for r in range(ts // CONV_ROWS):
        halves = []
        for c0 in range(0, n_ch, n_ch // 2):
            ch = slice(c0, c0 + n_ch // 2)
            acc = None
            for b in range(sub):
                taps = range(b, CONV_WIDTH, sub)
                lo = r * CONV_ROWS + first + b
                rows_b = CONV_ROWS + sub * (len(taps) - 1)
                win_scr[b, 0:rows_b, :] = buf_scr[lo:lo + rows_b, ch]
                for a, w in enumerate(taps):
                    term = win_scr[b, sub * a:sub * a + CONV_ROWS, :] * w_ref[w:w + 1, ch]
                    acc = term if acc is None else acc + term
            halves.append(acc)
        y = jnp.concatenate(halves, axis=1) + b_ref[...]
        mu = jnp.mean(y, axis=-1, keepdims=True)
        d = y - mu
        var = jnp.mean(d * d, axis=-1, keepdims=True)
        y = d * lax.rsqrt(var + EPS) * g_ref[...] + beta_ref[...]
        y = y * jax.nn.sigmoid(y)
        o_ref[0, r * CONV_ROWS:(r + 1) * CONV_ROWS, :] = y.astype(o_ref.dtype)


def conv_module(proj, conv_w, conv_b, ln_g, ln_b, *, ts=512):
    B, S, _ = proj.shape
    C = conv_w.shape[1]
    hb = ts // CONV_HALO
    row = lambda a: a.reshape(1, C).astype(F32)
    cur = lambda col: pl.BlockSpec((1, ts, C), lambda b, i: (b, i, col))
    halo = lambda col: pl.BlockSpec((1, CONV_HALO, C), lambda b, i: (b, jnp.maximum(i * hb - 1, 0), col))
    vec = pl.BlockSpec((1, C), lambda b, i: (0, 0))
    return pl.pallas_call(
        _conv_kernel,
        out_shape=jax.ShapeDtypeStruct((B, S, C), BF16),
        grid=(B, S // ts),
        in_specs=[cur(3), cur(4), halo(3), halo(4),
                  pl.BlockSpec((CONV_WIDTH + 1, C), lambda b, i: (0, 0)), vec, vec, vec],
        out_specs=pl.BlockSpec((1, ts, C), lambda b, i: (b, i, 0)),
        scratch_shapes=[pltpu.VMEM((ts + CONV_HALO, C), F32),
                        pltpu.VMEM((8, CONV_ROWS + CONV_HALO, C // 2), F32)],
        compiler_params=_cparams(("parallel", "parallel")),
        name="conv_module",
    )(proj, proj, proj, proj, jnp.pad(conv_w.astype(F32), ((0, 1), (0, 0))), row(conv_b), row(ln_g), row(ln_b))


GLA_DK = 128
GLA_DV = 256
GLA_Z_COLS = 128


def _split_bf16(x):
    hi = x.astype(BF16)
    lo = (x - hi.astype(F32)).astype(BF16)
    return hi, lo


def _gla_kernel(q_ref, k_ref, v_ref, g_ref, z_ref, wg_ref, bg_ref, gn_ref, o_ref, state_scr):
    ts = q_ref.shape[0]
    n_chunks = ts // CHUNK

    @pl.when(pl.program_id(1) == 0)
    def _():
        state_scr[...] = jnp.zeros(state_scr.shape, F32)

    zz = jnp.dot(z_ref[...], wg_ref[...], preferred_element_type=F32) + bg_ref[...]
    log_a = (jnp.minimum(zz, 0.0) - jnp.log1p(jnp.exp(-jnp.abs(zz)))) / GATE_TAU

    row = lax.broadcasted_iota(jnp.int32, (CHUNK, CHUNK), 0)
    col = lax.broadcasted_iota(jnp.int32, (CHUNK, CHUNK), 1)
    tri = jnp.where(col <= row, 1.0, 0.0).astype(BF16)
    ones = jnp.ones((CHUNK, LANES), BF16)
    tn = (((0,), (0,)), ((), ()))
    q_scale = GLA_DK ** -0.5

    for c in range(n_chunks):
        rows = slice(c * CHUNK, (c + 1) * CHUNK)
        la_hi, la_lo = _split_bf16(log_a[rows, :])
        cum = (jnp.dot(tri, la_hi, preferred_element_type=F32)
               + jnp.dot(tri, la_lo, preferred_element_type=F32))
        k_dec = k_ref[rows, :].astype(F32) * jnp.exp(cum[CHUNK - 1:CHUNK, :] - cum)
        for h in range(GLA_HEADS):
            kc = slice(h * GLA_DK, (h + 1) * GLA_DK)
            vc = slice(h * GLA_DV, (h + 1) * GLA_DV)
            tot = (lax.dot_general(la_hi[:, kc], ones, tn, preferred_element_type=F32)
                   + lax.dot_general(la_lo[:, kc], ones, tn, preferred_element_type=F32))
            decay = jnp.exp(tot)
            decay = jnp.concatenate([decay] * (GLA_DV // LANES), axis=1)
            kv = lax.dot_general(k_dec[:, kc].astype(BF16), v_ref[rows, vc], tn,
                                 preferred_element_type=F32)
            state = state_scr[h] * decay + kv
            state_scr[h] = state
            o = jnp.dot(q_ref[rows, kc], state.astype(BF16), preferred_element_type=F32) * q_scale
            o = o * lax.rsqrt(jnp.mean(o * o, axis=-1, keepdims=True) + EPS) * gn_ref[...]
            gt = g_ref[rows, vc].astype(F32)
            o_ref[rows, vc] = (o * (gt * jax.nn.sigmoid(gt))).astype(o_ref.dtype)


def gla_scan(proj, w_gate, b_gate, norm_g, *, batch, ts=512):
    T = proj.shape[0]
    S = T // batch
    nb = S // ts
    KD = GLA_HEADS * GLA_DK
    VD = GLA_HEADS * GLA_DV
    rows = lambda width, col: pl.BlockSpec((ts, width), lambda b, i: (b * nb + i, col))
    return pl.pallas_call(
        _gla_kernel,
        out_shape=jax.ShapeDtypeStruct((T, VD), BF16),
        grid=(batch, nb),
        in_specs=[
            rows(KD, 0), rows(KD, 1), rows(VD, 1), rows(VD, 2), rows(GLA_Z_COLS, (2 * KD + 2 * VD) // GLA_Z_COLS),
            pl.BlockSpec((GLA_Z_COLS, KD), lambda b, i: (0, 0)),
            pl.BlockSpec((1, KD), lambda b, i: (0, 0)),
            pl.BlockSpec((1, GLA_DV), lambda b, i: (0, 0)),
        ],
        out_specs=pl.BlockSpec((ts, VD), lambda b, i: (b * nb + i, 0)),
        scratch_shapes=[pltpu.VMEM((GLA_HEADS, GLA_DK, GLA_DV), F32)],
        compiler_params=_cparams(("parallel", "arbitrary")),
        name="gla_scan",
    )(proj, proj, proj, proj, proj, w_gate, b_gate.reshape(1, KD).astype(F32),
      norm_g.reshape(1, GLA_DV).astype(F32))


def kernel(x, rel_bias, ln_mix, ln_ffn, even_w_in, lam_q1, lam_k1, lam_q2, lam_k2, subln_g, conv_w, conv_b, conv_ln_g, conv_ln_b, even_w_out, odd_w_in, w_gate2, b_gate2, gla_norm_g, odd_w_out, peer_w_query, peer_sub_keys, peer_u, peer_v, ln_final):
    B, S, D = x.shape
    T = B * S
    xf = x.reshape(T, D)

    lam_init = 0.8 - 0.6 * math.exp(-0.3 * 0)
    lam = (jnp.exp(jnp.sum(lam_q1[0].astype(F32) * lam_k1[0].astype(F32)))
           - jnp.exp(jnp.sum(lam_q2[0].astype(F32) * lam_k2[0].astype(F32))) + lam_init)
    proj = norm_matmul(xf, ln_mix[0], even_w_in[0].astype(BF16)).reshape(B, S, -1)
    a_out = diff_attention(proj, lam, _attn_bias_tiles(rel_bias, ATTN_Q, ATTN_TILE), subln_g[0],
                           out_scale=1.0 - lam_init)
    c_out = conv_module(proj, conv_w[0], conv_b[0], conv_ln_g[0], conv_ln_b[0])
    a_width = A_HEADS * A_V_DIM
    w_out = even_w_out[0].astype(BF16)
    xf = proj_residual(xf, a_out.reshape(T, -1), c_out.reshape(T, -1), w_out[:a_width], w_out[a_width:])
    xf = peer_layer(xf, ln_ffn[0], peer_w_query[0], peer_sub_keys[0], peer_u[0], peer_v[0], ln_final,
                    final_norm=False)

    kd = GLA_HEADS * GLA_DK
    vd = GLA_HEADS * GLA_DV
    w_in = jnp.pad(odd_w_in[0], ((0, 0), (0, GLA_Z_COLS - GATE_RANK))).astype(BF16)
    w_gate = jnp.pad(w_gate2[0], ((0, GLA_Z_COLS - GATE_RANK), (0, 0))).astype(BF16)
    proj = norm_matmul(xf, ln_mix[1], w_in)
    og = gla_scan(proj, w_gate, b_gate2[0], gla_norm_g[0], batch=B)
    w_out = odd_w_out[0].astype(BF16)
    xf = proj_residual(xf, og, og, w_out[:vd // 2], w_out[vd // 2:], a_col=0, c_col=1)
    xf = peer_layer(xf, ln_ffn[1], peer_w_query[1], peer_sub_keys[1], peer_u[1], peer_v[1], ln_final,
                    final_norm=True)
    return xf.reshape(B, S, D)
```

```python
import functools
import math

import jax
import jax.numpy as jnp
import numpy as np
from jax import lax
from jax.experimental import pallas as pl
from jax.experimental.pallas import tpu as pltpu

F32 = jnp.float32
BF16 = jnp.bfloat16

EPS = 1e-6
LANES = 128
VMEM_LIMIT = 56 * 1024 * 1024

CHUNK = 64
N_BUCKETS = 32
MAX_DISTANCE = 128
A_HEADS = 4
A_QK_DIM = 64
A_V_DIM = 128
CONV_WIDTH = 31
GLA_HEADS = 4
GATE_RANK = 16
GATE_TAU = 16.0
PEER_HEADS = 8
N_KEYS = 128
PEER_TOPK = 16
PEER_HALF = 128
NEG_BIG = -1e30


def _cparams(sem):
    return pltpu.CompilerParams(dimension_semantics=sem, vmem_limit_bytes=VMEM_LIMIT)


def _rms_normed(x, g):
    ms = jnp.mean(x * x, axis=-1, keepdims=True)
    return x * lax.rsqrt(ms + EPS) * g


def _norm_matmul_kernel(x_ref, g_ref, w_ref, o_ref):
    xn = _rms_normed(x_ref[...], g_ref[...]).astype(BF16)
    o_ref[...] = jnp.dot(xn, w_ref[...], preferred_element_type=F32).astype(o_ref.dtype)


def norm_matmul(x, g, w, *, tm=512, out_dtype=BF16):
    T, D = x.shape
    N = w.shape[1]
    return pl.pallas_call(
        _norm_matmul_kernel,
        out_shape=jax.ShapeDtypeStruct((T, N), out_dtype),
        grid=(T // tm,),
        in_specs=[
            pl.BlockSpec((tm, D), lambda i: (i, 0)),
            pl.BlockSpec((1, D), lambda i: (0, 0)),
            pl.BlockSpec((D, N), lambda i: (0, 0)),
        ],
        out_specs=pl.BlockSpec((tm, N), lambda i: (i, 0)),
        compiler_params=_cparams(("parallel",)),
        name="norm_matmul",
    )(x, g.reshape(1, D).astype(F32), w)


def _proj_residual_kernel(x_ref, a_ref, c_ref, wa_ref, wc_ref, o_ref):
    acc = jnp.dot(a_ref[...], wa_ref[...], preferred_element_type=F32)
    acc = acc + jnp.dot(c_ref[...], wc_ref[...], preferred_element_type=F32)
    o_ref[...] = x_ref[...] + acc


def proj_residual(x, a, c, wa, wc, *, a_col=0, c_col=0, tm=512):
    T, D = x.shape
    Ka, Kc = wa.shape[0], wc.shape[0]
    return pl.pallas_call(
        _proj_residual_kernel,
        out_shape=jax.ShapeDtypeStruct((T, D), F32),
        grid=(T // tm,),
        in_specs=[
            pl.BlockSpec((tm, D), lambda i: (i, 0)),
            pl.BlockSpec((tm, Ka), lambda i: (i, a_col)),
            pl.BlockSpec((tm, Kc), lambda i: (i, c_col)),
            pl.BlockSpec((Ka, D), lambda i: (0, 0)),
            pl.BlockSpec((Kc, D), lambda i: (0, 0)),
        ],
        out_specs=pl.BlockSpec((tm, D), lambda i: (i, 0)),
        compiler_params=_cparams(("parallel",)),
        name="proj_residual",
    )(x, a, c, wa, wc)


def _pair_candidates():
    return [(a, b) for a in range(PEER_TOPK) for b in range(PEER_TOPK) if (a + 1) * (b + 1) <= PEER_TOPK]


def _dup_bf16_bits(v):
    hi = pltpu.bitcast(v.astype(BF16).astype(F32), jnp.uint32)
    return hi | (hi >> 16)


PAIR = 8


def _pack_bf16_pairs(lo, hi):
    bl = pltpu.bitcast(lo.astype(BF16).astype(F32), jnp.uint32)
    bh = pltpu.bitcast(hi.astype(BF16).astype(F32), jnp.uint32)
    return (bl >> 16) | bh


def _pair_rows(scr, idx, grp, half):
    n_rows = scr.shape[2]
    return jnp.concatenate([scr[idx, grp, g + PAIR * half:g + PAIR * (half + 1), :]
                            for g in range(0, n_rows, 2 * PAIR)], axis=0)


def _route_kernel(x_ref, g_ref, wqT_ref, sk_ref, rank2_ref, e2_ref, r1_ref, e1z_ref,
                  xn_scr, s_scr, rank_scr, top_scr, r1top_scr, stat_scr):
    tm = x_ref.shape[0]
    n_groups = tm // LANES
    xn_scr[...] = _rms_normed(x_ref[...], g_ref[...]).astype(BF16)

    def score_body(hh, carry):
        for h in (2 * hh, 2 * hh + 1):
            w_h = wqT_ref[pl.ds(pl.multiple_of(h * 2 * PEER_HALF, 2 * PEER_HALF), 2 * PEER_HALF), :]
            qT = lax.dot_general(w_h, xn_scr[...], (((1,), (1,)), ((), ())),
                                 preferred_element_type=F32).astype(BF16)
            for p in range(2):
                s = jnp.dot(sk_ref[2 * h + p], qT[p * PEER_HALF:(p + 1) * PEER_HALF, :],
                            preferred_element_type=F32)
                for grp in range(n_groups):
                    s_scr[2 * h + p, grp] = s[:, grp * LANES:(grp + 1) * LANES]
        return carry

    lax.fori_loop(0, PEER_HEADS // 2, score_body, 0)

    def top_body(hp, carry):
        for grp in range(n_groups):
            work = s_scr[hp, grp]
            rank = jnp.full(work.shape, float(PEER_TOPK), F32)
            for k in range(PEER_TOPK):
                m = jnp.max(work, axis=0, keepdims=True)
                eq = work == m
                rank = jnp.where(eq, float(k), rank)
                work = jnp.where(eq, -jnp.inf, work)
                top_scr[hp, k, grp:grp + 1, :] = m
            rank_scr[hp, grp] = rank
        return carry

    lax.fori_loop(0, 2 * PEER_HEADS, top_body, 0)

    cands = _pair_candidates()

    def pair_body(hq, carry):
        pair_heads(4 * hq)
        pair_heads(4 * hq + 2)
        return carry

    def pair_heads(h):
        def both(scr, idx, k):
            return jnp.concatenate([scr[idx(h), k], scr[idx(h + 1), k]], axis=0)

        v1 = [both(top_scr, lambda x: 2 * x, a) for a in range(PEER_TOPK)]
        v2 = [both(top_scr, lambda x: 2 * x + 1, b) for b in range(PEER_TOPK)]
        sums = [v1[a] + v2[b] for (a, b) in cands]
        work = list(sums)
        tau = None
        for k in range(PEER_TOPK):
            tau = functools.reduce(jnp.maximum, work)
            if k + 1 < PEER_TOPK:
                work = [jnp.where(w == tau, -jnp.inf, w) for w in work]
        cmax = v1[0] + v2[0]
        z = jnp.zeros_like(tau)
        r1 = [jnp.zeros_like(tau) for _ in range(PEER_TOPK)]
        for (a, b), c in zip(cands, sums):
            sel = c >= tau
            z = z + jnp.where(sel, jnp.exp(c - cmax), 0.0)
            r1[a] = r1[a] + jnp.where(sel, 1.0, 0.0)
        inv_z = 1.0 / z
        for d in range(2):
            part = slice(d * n_groups, (d + 1) * n_groups)
            for a in range(PEER_TOPK):
                r1top_scr[h + d, a] = r1[a][part]
            stat_scr[h + d, 0] = v1[0][part]
            stat_scr[h + d, 1] = v2[0][part]
            stat_scr[h + d, 2] = inv_z[part]

    lax.fori_loop(0, PEER_HEADS // 4, pair_body, 0)

    def table_body(h, carry):
        for grp in range(n_groups):
            lanes = slice(grp * LANES, (grp + 1) * LANES)
            max1 = stat_scr[h, 0, grp:grp + 1, :]
            max2 = stat_scr[h, 1, grp:grp + 1, :]
            inv_z = stat_scr[h, 2, grp:grp + 1, :]
            rank1 = rank_scr[2 * h, grp]
            r1_key = jnp.zeros(rank1.shape, F32)
            for a in range(PEER_TOPK):
                r1_key = jnp.where(rank1 == float(a), r1top_scr[h, a, grp:grp + 1, :], r1_key)
            r1_ref[h, :, lanes] = _dup_bf16_bits(r1_key)
            e1z = jnp.exp(s_scr[2 * h, grp] - max1) * inv_z
            e1z_ref[h, :, lanes] = _dup_bf16_bits(e1z)
            rk_lo = _pair_rows(rank_scr, 2 * h + 1, grp, 0)
            rk_hi = _pair_rows(rank_scr, 2 * h + 1, grp, 1)
            rank2_ref[h, :, lanes] = _pack_bf16_pairs(rk_lo, rk_hi)
            e2_lo = jnp.exp(jnp.minimum(_pair_rows(s_scr, 2 * h + 1, grp, 0) - max2, 0.0))
            e2_hi = jnp.exp(jnp.minimum(_pair_rows(s_scr, 2 * h + 1, grp, 1) - max2, 0.0))
            e2_ref[h, :, lanes] = _pack_bf16_pairs(e2_lo, e2_hi)
        return carry

    lax.fori_loop(0, PEER_HEADS, table_body, 0)


def peer_route(x, g, wqT, sk, *, tm=512):
    T, D = x.shape
    H = PEER_HEADS
    n_groups = tm // LANES
    tab = lambda rows: jax.ShapeDtypeStruct((H, rows, T), jnp.uint32)
    tab_spec = lambda rows: pl.BlockSpec((H, rows, tm), lambda i: (0, 0, i))
    return pl.pallas_call(
        _route_kernel,
        out_shape=(tab(N_KEYS // 2), tab(N_KEYS // 2), tab(N_KEYS), tab(N_KEYS)),
        grid=(T // tm,),
        in_specs=[
            pl.BlockSpec((tm, D), lambda i: (i, 0)),
            pl.BlockSpec((1, D), lambda i: (0, 0)),
            pl.BlockSpec((H * 2 * PEER_HALF, D), lambda i: (0, 0)),
            pl.BlockSpec((2 * H, N_KEYS, PEER_HALF), lambda i: (0, 0, 0)),
        ],
        out_specs=(tab_spec(N_KEYS // 2), tab_spec(N_KEYS // 2), tab_spec(N_KEYS), tab_spec(N_KEYS)),
        scratch_shapes=[
            pltpu.VMEM((tm, D), BF16),
            pltpu.VMEM((2 * H, n_groups, N_KEYS, LANES), F32),
            pltpu.VMEM((2 * H, n_groups, N_KEYS, LANES), F32),
            pltpu.VMEM((2 * H, PEER_TOPK, n_groups, LANES), F32),
            pltpu.VMEM((H, PEER_TOPK, n_groups, LANES), F32),
            pltpu.VMEM((H, 3, n_groups, LANES), F32),
        ],
        compiler_params=_cparams(("parallel",)),
        name="peer_route",
    )(x, g.reshape(1, D).astype(F32), wqT, sk)


GELU_C0 = math.sqrt(2.0 / math.pi)
GELU_C1 = 0.044715


def _gelu_tanh(a):
    inner = GELU_C0 * (a + GELU_C1 * (a * a * a))
    return 0.5 * a * (1.0 + jnp.tanh(inner))


PEER_SUB = 512


def _peer_main_kernel(x_ref, g_ref, rank2_ref, e2_ref, r1_ref, e1z_ref, u_ref, vt_ref, gf_ref, o_ref,
                      xn_scr, acc_scr, coef_scr, *, final_norm):
    j = pl.program_id(1)
    tm = x_ref.shape[0]
    n_i1 = r1_ref.shape[1]
    rows = 16
    words = rows * jnp.dtype(BF16).itemsize // 4

    @pl.when(j == 0)
    def _():
        xn_scr[...] = _rms_normed(x_ref[...], g_ref[...]).astype(BF16)
        acc_scr[...] = jnp.zeros_like(acc_scr)

    eb = 2 * u_ref.shape[0]
    d_model = 2 * vt_ref.shape[0]
    tn = 2 * LANES

    def stage_a(n, q):
        r = slice(q * PEER_SUB, (q + 1) * PEER_SUB)
        rw = slice(q * PEER_SUB // 2, (q + 1) * PEER_SUB // 2)
        c = slice(n * tn, (n + 1) * tn)
        return lax.dot_general(pltpu.bitcast(u_ref[rw, :], BF16), xn_scr[c, :],
                               (((1,), (1,)), ((), ())), preferred_element_type=F32)

    def stage_b(n, q, act):
        for sub in range(tn // LANES):
            lanes = slice(n * tn + sub * LANES, n * tn + (sub + 1) * LANES)
            sl = slice(sub * LANES, (sub + 1) * LANES)
            n_blk = N_KEYS // rows
            for i1 in range(q * PEER_SUB // N_KEYS, (q + 1) * PEER_SUB // N_KEYS, 2):
                gate = [[None] * n_blk, [None] * n_blk]
                for h in range(PEER_HEADS):
                    r1b, e1b = [], []
                    for d in range(2):
                        r1row = jnp.broadcast_to(r1_ref[h, i1 + d:i1 + d + 1, lanes], (words, LANES))
                        e1row = jnp.broadcast_to(e1z_ref[h, i1 + d:i1 + d + 1, lanes], (words, LANES))
                        r1b.append(pltpu.bitcast(r1row, BF16))
                        e1b.append(pltpu.bitcast(e1row, BF16))
                    for blk in range(n_blk):
                        i2w = slice(blk * words, (blk + 1) * words)
                        rk = pltpu.bitcast(rank2_ref[h, i2w, lanes], BF16)
                        ev = pltpu.bitcast(e2_ref[h, i2w, lanes], BF16)
                        for d in range(2):
                            term = jnp.where(rk < r1b[d], ev, jnp.zeros((), BF16)) * e1b[d]
                            gate[d][blk] = term if gate[d][blk] is None else gate[d][blk] + term
                for d in range(2):
                    for blk in range(n_blk):
                        r0 = (i1 + d) * N_KEYS + blk * rows
                        ra = r0 - q * PEER_SUB
                        a = act[ra:ra + rows, sl].astype(BF16)
                        coef_scr[n, r0:r0 + rows, sl] = _gelu_tanh(a) * gate[d][blk]

    def stage_c(n):
        for half in range(2):
            r = slice(half * (d_model // 2), (half + 1) * (d_model // 2))
            rw = slice(half * (d_model // 4), (half + 1) * (d_model // 4))
            res = jnp.dot(pltpu.bitcast(vt_ref[rw, :], BF16), coef_scr[n], preferred_element_type=F32)
            for sub in range(tn // LANES):
                acc_scr[n * (tn // LANES) + sub, r, :] += res[:, sub * LANES:(sub + 1) * LANES]

    for n in range(tm // tn):
        for q in range(eb // PEER_SUB):
            stage_b(n, q, stage_a(n, q))
        stage_c(n)

    @pl.when(j == pl.num_programs(1) - 1)
    def _():
        lo_rows = pl.ds(0, d_model // 2, stride=2)
        hi_rows = pl.ds(1, d_model // 2, stride=2)
        peer_out = jnp.concatenate(
            [jnp.concatenate([acc_scr[grp, lo_rows, :], acc_scr[grp, hi_rows, :]], axis=0).T
             for grp in range(tm // LANES)], axis=0)
        y = x_ref[...] + peer_out
        if final_norm:
            y = _rms_normed(y, gf_ref[...])
        o_ref[...] = y


def _bf16_words(lo, hi):
    bits = lambda a: lax.bitcast_convert_type(a.astype(BF16), jnp.uint16).astype(jnp.uint32)
    return bits(lo) | (bits(hi) << 16)


def _pack_expert_rows(u):
    e, d = u.shape
    u4 = u.reshape(e // (2 * PAIR), 2, PAIR, d)
    return _bf16_words(u4[:, 0], u4[:, 1]).reshape(e // 2, d)


def _pack_vt(v):
    e, d = v.shape
    v_perm = v.reshape(e // (2 * PAIR), 2, PAIR, d).transpose(0, 2, 1, 3).reshape(e, d)
    vt = v_perm.T
    return _bf16_words(vt[:d // 2], vt[d // 2:])


def peer_main(x, g, tables, u, vt, g_final, *, final_norm, tm=512, eb=2048):
    T, D = x.shape
    E = 2 * u.shape[0]
    H = PEER_HEADS
    rank2, e2, r1, e1z = tables
    n_i1 = eb // N_KEYS
    full_tab = pl.BlockSpec((H, N_KEYS // 2, tm), lambda i, j: (0, 0, i))
    i1_tab = pl.BlockSpec((H, n_i1, tm), lambda i, j: (0, j, i))
    return pl.pallas_call(
        functools.partial(_peer_main_kernel, final_norm=final_norm),
        out_shape=jax.ShapeDtypeStruct((T, D), F32),
        grid=(T // tm, E // eb),
        in_specs=[
            pl.BlockSpec((tm, D), lambda i, j: (i, 0)),
            pl.BlockSpec((1, D), lambda i, j: (0, 0)),
            full_tab, full_tab, i1_tab, i1_tab,
            pl.BlockSpec((eb // 2, D), lambda i, j: (j, 0)),
            pl.BlockSpec((D // 2, eb), lambda i, j: (0, j)),
            pl.BlockSpec((1, D), lambda i, j: (0, 0)),
        ],
        out_specs=pl.BlockSpec((tm, D), lambda i, j: (i, 0)),
        scratch_shapes=[
            pltpu.VMEM((tm, D), BF16),
            pltpu.VMEM((tm // LANES, D, LANES), F32),
            pltpu.VMEM((tm // (2 * LANES), eb, 2 * LANES), BF16),
        ],
        compiler_params=_cparams(("parallel", "arbitrary")),
        name="peer_main",
    )(x, g.reshape(1, D).astype(F32), rank2, e2, r1, e1z, u, vt, g_final.reshape(1, D).astype(F32))


def peer_layer(x, g, w_query, sub_keys, expert_u, expert_v, g_final, *, final_norm):
    wqT = w_query.T.astype(BF16)
    sk = sub_keys.reshape(2 * PEER_HEADS, N_KEYS, PEER_HALF).astype(BF16)
    tables = peer_route(x, g, wqT, sk)
    return peer_main(x, g, tables, _pack_expert_rows(expert_u), _pack_vt(expert_v), g_final,
                     final_norm=final_norm)


ATTN_TILE = 256
ATTN_Q = 512
ATTN_GROUP = 1
ATTN_UNROLL = 4
ATTN_ONES = 16


def _attn_kernel(lam_ref, q_ref, k_ref, v_ref, bias_ref, gsub_ref, o_ref, vt_scr, m_scr, acc_scr,
                 s0_scr, s1_scr, p0_scr, p1_scr, *, out_scale):
    i = pl.program_id(2)
    tq = q_ref.shape[1]
    tk = ATTN_TILE
    ratio = tq // tk
    heads = range(q_ref.shape[2] // A_V_DIM)
    cols = lambda g: slice(g * A_V_DIM, (g + 1) * A_V_DIM)

    @pl.when(i == 0)
    def _():
        for g in heads:
            vt_scr[g, :A_V_DIM, :] = v_ref[0, :, cols(g)].T
            vt_scr[g, A_V_DIM:, :] = jnp.ones((ATTN_ONES, vt_scr.shape[2]), BF16)

    scale = jnp.asarray(A_QK_DIM ** -0.5, BF16)
    zero = jnp.zeros((), BF16)
    qs = []
    for g in heads:
        q = q_ref[0, :, cols(g)]
        lane = lax.broadcasted_iota(jnp.int32, q.shape, 1)
        qs.append(jnp.concatenate([jnp.where(lane < A_QK_DIM, q, zero) * scale,
                                   jnp.where(lane >= A_QK_DIM, q, zero) * scale], axis=0))

    m_scr[...] = jnp.full(m_scr.shape, NEG_BIG, F32)
    acc_scr[...] = jnp.zeros(acc_scr.shape, F32)

    def scores(j, g):
        k = k_ref[0, pl.ds(pl.multiple_of(j * tk, tk), tk), cols(g)]
        return lax.dot_general(k, qs[g], (((1,), (1,)), ((), ())), preferred_element_type=F32)

    def pv(j, p_ref, g):
        vt = vt_scr[g, :, pl.ds(pl.multiple_of(j * tk, tk), tk)]
        return jnp.dot(vt, p_ref[g], preferred_element_type=F32)

    def consume(j, bias_idx, cur, nxt, prefetch):
        s_cur, p_cur = cur
        s_nxt, p_prev = nxt
        for g in heads:
            if prefetch:
                s_nxt[g] = scores(j + 1, g)
            pv_prev = pv(jnp.maximum(j - 1, 0), p_prev, g)
            s = s_cur[g]
            if bias_idx is not None:
                bias = bias_ref[g, bias_idx]
                s = s + jnp.concatenate([bias, bias], axis=1)
            m_prev = m_scr[g]
            m_new = jnp.maximum(m_prev, jnp.max(s, axis=0, keepdims=True))
            alpha = jnp.exp(m_prev - m_new)
            p_cur[g] = jnp.exp((s - m_new).astype(BF16))
            acc_scr[g] = alpha * (acc_scr[g] + pv_prev)
            m_scr[g] = m_new

    for g in heads:
        s0_scr[g] = scores(0, g)
    p1_scr[...] = jnp.zeros(p1_scr.shape, BF16)

    n_far = jnp.maximum(ratio * i - 1, 0)
    even_bufs, odd_bufs = (s0_scr, p0_scr), (s1_scr, p1_scr)

    def far_tiles(first, count):
        for t in range(count):
            cur, nxt = (even_bufs, odd_bufs) if t % 2 == 0 else (odd_bufs, even_bufs)
            consume(first + t, None, cur, nxt, True)

    def far_body(jj, carry):
        far_tiles(ATTN_UNROLL * jj, ATTN_UNROLL)
        return carry

    lax.fori_loop(0, n_far // ATTN_UNROLL, far_body, 0)
    rest = n_far % ATTN_UNROLL
    done = n_far - rest

    @pl.when(rest >= 2)
    def _():
        far_tiles(done, 2)

    @pl.when(rest % 2 == 1)
    def _():
        far_tiles(n_far - 1, 1)

    def near_tiles(with_prev):
        if with_prev:
            consume(ratio * i - 1, 0, odd_bufs, even_bufs, True)
        consume(ratio * i, 1, even_bufs, odd_bufs, True)
        consume(ratio * i + 1, 2, odd_bufs, even_bufs, False)
        for g in heads:
            acc_scr[g] += pv(ratio * i + 1, odd_bufs[1], g)

    @pl.when(i >= 1)
    def _():
        near_tiles(True)

    @pl.when(i == 0)
    def _():
        near_tiles(False)

    for g in heads:
        ot = acc_scr[g, :A_V_DIM, :] / acc_scr[g, A_V_DIM:A_V_DIM + 1, :]
        ot = ot[:, :tq] - lam_ref[0] * ot[:, tq:]
        ot = ot * lax.rsqrt(jnp.mean(ot * ot, axis=0, keepdims=True) + EPS)
        o_ref[0, :, cols(g)] = (ot.T * (gsub_ref[...] * out_scale)).astype(o_ref.dtype)


def _t5_bucket(rel):
    nb = N_BUCKETS // 2
    max_exact = nb // 2
    ret = jnp.where(rel > 0, nb, 0)
    n = jnp.abs(rel)
    nf = jnp.maximum(n, 1).astype(F32)
    large = max_exact + (jnp.log(nf / max_exact) / math.log(MAX_DISTANCE / max_exact)
                         * (nb - max_exact)).astype(jnp.int32)
    large = jnp.minimum(large, nb - 1)
    return ret + jnp.where(n < max_exact, n, large)


def _attn_bias_tiles(rel_bias, tq, tk):
    r = jnp.arange(tq)[None, :]
    rb = rel_bias.astype(F32)

    def lookup(bucket):
        return jnp.einsum("...b,bh->...h", jax.nn.one_hot(bucket, N_BUCKETS, dtype=F32), rb,
                          precision=lax.Precision.HIGHEST)

    far = lookup(_t5_bucket(jnp.asarray(-(tq + tk))))
    tiles = []
    for start in (-tk, 0, tk):
        c = jnp.arange(tk)[:, None] + start
        bias = lookup(_t5_bucket(c - r)) - far
        tiles.append(jnp.where((jnp.floor_divide(c, CHUNK) <= r // CHUNK)[..., None], bias, NEG_BIG))
    return jnp.transpose(jnp.stack(tiles, axis=0), (3, 0, 1, 2))


def diff_attention(proj, lam, bias_tiles, subln_g, *, out_scale):
    B, S, _ = proj.shape
    H = A_HEADS
    G = ATTN_GROUP
    tq, tk = ATTN_Q, ATTN_TILE
    width = G * A_V_DIM
    return pl.pallas_call(
        functools.partial(_attn_kernel, out_scale=out_scale),
        out_shape=jax.ShapeDtypeStruct((B, S, H * A_V_DIM), BF16),
        grid=(B, H // G, S // tq),
        in_specs=[
            pl.BlockSpec(memory_space=pltpu.SMEM),
            pl.BlockSpec((1, tq, width), lambda b, h, i: (b, i, h)),
            pl.BlockSpec((1, S, width), lambda b, h, i: (b, 0, H // G + h)),
            pl.BlockSpec((1, S, width), lambda b, h, i: (b, 0, 2 * (H // G) + h)),
            pl.BlockSpec((G, 3, tk, tq), lambda b, h, i: (h, 0, 0, 0)),
            pl.BlockSpec((1, A_V_DIM), lambda b, h, i: (0, 0)),
        ],
        out_specs=pl.BlockSpec((1, tq, width), lambda b, h, i: (b, i, h)),
        scratch_shapes=[
            pltpu.VMEM((G, A_V_DIM + ATTN_ONES, S), BF16),
            pltpu.VMEM((G, 1, 2 * tq), F32),
            pltpu.VMEM((G, A_V_DIM + ATTN_ONES, 2 * tq), F32),
            pltpu.VMEM((G, tk, 2 * tq), F32),
            pltpu.VMEM((G, tk, 2 * tq), F32),
            pltpu.VMEM((G, tk, 2 * tq), BF16),
            pltpu.VMEM((G, tk, 2 * tq), BF16),
        ],
        compiler_params=_cparams(("parallel", "parallel", "arbitrary")),
        name="diff_attention",
    )(lam.reshape(1).astype(F32), proj, proj, proj, bias_tiles, subln_g.reshape(1, A_V_DIM).astype(F32))


CONV_HALO = 32
CONV_ROWS = 64


def _conv_kernel(val_ref, gate_ref, hval_ref, hgate_ref, w_ref, b_ref, g_ref, beta_ref, o_ref, buf_scr,
                 win_scr):
    i = pl.program_id(1)
    ts = val_ref.shape[1]

    def glu(v, gt):
        return v.astype(F32) * jax.nn.sigmoid(gt.astype(F32))

    halo = glu(hval_ref[0], hgate_ref[0])
    buf_scr[0:CONV_HALO, :] = jnp.where(i > 0, halo, 0.0)
    buf_scr[CONV_HALO:, :] = glu(val_ref[0], gate_ref[0])

    first = CONV_HALO - (CONV_WIDTH - 1)
    sub = 8
    n_ch = buf_scr.shape[1]
    for r in range(ts // CONV_ROWS):
        halves = []
        for c0 in range(0, n_ch, n_ch // 2):
            ch = slice(c0, c0 + n_ch // 2)
            acc = None
            for b in range(sub):
                taps = range(b, CONV_WIDTH, sub)
                lo = r * CONV_ROWS + first + b
                rows_b = CONV_ROWS + sub * (len(taps) - 1)
                win_scr[b, 0:rows_b, :] = buf_scr[lo:lo + rows_b, ch]
                for a, w in enumerate(taps):
                    term = win_scr[b, sub * a:sub * a + CONV_ROWS, :] * w_ref[w:w + 1, ch]
                    acc = term if acc is None else acc + term
            halves.append(acc)
        y = jnp.concatenate(halves, axis=1) + b_ref[...]
        mu = jnp.mean(y, axis=-1, keepdims=True)
        d = y - mu
        var = jnp.mean(d * d, axis=-1, keepdims=True)
        y = d * lax.rsqrt(var + EPS) * g_ref[...] + beta_ref[...]
        y = y * jax.nn.sigmoid(y)
        o_ref[0, r * CONV_ROWS:(r + 1) * CONV_ROWS, :] = y.astype(o_ref.dtype)


def conv_module(proj, conv_w, conv_b, ln_g, ln_b, *, ts=512):
    B, S, _ = proj.shape
    C = conv_w.shape[1]
    hb = ts // CONV_HALO
    row = lambda a: a.reshape(1, C).astype(F32)
    cur = lambda col: pl.BlockSpec((1, ts, C), lambda b, i: (b, i, col))
    halo = lambda col: pl.BlockSpec((1, CONV_HALO, C), lambda b, i: (b, jnp.maximum(i * hb - 1, 0), col))
    vec = pl.BlockSpec((1, C), lambda b, i: (0, 0))
    return pl.pallas_call(
        _conv_kernel,
        out_shape=jax.ShapeDtypeStruct((B, S, C), BF16),
        grid=(B, S // ts),
        in_specs=[cur(3), cur(4), halo(3), halo(4),
                  pl.BlockSpec((CONV_WIDTH + 1, C), lambda b, i: (0, 0)), vec, vec, vec],
        out_specs=pl.BlockSpec((1, ts, C), lambda b, i: (b, i, 0)),
        scratch_shapes=[pltpu.VMEM((ts + CONV_HALO, C), F32),
                        pltpu.VMEM((8, CONV_ROWS + CONV_HALO, C // 2), F32)],
        compiler_params=_cparams(("parallel", "parallel")),
        name="conv_module",
    )(proj, proj, proj, proj, jnp.pad(conv_w.astype(F32), ((0, 1), (0, 0))), row(conv_b), row(ln_g), row(ln_b))


GLA_DK = 128
GLA_DV = 256
GLA_Z_COLS = 128


def _split_bf16(x):
    hi = x.astype(BF16)
    lo = (x - hi.astype(F32)).astype(BF16)
    return hi, lo


def _gla_kernel(q_ref, k_ref, v_ref, g_ref, z_ref, wg_ref, bg_ref, gn_ref, o_ref, state_scr):
    ts = q_ref.shape[0]
    n_chunks = ts // CHUNK

    @pl.when(pl.program_id(1) == 0)
    def _():
        state_scr[...] = jnp.zeros(state_scr.shape, F32)

    zz = jnp.dot(z_ref[...], wg_ref[...], preferred_element_type=F32) + bg_ref[...]
    log_a = (jnp.minimum(zz, 0.0) - jnp.log1p(jnp.exp(-jnp.abs(zz)))) / GATE_TAU

    row = lax.broadcasted_iota(jnp.int32, (CHUNK, CHUNK), 0)
    col = lax.broadcasted_iota(jnp.int32, (CHUNK, CHUNK), 1)
    tri = jnp.where(col <= row, 1.0, 0.0).astype(BF16)
    tn = (((0,), (0,)), ((), ()))
    nt = (((1,), (1,)), ((), ()))
    q_scale = GLA_DK ** -0.5

    for c in range(n_chunks):
        rows = slice(c * CHUNK, (c + 1) * CHUNK)
        la_hi, la_lo = _split_bf16(log_a[rows, :])
        cum = (jnp.dot(tri, la_hi, preferred_element_type=F32)
               + jnp.dot(tri, la_lo, preferred_element_type=F32))
        total = cum[CHUNK - 1:CHUNK, :]
        k_dec = (k_ref[rows, :].astype(F32) * jnp.exp(total - cum)).astype(BF16)
        decay = jnp.exp(total)
        for h in range(GLA_HEADS):
            kc = slice(h * GLA_DK, (h + 1) * GLA_DK)
            vc = slice(h * GLA_DV, (h + 1) * GLA_DV)
            vk = lax.dot_general(v_ref[rows, vc], k_dec[:, kc], tn, preferred_element_type=F32)
            state = state_scr[h] * decay[:, kc] + vk
            state_scr[h] = state
            o = lax.dot_general(q_ref[rows, kc], state.astype(BF16), nt,
                                preferred_element_type=F32) * q_scale
            o = o * lax.rsqrt(jnp.mean(o * o, axis=-1, keepdims=True) + EPS) * gn_ref[...]
            gt = g_ref[rows, vc].astype(F32)
            o_ref[rows, vc] = (o * (gt * jax.nn.sigmoid(gt))).astype(o_ref.dtype)


def gla_scan(proj, w_gate, b_gate, norm_g, *, batch, ts=512):
    T = proj.shape[0]
    S = T // batch
    nb = S // ts
    KD = GLA_HEADS * GLA_DK
    VD = GLA_HEADS * GLA_DV
    rows = lambda width, col: pl.BlockSpec((ts, width), lambda b, i: (b * nb + i, col))
    return pl.pallas_call(
        _gla_kernel,
        out_shape=jax.ShapeDtypeStruct((T, VD), BF16),
        grid=(batch, nb),
        in_specs=[
            rows(KD, 0), rows(KD, 1), rows(VD, 1), rows(VD, 2), rows(GLA_Z_COLS, (2 * KD + 2 * VD) // GLA_Z_COLS),
            pl.BlockSpec((GLA_Z_COLS, KD), lambda b, i: (0, 0)),
            pl.BlockSpec((1, KD), lambda b, i: (0, 0)),
            pl.BlockSpec((1, GLA_DV), lambda b, i: (0, 0)),
        ],
        out_specs=pl.BlockSpec((ts, VD), lambda b, i: (b * nb + i, 0)),
        scratch_shapes=[pltpu.VMEM((GLA_HEADS, GLA_DV, GLA_DK), F32)],
        compiler_params=_cparams(("parallel", "arbitrary")),
        name="gla_scan",
    )(proj, proj, proj, proj, proj, w_gate, b_gate.reshape(1, KD).astype(F32),
      norm_g.reshape(1, GLA_DV).astype(F32))


def kernel(x, rel_bias, ln_mix, ln_ffn, even_w_in, lam_q1, lam_k1, lam_q2, lam_k2, subln_g, conv_w, conv_b, conv_ln_g, conv_ln_b, even_w_out, odd_w_in, w_gate2, b_gate2, gla_norm_g, odd_w_out, peer_w_query, peer_sub_keys, peer_u, peer_v, ln_final):
    B, S, D = x.shape
    T = B * S
    xf = x.reshape(T, D)

    lam_init = 0.8 - 0.6 * math.exp(-0.3 * 0)
    lam = (jnp.exp(jnp.sum(lam_q1[0].astype(F32) * lam_k1[0].astype(F32)))
           - jnp.exp(jnp.sum(lam_q2[0].astype(F32) * lam_k2[0].astype(F32))) + lam_init)
    proj = norm_matmul(xf, ln_mix[0], even_w_in[0].astype(BF16)).reshape(B, S, -1)
    a_out = diff_attention(proj, lam, _attn_bias_tiles(rel_bias, ATTN_Q, ATTN_TILE), subln_g[0],
                           out_scale=1.0 - lam_init)
    c_out = conv_module(proj, conv_w[0], conv_b[0], conv_ln_g[0], conv_ln_b[0])
    a_width = A_HEADS * A_V_DIM
    w_out = even_w_out[0].astype(BF16)
    xf = proj_residual(xf, a_out.reshape(T, -1), c_out.reshape(T, -1), w_out[:a_width], w_out[a_width:])
    xf = peer_layer(xf, ln_ffn[0], peer_w_query[0], peer_sub_keys[0], peer_u[0], peer_v[0], ln_final,
                    final_norm=False)

    kd = GLA_HEADS * GLA_DK
    vd = GLA_HEADS * GLA_DV
    w_in = jnp.pad(odd_w_in[0], ((0, 0), (0, GLA_Z_COLS - GATE_RANK))).astype(BF16)
    w_gate = jnp.pad(w_gate2[0], ((0, GLA_Z_COLS - GATE_RANK), (0, 0))).astype(BF16)
    proj = norm_matmul(xf, ln_mix[1], w_in)
    og = gla_scan(proj, w_gate, b_gate2[0], gla_norm_g[0], batch=B)
    w_out = odd_w_out[0].astype(BF16)
    xf = proj_residual(xf, og, og, w_out[:vd // 2], w_out[vd // 2:], a_col=0, c_col=1)
    xf = peer_layer(xf, ln_ffn[1], peer_w_query[1], peer_sub_keys[1], peer_u[1], peer_v[1], ln_final,
                    final_norm=True)
    return xf.reshape(B, S, D)
```

```python
import functools
import math

import jax
import jax.numpy as jnp
import numpy as np
from jax import lax
from jax.experimental import pallas as pl
from jax.experimental.pallas import tpu as pltpu

F32 = jnp.float32
BF16 = jnp.bfloat16

EPS = 1e-6
LANES = 128
VMEM_LIMIT = 56 * 1024 * 1024

CHUNK = 64
N_BUCKETS = 32
MAX_DISTANCE = 128
A_HEADS = 4
A_QK_DIM = 64
A_V_DIM = 128
CONV_WIDTH = 31
GLA_HEADS = 4
GATE_RANK = 16
GATE_TAU = 16.0
PEER_HEADS = 8
N_KEYS = 128
PEER_TOPK = 16
PEER_HALF = 128
NEG_BIG = -1e30


def _cparams(sem):
    return pltpu.CompilerParams(dimension_semantics=sem, vmem_limit_bytes=VMEM_LIMIT)


def _rms_normed(x, g):
    ms = jnp.mean(x * x, axis=-1, keepdims=True)
    return x * lax.rsqrt(ms + EPS) * g


def _norm_matmul_kernel(x_ref, g_ref, w_ref, o_ref):
    xn = _rms_normed(x_ref[...], g_ref[...]).astype(BF16)
    o_ref[...] = jnp.dot(xn, w_ref[...], preferred_element_type=F32).astype(o_ref.dtype)


def norm_matmul(x, g, w, *, tm=512, out_dtype=BF16):
    T, D = x.shape
    N = w.shape[1]
    return pl.pallas_call(
        _norm_matmul_kernel,
        out_shape=jax.ShapeDtypeStruct((T, N), out_dtype),
        grid=(T // tm,),
        in_specs=[
            pl.BlockSpec((tm, D), lambda i: (i, 0)),
            pl.BlockSpec((1, D), lambda i: (0, 0)),
            pl.BlockSpec((D, N), lambda i: (0, 0)),
        ],
        out_specs=pl.BlockSpec((tm, N), lambda i: (i, 0)),
        compiler_params=_cparams(("parallel",)),
        name="norm_matmul",
    )(x, g.reshape(1, D).astype(F32), w)


def _proj_residual_kernel(x_ref, a_ref, c_ref, wa_ref, wc_ref, o_ref):
    acc = jnp.dot(a_ref[...], wa_ref[...], preferred_element_type=F32)
    acc = acc + jnp.dot(c_ref[...], wc_ref[...], preferred_element_type=F32)
    o_ref[...] = x_ref[...] + acc


def proj_residual(x, a, c, wa, wc, *, a_col=0, c_col=0, tm=512):
    T, D = x.shape
    Ka, Kc = wa.shape[0], wc.shape[0]
    return pl.pallas_call(
        _proj_residual_kernel,
        out_shape=jax.ShapeDtypeStruct((T, D), F32),
        grid=(T // tm,),
        in_specs=[
            pl.BlockSpec((tm, D), lambda i: (i, 0)),
            pl.BlockSpec((tm, Ka), lambda i: (i, a_col)),
            pl.BlockSpec((tm, Kc), lambda i: (i, c_col)),
            pl.BlockSpec((Ka, D), lambda i: (0, 0)),
            pl.BlockSpec((Kc, D), lambda i: (0, 0)),
        ],
        out_specs=pl.BlockSpec((tm, D), lambda i: (i, 0)),
        compiler_params=_cparams(("parallel",)),
        name="proj_residual",
    )(x, a, c, wa, wc)


def _pair_candidates():
    return [(a, b) for a in range(PEER_TOPK) for b in range(PEER_TOPK) if (a + 1) * (b + 1) <= PEER_TOPK]


def _dup_bf16_bits(v):
    hi = pltpu.bitcast(v.astype(BF16).astype(F32), jnp.uint32)
    return hi | (hi >> 16)


def _pack_bf16_pairs(lo, hi):
    bl = pltpu.bitcast(lo.astype(BF16).astype(F32), jnp.uint32)
    bh = pltpu.bitcast(hi.astype(BF16).astype(F32), jnp.uint32)
    return (bl >> 16) | bh


def _pair_rows(scr, idx, grp, half):
    return scr[idx, grp, pl.ds(half, scr.shape[2] // 2, stride=2), :]


def _route_kernel(x_ref, g_ref, wqT_ref, sk_ref, rank2_ref, e2_ref, r1_ref, e1z_ref,
                  xn_scr, s_scr, rank_scr, top_scr, r1top_scr, stat_scr):
    tm = x_ref.shape[0]
    n_groups = tm // LANES
    xn_scr[...] = _rms_normed(x_ref[...], g_ref[...]).astype(BF16)

    def score_body(hh, carry):
        for h in (2 * hh, 2 * hh + 1):
            w_h = wqT_ref[pl.ds(pl.multiple_of(h * 2 * PEER_HALF, 2 * PEER_HALF), 2 * PEER_HALF), :]
            qT = lax.dot_general(w_h, xn_scr[...], (((1,), (1,)), ((), ())),
                                 preferred_element_type=F32).astype(BF16)
            for p in range(2):
                s = jnp.dot(sk_ref[2 * h + p], qT[p * PEER_HALF:(p + 1) * PEER_HALF, :],
                            preferred_element_type=F32)
                for grp in range(n_groups):
                    s_scr[2 * h + p, grp] = s[:, grp * LANES:(grp + 1) * LANES]
        return carry

    lax.fori_loop(0, PEER_HEADS // 2, score_body, 0)

    def top_body(hp, carry):
        for grp in range(n_groups):
            work = s_scr[hp, grp]
            rank = jnp.full(work.shape, float(PEER_TOPK), F32)
            for k in range(PEER_TOPK):
                m = jnp.max(work, axis=0, keepdims=True)
                eq = work == m
                rank = jnp.where(eq, float(k), rank)
                work = jnp.where(eq, -jnp.inf, work)
                top_scr[hp, k, grp:grp + 1, :] = m
            rank_scr[hp, grp] = rank
        return carry

    lax.fori_loop(0, 2 * PEER_HEADS, top_body, 0)

    cands = _pair_candidates()

    def pair_body(hq, carry):
        pair_heads(4 * hq)
        pair_heads(4 * hq + 2)
        return carry

    def pair_heads(h):
        def both(scr, idx, k):
            return jnp.concatenate([scr[idx(h), k], scr[idx(h + 1), k]], axis=0)

        v1 = [both(top_scr, lambda x: 2 * x, a) for a in range(PEER_TOPK)]
        v2 = [both(top_scr, lambda x: 2 * x + 1, b) for b in range(PEER_TOPK)]
        sums = [v1[a] + v2[b] for (a, b) in cands]
        work = list(sums)
        tau = None
        for k in range(PEER_TOPK):
            tau = functools.reduce(jnp.maximum, work)
            if k + 1 < PEER_TOPK:
                work = [jnp.where(w == tau, -jnp.inf, w) for w in work]
        cmax = v1[0] + v2[0]
        z = jnp.zeros_like(tau)
        r1 = [jnp.zeros_like(tau) for _ in range(PEER_TOPK)]
        for (a, b), c in zip(cands, sums):
            sel = c >= tau
            z = z + jnp.where(sel, jnp.exp(c - cmax), 0.0)
            r1[a] = r1[a] + jnp.where(sel, 1.0, 0.0)
        inv_z = 1.0 / z
        for d in range(2):
            part = slice(d * n_groups, (d + 1) * n_groups)
            for a in range(PEER_TOPK):
                r1top_scr[h + d, a] = r1[a][part]
            stat_scr[h + d, 0] = v1[0][part]
            stat_scr[h + d, 1] = v2[0][part]
            stat_scr[h + d, 2] = inv_z[part]

    lax.fori_loop(0, PEER_HEADS // 4, pair_body, 0)

    def table_body(h, carry):
        for grp in range(n_groups):
            lanes = slice(grp * LANES, (grp + 1) * LANES)
            max1 = stat_scr[h, 0, grp:grp + 1, :]
            max2 = stat_scr[h, 1, grp:grp + 1, :]
            inv_z = stat_scr[h, 2, grp:grp + 1, :]
            rank1 = rank_scr[2 * h, grp]
            r1_key = jnp.zeros(rank1.shape, F32)
            for a in range(PEER_TOPK):
                r1_key = jnp.where(rank1 == float(a), r1top_scr[h, a, grp:grp + 1, :], r1_key)
            r1_ref[h, :, lanes] = _dup_bf16_bits(r1_key)
            e1z = jnp.exp(s_scr[2 * h, grp] - max1) * inv_z
            e1z_ref[h, :, lanes] = _dup_bf16_bits(e1z)
            rk_lo = _pair_rows(rank_scr, 2 * h + 1, grp, 0)
            rk_hi = _pair_rows(rank_scr, 2 * h + 1, grp, 1)
            rank2_ref[h, :, lanes] = _pack_bf16_pairs(rk_lo, rk_hi)
            e2_lo = jnp.exp(jnp.minimum(_pair_rows(s_scr, 2 * h + 1, grp, 0) - max2, 0.0))
            e2_hi = jnp.exp(jnp.minimum(_pair_rows(s_scr, 2 * h + 1, grp, 1) - max2, 0.0))
            e2_ref[h, :, lanes] = _pack_bf16_pairs(e2_lo, e2_hi)
        return carry

    lax.fori_loop(0, PEER_HEADS, table_body, 0)


def peer_route(x, g, wqT, sk, *, tm=512):
    T, D = x.shape
    H = PEER_HEADS
    n_groups = tm // LANES
    tab = lambda rows: jax.ShapeDtypeStruct((H, rows, T), jnp.uint32)
    tab_spec = lambda rows: pl.BlockSpec((H, rows, tm), lambda i: (0, 0, i))
    return pl.pallas_call(
        _route_kernel,
        out_shape=(tab(N_KEYS // 2), tab(N_KEYS // 2), tab(N_KEYS), tab(N_KEYS)),
        grid=(T // tm,),
        in_specs=[
            pl.BlockSpec((tm, D), lambda i: (i, 0)),
            pl.BlockSpec((1, D), lambda i: (0, 0)),
            pl.BlockSpec((H * 2 * PEER_HALF, D), lambda i: (0, 0)),
            pl.BlockSpec((2 * H, N_KEYS, PEER_HALF), lambda i: (0, 0, 0)),
        ],
        out_specs=(tab_spec(N_KEYS // 2), tab_spec(N_KEYS // 2), tab_spec(N_KEYS), tab_spec(N_KEYS)),
        scratch_shapes=[
            pltpu.VMEM((tm, D), BF16),
            pltpu.VMEM((2 * H, n_groups, N_KEYS, LANES), F32),
            pltpu.VMEM((2 * H, n_groups, N_KEYS, LANES), F32),
            pltpu.VMEM((2 * H, PEER_TOPK, n_groups, LANES), F32),
            pltpu.VMEM((H, PEER_TOPK, n_groups, LANES), F32),
            pltpu.VMEM((H, 3, n_groups, LANES), F32),
        ],
        compiler_params=_cparams(("parallel",)),
        name="peer_route",
    )(x, g.reshape(1, D).astype(F32), wqT, sk)


GELU_C0 = math.sqrt(2.0 / math.pi)
GELU_C1 = 0.044715


def _gelu_tanh(a):
    inner = GELU_C0 * (a + GELU_C1 * (a * a * a))
    return 0.5 * a * (1.0 + jnp.tanh(inner))


PEER_SUB = 512


def _peer_main_kernel(x_ref, g_ref, rank2_ref, e2_ref, r1_ref, e1z_ref, u_ref, vt_ref, gf_ref, o_ref,
                      xn_scr, acc_scr, coef_scr, *, final_norm):
    j = pl.program_id(1)
    tm = x_ref.shape[0]
    n_i1 = r1_ref.shape[1]
    rows = 16
    words = rows * jnp.dtype(BF16).itemsize // 4

    @pl.when(j == 0)
    def _():
        xn_scr[...] = _rms_normed(x_ref[...], g_ref[...]).astype(BF16)
        acc_scr[...] = jnp.zeros_like(acc_scr)

    eb = 2 * u_ref.shape[0]
    d_model = 2 * vt_ref.shape[0]
    tn = 2 * LANES

    def stage_a(n, q):
        r = slice(q * PEER_SUB, (q + 1) * PEER_SUB)
        rw = slice(q * PEER_SUB // 2, (q + 1) * PEER_SUB // 2)
        c = slice(n * tn, (n + 1) * tn)
        return lax.dot_general(pltpu.bitcast(u_ref[rw, :], BF16), xn_scr[c, :],
                               (((1,), (1,)), ((), ())), preferred_element_type=F32)

    def stage_b(n, q, act):
        for sub in range(tn // LANES):
            lanes = slice(n * tn + sub * LANES, n * tn + (sub + 1) * LANES)
            sl = slice(sub * LANES, (sub + 1) * LANES)
            n_blk = N_KEYS // rows
            for i1 in range(q * PEER_SUB // N_KEYS, (q + 1) * PEER_SUB // N_KEYS, 2):
                gate = [[None] * n_blk, [None] * n_blk]
                for h in range(PEER_HEADS):
                    r1b, e1b = [], []
                    for d in range(2):
                        r1row = jnp.broadcast_to(r1_ref[h, i1 + d:i1 + d + 1, lanes], (words, LANES))
                        e1row = jnp.broadcast_to(e1z_ref[h, i1 + d:i1 + d + 1, lanes], (words, LANES))
                        r1b.append(pltpu.bitcast(r1row, BF16))
                        e1b.append(pltpu.bitcast(e1row, BF16))
                    for blk in range(n_blk):
                        i2w = slice(blk * words, (blk + 1) * words)
                        rk = pltpu.bitcast(rank2_ref[h, i2w, lanes], BF16)
                        ev = pltpu.bitcast(e2_ref[h, i2w, lanes], BF16)
                        for d in range(2):
                            term = jnp.where(rk < r1b[d], ev, jnp.zeros((), BF16)) * e1b[d]
                            gate[d][blk] = term if gate[d][blk] is None else gate[d][blk] + term
                for d in range(2):
                    for blk in range(n_blk):
                        r0 = (i1 + d) * N_KEYS + blk * rows
                        ra = r0 - q * PEER_SUB
                        a = act[ra:ra + rows, sl].astype(BF16)
                        coef_scr[n, r0:r0 + rows, sl] = _gelu_tanh(a) * gate[d][blk]

    def stage_c(n):
        for half in range(2):
            r = slice(half * (d_model // 2), (half + 1) * (d_model // 2))
            rw = slice(half * (d_model // 4), (half + 1) * (d_model // 4))
            res = jnp.dot(pltpu.bitcast(vt_ref[rw, :], BF16), coef_scr[n], preferred_element_type=F32)
            for sub in range(tn // LANES):
                acc_scr[n * (tn // LANES) + sub, r, :] += res[:, sub * LANES:(sub + 1) * LANES]

    for n in range(tm // tn):
        for q in range(eb // PEER_SUB):
            stage_b(n, q, stage_a(n, q))
        stage_c(n)

    @pl.when(j == pl.num_programs(1) - 1)
    def _():
        lo_rows = pl.ds(0, d_model // 2, stride=2)
        hi_rows = pl.ds(1, d_model // 2, stride=2)
        peer_out = jnp.concatenate(
            [jnp.concatenate([acc_scr[grp, lo_rows, :], acc_scr[grp, hi_rows, :]], axis=0).T
             for grp in range(tm // LANES)], axis=0)
        y = x_ref[...] + peer_out
        if final_norm:
            y = _rms_normed(y, gf_ref[...])
        o_ref[...] = y


PACK_BLOCK = 512


def _pack_u_kernel(u_ref, o_ref):
    n = o_ref.shape[0]
    o_ref[...] = _pack_bf16_pairs(u_ref[pl.ds(0, n, stride=2), :], u_ref[pl.ds(1, n, stride=2), :])


def _pack_expert_rows(u):
    e, d = u.shape
    rows = 4 * PACK_BLOCK
    return pl.pallas_call(
        _pack_u_kernel,
        out_shape=jax.ShapeDtypeStruct((e // 2, d), jnp.uint32),
        grid=(e // rows, d // LANES),
        in_specs=[pl.BlockSpec((rows, LANES), lambda i, c: (i, c))],
        out_specs=pl.BlockSpec((rows // 2, LANES), lambda i, c: (i, c)),
        compiler_params=_cparams(("parallel", "parallel")),
        name="pack_u",
    )(u)


def _pack_vt_kernel(v_ref, o_ref):
    v = v_ref[...]
    half = v.shape[1] // 2
    o_ref[...] = _pack_bf16_pairs(v[:, :half].T, v[:, half:].T)


def _pack_vt(v):
    e, d = v.shape
    return pl.pallas_call(
        _pack_vt_kernel,
        out_shape=jax.ShapeDtypeStruct((d // 2, e), jnp.uint32),
        grid=(e // PACK_BLOCK,),
        in_specs=[pl.BlockSpec((PACK_BLOCK, d), lambda i: (i, 0))],
        out_specs=pl.BlockSpec((d // 2, PACK_BLOCK), lambda i: (0, i)),
        compiler_params=_cparams(("parallel",)),
        name="pack_vt",
    )(v)


def peer_main(x, g, tables, u, vt, g_final, *, final_norm, tm=512, eb=2048):
    T, D = x.shape
    E = 2 * u.shape[0]
    H = PEER_HEADS
    rank2, e2, r1, e1z = tables
    n_i1 = eb // N_KEYS
    full_tab = pl.BlockSpec((H, N_KEYS // 2, tm), lambda i, j: (0, 0, i))
    i1_tab = pl.BlockSpec((H, n_i1, tm), lambda i, j: (0, j, i))
    return pl.pallas_call(
        functools.partial(_peer_main_kernel, final_norm=final_norm),
        out_shape=jax.ShapeDtypeStruct((T, D), F32),
        grid=(T // tm, E // eb),
        in_specs=[
            pl.BlockSpec((tm, D), lambda i, j: (i, 0)),
            pl.BlockSpec((1, D), lambda i, j: (0, 0)),
            full_tab, full_tab, i1_tab, i1_tab,
            pl.BlockSpec((eb // 2, D), lambda i, j: (j, 0)),
            pl.BlockSpec((D // 2, eb), lambda i, j: (0, j)),
            pl.BlockSpec((1, D), lambda i, j: (0, 0)),
        ],
        out_specs=pl.BlockSpec((tm, D), lambda i, j: (i, 0)),
        scratch_shapes=[
            pltpu.VMEM((tm, D), BF16),
            pltpu.VMEM((tm // LANES, D, LANES), F32),
            pltpu.VMEM((tm // (2 * LANES), eb, 2 * LANES), BF16),
        ],
        compiler_params=_cparams(("parallel", "arbitrary")),
        name="peer_main",
    )(x, g.reshape(1, D).astype(F32), rank2, e2, r1, e1z, u, vt, g_final.reshape(1, D).astype(F32))


def peer_layer(x, g, w_query, sub_keys, expert_u, expert_v, g_final, *, final_norm):
    wqT = w_query.T.astype(BF16)
    sk = sub_keys.reshape(2 * PEER_HEADS, N_KEYS, PEER_HALF).astype(BF16)
    tables = peer_route(x, g, wqT, sk)
    return peer_main(x, g, tables, _pack_expert_rows(expert_u), _pack_vt(expert_v), g_final,
                     final_norm=final_norm)


ATTN_TILE = 256
ATTN_Q = 512
ATTN_GROUP = 1
ATTN_UNROLL = 4
ATTN_ONES = 16


def _attn_kernel(lam_ref, q_ref, k_ref, v_ref, bias_ref, gsub_ref, o_ref, vt_scr, m_scr, acc_scr,
                 s0_scr, s1_scr, p0_scr, p1_scr, *, out_scale):
    i = pl.program_id(2)
    tq = q_ref.shape[1]
    tk = ATTN_TILE
    ratio = tq // tk
    heads = range(q_ref.shape[2] // A_V_DIM)
    cols = lambda g: slice(g * A_V_DIM, (g + 1) * A_V_DIM)

    @pl.when(i == 0)
    def _():
        for g in heads:
            vt_scr[g, :A_V_DIM, :] = v_ref[0, :, cols(g)].T
            vt_scr[g, A_V_DIM:, :] = jnp.ones((ATTN_ONES, vt_scr.shape[2]), BF16)

    scale = jnp.asarray(A_QK_DIM ** -0.5, BF16)
    zero = jnp.zeros((), BF16)
    qs = []
    for g in heads:
        q = q_ref[0, :, cols(g)]
        lane = lax.broadcasted_iota(jnp.int32, q.shape, 1)
        qs.append(jnp.concatenate([jnp.where(lane < A_QK_DIM, q, zero) * scale,
                                   jnp.where(lane >= A_QK_DIM, q, zero) * scale], axis=0))

    m_scr[...] = jnp.full(m_scr.shape, NEG_BIG, F32)
    acc_scr[...] = jnp.zeros(acc_scr.shape, F32)

    def scores(j, g):
        k = k_ref[0, pl.ds(pl.multiple_of(j * tk, tk), tk), cols(g)]
        return lax.dot_general(k, qs[g], (((1,), (1,)), ((), ())), preferred_element_type=F32)

    def pv(j, p_ref, g):
        vt = vt_scr[g, :, pl.ds(pl.multiple_of(j * tk, tk), tk)]
        return jnp.dot(vt, p_ref[g], preferred_element_type=F32)

    def consume(j, bias_idx, cur, nxt, prefetch):
        s_cur, p_cur = cur
        s_nxt, p_prev = nxt
        for g in heads:
            if prefetch:
                s_nxt[g] = scores(j + 1, g)
            pv_prev = pv(jnp.maximum(j - 1, 0), p_prev, g)
            s = s_cur[g]
            if bias_idx is not None:
                bias = bias_ref[g, bias_idx]
                s = s + jnp.concatenate([bias, bias], axis=1)
            m_prev = m_scr[g]
            m_new = jnp.maximum(m_prev, jnp.max(s, axis=0, keepdims=True))
            alpha = jnp.exp(m_prev - m_new)
            p_cur[g] = jnp.exp((s - m_new).astype(BF16))
            acc_scr[g] = alpha * (acc_scr[g] + pv_prev)
            m_scr[g] = m_new

    for g in heads:
        s0_scr[g] = scores(0, g)
    p1_scr[...] = jnp.zeros(p1_scr.shape, BF16)

    n_far = jnp.maximum(ratio * i - 1, 0)
    even_bufs, odd_bufs = (s0_scr, p0_scr), (s1_scr, p1_scr)

    def far_tiles(first, count):
        for t in range(count):
            cur, nxt = (even_bufs, odd_bufs) if t % 2 == 0 else (odd_bufs, even_bufs)
            consume(first + t, None, cur, nxt, True)

    def far_body(jj, carry):
        far_tiles(ATTN_UNROLL * jj, ATTN_UNROLL)
        return carry

    lax.fori_loop(0, n_far // ATTN_UNROLL, far_body, 0)
    rest = n_far % ATTN_UNROLL
    done = n_far - rest

    @pl.when(rest >= 2)
    def _():
        far_tiles(done, 2)

    @pl.when(rest % 2 == 1)
    def _():
        far_tiles(n_far - 1, 1)

    def near_tiles(with_prev):
        if with_prev:
            consume(ratio * i - 1, 0, odd_bufs, even_bufs, True)
        consume(ratio * i, 1, even_bufs, odd_bufs, True)
        consume(ratio * i + 1, 2, odd_bufs, even_bufs, False)
        for g in heads:
            acc_scr[g] += pv(ratio * i + 1, odd_bufs[1], g)

    @pl.when(i >= 1)
    def _():
        near_tiles(True)

    @pl.when(i == 0)
    def _():
        near_tiles(False)

    for g in heads:
        ot = acc_scr[g, :A_V_DIM, :] / acc_scr[g, A_V_DIM:A_V_DIM + 1, :]
        ot = ot[:, :tq] - lam_ref[0] * ot[:, tq:]
        ot = ot * lax.rsqrt(jnp.mean(ot * ot, axis=0, keepdims=True) + EPS)
        o_ref[0, :, cols(g)] = (ot.T * (gsub_ref[...] * out_scale)).astype(o_ref.dtype)


def _t5_bucket(rel):
    nb = N_BUCKETS // 2
    max_exact = nb // 2
    ret = jnp.where(rel > 0, nb, 0)
    n = jnp.abs(rel)
    nf = jnp.maximum(n, 1).astype(F32)
    large = max_exact + (jnp.log(nf / max_exact) / math.log(MAX_DISTANCE / max_exact)
                         * (nb - max_exact)).astype(jnp.int32)
    large = jnp.minimum(large, nb - 1)
    return ret + jnp.where(n < max_exact, n, large)


def _attn_bias_tiles(rel_bias, tq, tk):
    r = jnp.arange(tq)[None, :]
    rb = rel_bias.astype(F32)

    def lookup(bucket):
        return jnp.einsum("...b,bh->...h", jax.nn.one_hot(bucket, N_BUCKETS, dtype=F32), rb,
                          precision=lax.Precision.HIGHEST)

    far = lookup(_t5_bucket(jnp.asarray(-(tq + tk))))
    tiles = []
    for start in (-tk, 0, tk):
        c = jnp.arange(tk)[:, None] + start
        bias = lookup(_t5_bucket(c - r)) - far
        tiles.append(jnp.where((jnp.floor_divide(c, CHUNK) <= r // CHUNK)[..., None], bias, NEG_BIG))
    return jnp.transpose(jnp.stack(tiles, axis=0), (3, 0, 1, 2))


def diff_attention(proj, lam, bias_tiles, subln_g, *, out_scale):
    B, S, _ = proj.shape
    H = A_HEADS
    G = ATTN_GROUP
    tq, tk = ATTN_Q, ATTN_TILE
    width = G * A_V_DIM
    return pl.pallas_call(
        functools.partial(_attn_kernel, out_scale=out_scale),
        out_shape=jax.ShapeDtypeStruct((B, S, H * A_V_DIM), BF16),
        grid=(B, H // G, S // tq),
        in_specs=[
            pl.BlockSpec(memory_space=pltpu.SMEM),
            pl.BlockSpec((1, tq, width), lambda b, h, i: (b, i, h)),
            pl.BlockSpec((1, S, width), lambda b, h, i: (b, 0, H // G + h)),
            pl.BlockSpec((1, S, width), lambda b, h, i: (b, 0, 2 * (H // G) + h)),
            pl.BlockSpec((G, 3, tk, tq), lambda b, h, i: (h, 0, 0, 0)),
            pl.BlockSpec((1, A_V_DIM), lambda b, h, i: (0, 0)),
        ],
        out_specs=pl.BlockSpec((1, tq, width), lambda b, h, i: (b, i, h)),
        scratch_shapes=[
            pltpu.VMEM((G, A_V_DIM + ATTN_ONES, S), BF16),
            pltpu.VMEM((G, 1, 2 * tq), F32),
            pltpu.VMEM((G, A_V_DIM + ATTN_ONES, 2 * tq), F32),
            pltpu.VMEM((G, tk, 2 * tq), F32),
            pltpu.VMEM((G, tk, 2 * tq), F32),
            pltpu.VMEM((G, tk, 2 * tq), BF16),
            pltpu.VMEM((G, tk, 2 * tq), BF16),
        ],
        compiler_params=_cparams(("parallel", "parallel", "arbitrary")),
        name="diff_attention",
    )(lam.reshape(1).astype(F32), proj, proj, proj, bias_tiles, subln_g.reshape(1, A_V_DIM).astype(F32))


CONV_HALO = 32
CONV_ROWS = 64


def _conv_kernel(val_ref, gate_ref, hval_ref, hgate_ref, w_ref, b_ref, g_ref, beta_ref, o_ref, buf_scr,
                 win_scr):
    i = pl.program_id(1)
    ts = val_ref.shape[1]

    def glu(v, gt):
        return v.astype(F32) * jax.nn.sigmoid(gt.astype(F32))

    halo = glu(hval_ref[0], hgate_ref[0])
    buf_scr[0:CONV_HALO, :] = jnp.where(i > 0, halo, 0.0)
    buf_scr[CONV_HALO:, :] = glu(val_ref[0], gate_ref[0])

    first = CONV_HALO - (CONV_WIDTH - 1)
    sub = 8
    n_ch = buf_scr.shape[1]
    for r in range(ts // CONV_ROWS):
        halves = []
        for c0 in range(0, n_ch, n_ch // 2):
            ch = slice(c0, c0 + n_ch // 2)
            acc = None
            for b in range(sub):
                taps = range(b, CONV_WIDTH, sub)
                lo = r * CONV_ROWS + first + b
                rows_b = CONV_ROWS + sub * (len(taps) - 1)
                win_scr[b, 0:rows_b, :] = buf_scr[lo:lo + rows_b, ch]
                for a, w in enumerate(taps):
                    term = win_scr[b, sub * a:sub * a + CONV_ROWS, :] * w_ref[w:w + 1, ch]
                    acc = term if acc is None else acc + term
            halves.append(acc)
        y = jnp.concatenate(halves, axis=1) + b_ref[...]
        mu = jnp.mean(y, axis=-1, keepdims=True)
        d = y - mu
        var = jnp.mean(d * d, axis=-1, keepdims=True)
        y = d * lax.rsqrt(var + EPS) * g_ref[...] + beta_ref[...]
        y = y * jax.nn.sigmoid(y)
        o_ref[0, r * CONV_ROWS:(r + 1) * CONV_ROWS, :] = y.astype(o_ref.dtype)


def conv_module(proj, conv_w, conv_b, ln_g, ln_b, *, ts=512):
    B, S, _ = proj.shape
    C = conv_w.shape[1]
    hb = ts // CONV_HALO
    row = lambda a: a.reshape(1, C).astype(F32)
    cur = lambda col: pl.BlockSpec((1, ts, C), lambda b, i: (b, i, col))
    halo = lambda col: pl.BlockSpec((1, CONV_HALO, C), lambda b, i: (b, jnp.maximum(i * hb - 1, 0), col))
    vec = pl.BlockSpec((1, C), lambda b, i: (0, 0))
    return pl.pallas_call(
        _conv_kernel,
        out_shape=jax.ShapeDtypeStruct((B, S, C), BF16),
        grid=(B, S // ts),
        in_specs=[cur(3), cur(4), halo(3), halo(4),
                  pl.BlockSpec((CONV_WIDTH + 1, C), lambda b, i: (0, 0)), vec, vec, vec],
        out_specs=pl.BlockSpec((1, ts, C), lambda b, i: (b, i, 0)),
        scratch_shapes=[pltpu.VMEM((ts + CONV_HALO, C), F32),
                        pltpu.VMEM((8, CONV_ROWS + CONV_HALO, C // 2), F32)],
        compiler_params=_cparams(("parallel", "parallel")),
        name="conv_module",
    )(proj, proj, proj, proj, jnp.pad(conv_w.astype(F32), ((0, 1), (0, 0))), row(conv_b), row(ln_g), row(ln_b))


GLA_DK = 128
GLA_DV = 256
GLA_Z_COLS = 128


def _split_bf16(x):
    hi = x.astype(BF16)
    lo = (x - hi.astype(F32)).astype(BF16)
    return hi, lo


def _gla_kernel(q_ref, k_ref, v_ref, g_ref, z_ref, wg_ref, bg_ref, gn_ref, o_ref, state_scr):
    ts = q_ref.shape[0]
    n_chunks = ts // CHUNK

    @pl.when(pl.program_id(1) == 0)
    def _():
        state_scr[...] = jnp.zeros(state_scr.shape, F32)

    zz = jnp.dot(z_ref[...], wg_ref[...], preferred_element_type=F32) + bg_ref[...]
    log_a = (jnp.minimum(zz, 0.0) - jnp.log1p(jnp.exp(-jnp.abs(zz)))) / GATE_TAU

    row = lax.broadcasted_iota(jnp.int32, (CHUNK, CHUNK), 0)
    col = lax.broadcasted_iota(jnp.int32, (CHUNK, CHUNK), 1)
    tri = jnp.where(col <= row, 1.0, 0.0).astype(BF16)
    tn = (((0,), (0,)), ((), ()))
    nt = (((1,), (1,)), ((), ()))
    q_scale = GLA_DK ** -0.5

    for c in range(n_chunks):
        rows = slice(c * CHUNK, (c + 1) * CHUNK)
        la_hi, la_lo = _split_bf16(log_a[rows, :])
        cum = (jnp.dot(tri, la_hi, preferred_element_type=F32)
               + jnp.dot(tri, la_lo, preferred_element_type=F32))
        total = cum[CHUNK - 1:CHUNK, :]
        k_dec = (k_ref[rows, :].astype(F32) * jnp.exp(total - cum)).astype(BF16)
        decay = jnp.exp(total)
        for h in range(GLA_HEADS):
            kc = slice(h * GLA_DK, (h + 1) * GLA_DK)
            vc = slice(h * GLA_DV, (h + 1) * GLA_DV)
            vk = lax.dot_general(v_ref[rows, vc], k_dec[:, kc], tn, preferred_element_type=F32)
            state = state_scr[h] * decay[:, kc] + vk
            state_scr[h] = state
            o = lax.dot_general(q_ref[rows, kc], state.astype(BF16), nt,
                                preferred_element_type=F32) * q_scale
            o = o * lax.rsqrt(jnp.mean(o * o, axis=-1, keepdims=True) + EPS) * gn_ref[...]
            gt = g_ref[rows, vc].astype(F32)
            o_ref[rows, vc] = (o * (gt * jax.nn.sigmoid(gt))).astype(o_ref.dtype)


def gla_scan(proj, w_gate, b_gate, norm_g, *, batch, ts=512):
    T = proj.shape[0]
    S = T // batch
    nb = S // ts
    KD = GLA_HEADS * GLA_DK
    VD = GLA_HEADS * GLA_DV
    rows = lambda width, col: pl.BlockSpec((ts, width), lambda b, i: (b * nb + i, col))
    return pl.pallas_call(
        _gla_kernel,
        out_shape=jax.ShapeDtypeStruct((T, VD), BF16),
        grid=(batch, nb),
        in_specs=[
            rows(KD, 0), rows(KD, 1), rows(VD, 1), rows(VD, 2), rows(GLA_Z_COLS, (2 * KD + 2 * VD) // GLA_Z_COLS),
            pl.BlockSpec((GLA_Z_COLS, KD), lambda b, i: (0, 0)),
            pl.BlockSpec((1, KD), lambda b, i: (0, 0)),
            pl.BlockSpec((1, GLA_DV), lambda b, i: (0, 0)),
        ],
        out_specs=pl.BlockSpec((ts, VD), lambda b, i: (b * nb + i, 0)),
        scratch_shapes=[pltpu.VMEM((GLA_HEADS, GLA_DV, GLA_DK), F32)],
        compiler_params=_cparams(("parallel", "arbitrary")),
        name="gla_scan",
    )(proj, proj, proj, proj, proj, w_gate, b_gate.reshape(1, KD).astype(F32),
      norm_g.reshape(1, GLA_DV).astype(F32))


def kernel(x, rel_bias, ln_mix, ln_ffn, even_w_in, lam_q1, lam_k1, lam_q2, lam_k2, subln_g, conv_w, conv_b, conv_ln_g, conv_ln_b, even_w_out, odd_w_in, w_gate2, b_gate2, gla_norm_g, odd_w_out, peer_w_query, peer_sub_keys, peer_u, peer_v, ln_final):
    B, S, D = x.shape
    T = B * S
    xf = x.reshape(T, D)

    lam_init = 0.8 - 0.6 * math.exp(-0.3 * 0)
    lam = (jnp.exp(jnp.sum(lam_q1[0].astype(F32) * lam_k1[0].astype(F32)))
           - jnp.exp(jnp.sum(lam_q2[0].astype(F32) * lam_k2[0].astype(F32))) + lam_init)
    proj = norm_matmul(xf, ln_mix[0], even_w_in[0].astype(BF16)).reshape(B, S, -1)
    a_out = diff_attention(proj, lam, _attn_bias_tiles(rel_bias, ATTN_Q, ATTN_TILE), subln_g[0],
                           out_scale=1.0 - lam_init)
    c_out = conv_module(proj, conv_w[0], conv_b[0], conv_ln_g[0], conv_ln_b[0])
    a_width = A_HEADS * A_V_DIM
    w_out = even_w_out[0].astype(BF16)
    xf = proj_residual(xf, a_out.reshape(T, -1), c_out.reshape(T, -1), w_out[:a_width], w_out[a_width:])
    xf = peer_layer(xf, ln_ffn[0], peer_w_query[0], peer_sub_keys[0], peer_u[0], peer_v[0], ln_final,
                    final_norm=False)

    kd = GLA_HEADS * GLA_DK
    vd = GLA_HEADS * GLA_DV
    w_in = jnp.pad(odd_w_in[0], ((0, 0), (0, GLA_Z_COLS - GATE_RANK))).astype(BF16)
    w_gate = jnp.pad(w_gate2[0], ((0, GLA_Z_COLS - GATE_RANK), (0, 0))).astype(BF16)
    proj = norm_matmul(xf, ln_mix[1], w_in)
    og = gla_scan(proj, w_gate, b_gate2[0], gla_norm_g[0], batch=B)
    w_out = odd_w_out[0].astype(BF16)
    xf = proj_residual(xf, og, og, w_out[:vd // 2], w_out[vd // 2:], a_col=0, c_col=1)
    xf = peer_layer(xf, ln_ffn[1], peer_w_query[1], peer_sub_keys[1], peer_u[1], peer_v[1], ln_final,
                    final_norm=True)
    return xf.reshape(B, S, D)
```

```python
import functools
import math

import jax
import jax.numpy as jnp
import numpy as np
from jax import lax
from jax.experimental import pallas as pl
from jax.experimental.pallas import tpu as pltpu

F32 = jnp.float32
BF16 = jnp.bfloat16

EPS = 1e-6
LANES = 128
VMEM_LIMIT = 56 * 1024 * 1024

CHUNK = 64
N_BUCKETS = 32
MAX_DISTANCE = 128
A_HEADS = 4
A_QK_DIM = 64
A_V_DIM = 128
CONV_WIDTH = 31
GLA_HEADS = 4
GATE_RANK = 16
GATE_TAU = 16.0
PEER_HEADS = 8
N_KEYS = 128
PEER_TOPK = 16
PEER_HALF = 128
NEG_BIG = -1e30


def _cparams(sem):
    return pltpu.CompilerParams(dimension_semantics=sem, vmem_limit_bytes=VMEM_LIMIT)


def _rms_normed(x, g):
    ms = jnp.mean(x * x, axis=-1, keepdims=True)
    return x * lax.rsqrt(ms + EPS) * g


def _norm_matmul_kernel(x_ref, g_ref, w_ref, o_ref):
    xn = _rms_normed(x_ref[...], g_ref[...]).astype(BF16)
    o_ref[...] = jnp.dot(xn, w_ref[...], preferred_element_type=F32).astype(o_ref.dtype)


def norm_matmul(x, g, w, *, tm=512, out_dtype=BF16):
    T, D = x.shape
    N = w.shape[1]
    return pl.pallas_call(
        _norm_matmul_kernel,
        out_shape=jax.ShapeDtypeStruct((T, N), out_dtype),
        grid=(T // tm,),
        in_specs=[
            pl.BlockSpec((tm, D), lambda i: (i, 0)),
            pl.BlockSpec((1, D), lambda i: (0, 0)),
            pl.BlockSpec((D, N), lambda i: (0, 0)),
        ],
        out_specs=pl.BlockSpec((tm, N), lambda i: (i, 0)),
        compiler_params=_cparams(("parallel",)),
        name="norm_matmul",
    )(x, g.reshape(1, D).astype(F32), w)


def _proj_residual_kernel(x_ref, a_ref, c_ref, wa_ref, wc_ref, o_ref):
    acc = jnp.dot(a_ref[...], wa_ref[...], preferred_element_type=F32)
    acc = acc + jnp.dot(c_ref[...], wc_ref[...], preferred_element_type=F32)
    o_ref[...] = x_ref[...] + acc


def proj_residual(x, a, c, wa, wc, *, a_col=0, c_col=0, tm=512):
    T, D = x.shape
    Ka, Kc = wa.shape[0], wc.shape[0]
    return pl.pallas_call(
        _proj_residual_kernel,
        out_shape=jax.ShapeDtypeStruct((T, D), F32),
        grid=(T // tm,),
        in_specs=[
            pl.BlockSpec((tm, D), lambda i: (i, 0)),
            pl.BlockSpec((tm, Ka), lambda i: (i, a_col)),
            pl.BlockSpec((tm, Kc), lambda i: (i, c_col)),
            pl.BlockSpec((Ka, D), lambda i: (0, 0)),
            pl.BlockSpec((Kc, D), lambda i: (0, 0)),
        ],
        out_specs=pl.BlockSpec((tm, D), lambda i: (i, 0)),
        compiler_params=_cparams(("parallel",)),
        name="proj_residual",
    )(x, a, c, wa, wc)


def _pair_candidates():
    return [(a, b) for a in range(PEER_TOPK) for b in range(PEER_TOPK) if (a + 1) * (b + 1) <= PEER_TOPK]


def _dup_bf16_bits(v):
    hi = pltpu.bitcast(v.astype(BF16).astype(F32), jnp.uint32)
    return hi | (hi >> 16)


def _pack_bf16_pairs(lo, hi):
    bl = pltpu.bitcast(lo.astype(BF16).astype(F32), jnp.uint32)
    bh = pltpu.bitcast(hi.astype(BF16).astype(F32), jnp.uint32)
    return (bl >> 16) | bh


def _pair_rows(scr, idx, grp, half):
    return scr[idx, grp, pl.ds(half, scr.shape[2] // 2, stride=2), :]


def _route_kernel(x_ref, g_ref, wqT_ref, sk_ref, rank2_ref, e2_ref, r1_ref, e1z_ref,
                  xn_scr, s_scr, rank_scr, top_scr, r1top_scr, stat_scr):
    tm = x_ref.shape[0]
    n_groups = tm // LANES
    xn_scr[...] = _rms_normed(x_ref[...], g_ref[...]).astype(BF16)

    def score_body(hh, carry):
        for h in (2 * hh, 2 * hh + 1):
            w_h = wqT_ref[pl.ds(pl.multiple_of(h * 2 * PEER_HALF, 2 * PEER_HALF), 2 * PEER_HALF), :]
            qT = lax.dot_general(w_h, xn_scr[...], (((1,), (1,)), ((), ())),
                                 preferred_element_type=F32).astype(BF16)
            for p in range(2):
                s = jnp.dot(sk_ref[2 * h + p], qT[p * PEER_HALF:(p + 1) * PEER_HALF, :],
                            preferred_element_type=F32)
                for grp in range(n_groups):
                    s_scr[2 * h + p, grp] = s[:, grp * LANES:(grp + 1) * LANES]
        return carry

    lax.fori_loop(0, PEER_HEADS // 2, score_body, 0)

    def top_body(hp, carry):
        for grp in range(n_groups):
            work = s_scr[hp, grp]
            rank = jnp.full(work.shape, float(PEER_TOPK), F32)
            for k in range(PEER_TOPK):
                m = jnp.max(work, axis=0, keepdims=True)
                eq = work == m
                rank = jnp.where(eq, float(k), rank)
                work = jnp.where(eq, -jnp.inf, work)
                top_scr[hp, k, grp:grp + 1, :] = m
            rank_scr[hp, grp] = rank
        return carry

    lax.fori_loop(0, 2 * PEER_HEADS, top_body, 0)

    cands = _pair_candidates()

    def pair_body(hq, carry):
        pair_heads(4 * hq)
        pair_heads(4 * hq + 2)
        return carry

    def pair_heads(h):
        def both(scr, idx, k):
            return jnp.concatenate([scr[idx(h), k], scr[idx(h + 1), k]], axis=0)

        v1 = [both(top_scr, lambda x: 2 * x, a) for a in range(PEER_TOPK)]
        v2 = [both(top_scr, lambda x: 2 * x + 1, b) for b in range(PEER_TOPK)]
        sums = [v1[a] + v2[b] for (a, b) in cands]
        work = list(sums)
        tau = None
        for k in range(PEER_TOPK):
            tau = functools.reduce(jnp.maximum, work)
            if k + 1 < PEER_TOPK:
                work = [jnp.where(w == tau, -jnp.inf, w) for w in work]
        cmax = v1[0] + v2[0]
        z = jnp.zeros_like(tau)
        r1 = [jnp.zeros_like(tau) for _ in range(PEER_TOPK)]
        for (a, b), c in zip(cands, sums):
            sel = c >= tau
            z = z + jnp.where(sel, jnp.exp(c - cmax), 0.0)
            r1[a] = r1[a] + jnp.where(sel, 1.0, 0.0)
        inv_z = 1.0 / z
        for d in range(2):
            part = slice(d * n_groups, (d + 1) * n_groups)
            for a in range(PEER_TOPK):
                r1top_scr[h + d, a] = r1[a][part]
            stat_scr[h + d, 0] = v1[0][part]
            stat_scr[h + d, 1] = v2[0][part]
            stat_scr[h + d, 2] = inv_z[part]

    lax.fori_loop(0, PEER_HEADS // 4, pair_body, 0)

    def table_body(h, carry):
        for grp in range(n_groups):
            lanes = slice(grp * LANES, (grp + 1) * LANES)
            max1 = stat_scr[h, 0, grp:grp + 1, :]
            max2 = stat_scr[h, 1, grp:grp + 1, :]
            inv_z = stat_scr[h, 2, grp:grp + 1, :]
            rank1 = rank_scr[2 * h, grp]
            r1_key = jnp.zeros(rank1.shape, F32)
            for a in range(PEER_TOPK):
                r1_key = jnp.where(rank1 == float(a), r1top_scr[h, a, grp:grp + 1, :], r1_key)
            r1_ref[h, :, lanes] = _dup_bf16_bits(r1_key)
            e1z = jnp.exp(s_scr[2 * h, grp] - max1) * inv_z
            e1z_ref[h, :, lanes] = _dup_bf16_bits(e1z)
            rk_lo = _pair_rows(rank_scr, 2 * h + 1, grp, 0)
            rk_hi = _pair_rows(rank_scr, 2 * h + 1, grp, 1)
            rank2_ref[h, :, lanes] = _pack_bf16_pairs(rk_lo, rk_hi)
            e2_lo = jnp.exp(jnp.minimum(_pair_rows(s_scr, 2 * h + 1, grp, 0) - max2, 0.0))
            e2_hi = jnp.exp(jnp.minimum(_pair_rows(s_scr, 2 * h + 1, grp, 1) - max2, 0.0))
            e2_ref[h, :, lanes] = _pack_bf16_pairs(e2_lo, e2_hi)
        return carry

    lax.fori_loop(0, PEER_HEADS, table_body, 0)


def peer_route(x, g, wqT, sk, *, tm=512):
    T, D = x.shape
    H = PEER_HEADS
    n_groups = tm // LANES
    tab = lambda rows: jax.ShapeDtypeStruct((H, rows, T), jnp.uint32)
    tab_spec = lambda rows: pl.BlockSpec((H, rows, tm), lambda i: (0, 0, i))
    return pl.pallas_call(
        _route_kernel,
        out_shape=(tab(N_KEYS // 2), tab(N_KEYS // 2), tab(N_KEYS), tab(N_KEYS)),
        grid=(T // tm,),
        in_specs=[
            pl.BlockSpec((tm, D), lambda i: (i, 0)),
            pl.BlockSpec((1, D), lambda i: (0, 0)),
            pl.BlockSpec((H * 2 * PEER_HALF, D), lambda i: (0, 0)),
            pl.BlockSpec((2 * H, N_KEYS, PEER_HALF), lambda i: (0, 0, 0)),
        ],
        out_specs=(tab_spec(N_KEYS // 2), tab_spec(N_KEYS // 2), tab_spec(N_KEYS), tab_spec(N_KEYS)),
        scratch_shapes=[
            pltpu.VMEM((tm, D), BF16),
            pltpu.VMEM((2 * H, n_groups, N_KEYS, LANES), F32),
            pltpu.VMEM((2 * H, n_groups, N_KEYS, LANES), F32),
            pltpu.VMEM((2 * H, PEER_TOPK, n_groups, LANES), F32),
            pltpu.VMEM((H, PEER_TOPK, n_groups, LANES), F32),
            pltpu.VMEM((H, 3, n_groups, LANES), F32),
        ],
        compiler_params=_cparams(("parallel",)),
        name="peer_route",
    )(x, g.reshape(1, D).astype(F32), wqT, sk)


GELU_C0 = math.sqrt(2.0 / math.pi)
GELU_C1 = 0.044715


def _gelu_tanh(a):
    inner = GELU_C0 * (a + GELU_C1 * (a * a * a))
    return 0.5 * a * (1.0 + jnp.tanh(inner))


PEER_SUB = 512


def _peer_main_kernel(x_ref, g_ref, rank2_ref, e2_ref, r1_ref, e1z_ref, u_ref, vt_ref, gf_ref, o_ref,
                      xn_scr, acc_scr, coef_scr, *, final_norm):
    j = pl.program_id(1)
    tm = x_ref.shape[0]
    n_i1 = r1_ref.shape[1]
    rows = 16
    words = rows * jnp.dtype(BF16).itemsize // 4

    @pl.when(j == 0)
    def _():
        xn_scr[...] = _rms_normed(x_ref[...], g_ref[...]).astype(BF16)
        acc_scr[...] = jnp.zeros_like(acc_scr)

    eb = 2 * u_ref.shape[0]
    d_model = 2 * vt_ref.shape[0]
    tn = 2 * LANES

    def stage_a(n, q):
        r = slice(q * PEER_SUB, (q + 1) * PEER_SUB)
        rw = slice(q * PEER_SUB // 2, (q + 1) * PEER_SUB // 2)
        c = slice(n * tn, (n + 1) * tn)
        return lax.dot_general(pltpu.bitcast(u_ref[rw, :], BF16), xn_scr[c, :],
                               (((1,), (1,)), ((), ())), preferred_element_type=F32)

    def stage_b(n, q, act):
        for sub in range(tn // LANES):
            lanes = slice(n * tn + sub * LANES, n * tn + (sub + 1) * LANES)
            sl = slice(sub * LANES, (sub + 1) * LANES)
            n_blk = N_KEYS // rows
            for i1 in range(q * PEER_SUB // N_KEYS, (q + 1) * PEER_SUB // N_KEYS, 2):
                gate = [[None] * n_blk, [None] * n_blk]
                for h in range(PEER_HEADS):
                    r1b, e1b = [], []
                    for d in range(2):
                        r1row = jnp.broadcast_to(r1_ref[h, i1 + d:i1 + d + 1, lanes], (words, LANES))
                        e1row = jnp.broadcast_to(e1z_ref[h, i1 + d:i1 + d + 1, lanes], (words, LANES))
                        r1b.append(pltpu.bitcast(r1row, BF16))
                        e1b.append(pltpu.bitcast(e1row, BF16))
                    for blk in range(n_blk):
                        i2w = slice(blk * words, (blk + 1) * words)
                        rk = pltpu.bitcast(rank2_ref[h, i2w, lanes], BF16)
                        ev = pltpu.bitcast(e2_ref[h, i2w, lanes], BF16)
                        for d in range(2):
                            term = jnp.where(rk < r1b[d], ev, jnp.zeros((), BF16)) * e1b[d]
                            gate[d][blk] = term if gate[d][blk] is None else gate[d][blk] + term
                for d in range(2):
                    for blk in range(n_blk):
                        r0 = (i1 + d) * N_KEYS + blk * rows
                        ra = r0 - q * PEER_SUB
                        a = act[ra:ra + rows, sl].astype(BF16)
                        coef_scr[n, r0:r0 + rows, sl] = _gelu_tanh(a) * gate[d][blk]

    def stage_c(n):
        for half in range(2):
            r = slice(half * (d_model // 2), (half + 1) * (d_model // 2))
            rw = slice(half * (d_model // 4), (half + 1) * (d_model // 4))
            res = jnp.dot(pltpu.bitcast(vt_ref[rw, :], BF16), coef_scr[n], preferred_element_type=F32)
            for sub in range(tn // LANES):
                acc_scr[n * (tn // LANES) + sub, r, :] += res[:, sub * LANES:(sub + 1) * LANES]

    for n in range(tm // tn):
        for q in range(eb // PEER_SUB):
            stage_b(n, q, stage_a(n, q))
        stage_c(n)

    @pl.when(j == pl.num_programs(1) - 1)
    def _():
        lo_rows = pl.ds(0, d_model // 2, stride=2)
        hi_rows = pl.ds(1, d_model // 2, stride=2)
        peer_out = jnp.concatenate(
            [jnp.concatenate([acc_scr[grp, lo_rows, :], acc_scr[grp, hi_rows, :]], axis=0).T
             for grp in range(tm // LANES)], axis=0)
        y = x_ref[...] + peer_out
        if final_norm:
            y = _rms_normed(y, gf_ref[...])
        o_ref[...] = y


PACK_BLOCK = 512


def _pack_u_kernel(u_ref, o_ref):
    n = o_ref.shape[0]
    o_ref[...] = _pack_bf16_pairs(u_ref[pl.ds(0, n, stride=2), :], u_ref[pl.ds(1, n, stride=2), :])


def _pack_expert_rows(u_all, layer):
    _, e, d = u_all.shape
    rows = 4 * PACK_BLOCK
    return pl.pallas_call(
        _pack_u_kernel,
        out_shape=jax.ShapeDtypeStruct((e // 2, d), jnp.uint32),
        grid=(e // rows, d // LANES),
        in_specs=[pl.BlockSpec((None, rows, LANES), lambda i, c: (layer, i, c))],
        out_specs=pl.BlockSpec((rows // 2, LANES), lambda i, c: (i, c)),
        compiler_params=_cparams(("parallel", "parallel")),
        name="pack_u",
    )(u_all)


def _pack_vt_kernel(v_ref, o_ref):
    v = v_ref[...]
    half = v.shape[1] // 2
    o_ref[...] = _pack_bf16_pairs(v[:, :half].T, v[:, half:].T)


def _pack_vt(v_all, layer):
    _, e, d = v_all.shape
    return pl.pallas_call(
        _pack_vt_kernel,
        out_shape=jax.ShapeDtypeStruct((d // 2, e), jnp.uint32),
        grid=(e // PACK_BLOCK,),
        in_specs=[pl.BlockSpec((None, PACK_BLOCK, d), lambda i: (layer, i, 0))],
        out_specs=pl.BlockSpec((d // 2, PACK_BLOCK), lambda i: (0, i)),
        compiler_params=_cparams(("parallel",)),
        name="pack_vt",
    )(v_all)


def peer_main(x, g, tables, u, vt, g_final, *, final_norm, tm=512, eb=2048):
    T, D = x.shape
    E = 2 * u.shape[0]
    H = PEER_HEADS
    rank2, e2, r1, e1z = tables
    n_i1 = eb // N_KEYS
    full_tab = pl.BlockSpec((H, N_KEYS // 2, tm), lambda i, j: (0, 0, i))
    i1_tab = pl.BlockSpec((H, n_i1, tm), lambda i, j: (0, j, i))
    return pl.pallas_call(
        functools.partial(_peer_main_kernel, final_norm=final_norm),
        out_shape=jax.ShapeDtypeStruct((T, D), F32),
        grid=(T // tm, E // eb),
        in_specs=[
            pl.BlockSpec((tm, D), lambda i, j: (i, 0)),
            pl.BlockSpec((1, D), lambda i, j: (0, 0)),
            full_tab, full_tab, i1_tab, i1_tab,
            pl.BlockSpec((eb // 2, D), lambda i, j: (j, 0)),
            pl.BlockSpec((D // 2, eb), lambda i, j: (0, j)),
            pl.BlockSpec((1, D), lambda i, j: (0, 0)),
        ],
        out_specs=pl.BlockSpec((tm, D), lambda i, j: (i, 0)),
        scratch_shapes=[
            pltpu.VMEM((tm, D), BF16),
            pltpu.VMEM((tm // LANES, D, LANES), F32),
            pltpu.VMEM((tm // (2 * LANES), eb, 2 * LANES), BF16),
        ],
        compiler_params=_cparams(("parallel", "arbitrary")),
        name="peer_main",
    )(x, g.reshape(1, D).astype(F32), rank2, e2, r1, e1z, u, vt, g_final.reshape(1, D).astype(F32))


def peer_layer(x, g, w_query, sub_keys, expert_u, expert_v, layer, g_final, *, final_norm):
    wqT = w_query.T.astype(BF16)
    sk = sub_keys.reshape(2 * PEER_HEADS, N_KEYS, PEER_HALF).astype(BF16)
    tables = peer_route(x, g, wqT, sk)
    return peer_main(x, g, tables, _pack_expert_rows(expert_u, layer), _pack_vt(expert_v, layer), g_final,
                     final_norm=final_norm)


ATTN_TILE = 256
ATTN_Q = 512
ATTN_GROUP = 1
ATTN_UNROLL = 4
ATTN_ONES = 16


def _attn_kernel(lam_ref, q_ref, k_ref, v_ref, bias_ref, gsub_ref, o_ref, vt_scr, m_scr, acc_scr,
                 s0_scr, s1_scr, p0_scr, p1_scr, *, out_scale):
    i = pl.program_id(2)
    tq = q_ref.shape[1]
    tk = ATTN_TILE
    ratio = tq // tk
    heads = range(q_ref.shape[2] // A_V_DIM)
    cols = lambda g: slice(g * A_V_DIM, (g + 1) * A_V_DIM)

    @pl.when(i == 0)
    def _():
        for g in heads:
            vt_scr[g, :A_V_DIM, :] = v_ref[0, :, cols(g)].T
            vt_scr[g, A_V_DIM:, :] = jnp.ones((ATTN_ONES, vt_scr.shape[2]), BF16)

    scale = jnp.asarray(A_QK_DIM ** -0.5, BF16)
    zero = jnp.zeros((), BF16)
    qs = []
    for g in heads:
        q = q_ref[0, :, cols(g)]
        lane = lax.broadcasted_iota(jnp.int32, q.shape, 1)
        qs.append(jnp.concatenate([jnp.where(lane < A_QK_DIM, q, zero) * scale,
                                   jnp.where(lane >= A_QK_DIM, q, zero) * scale], axis=0))

    m_scr[...] = jnp.full(m_scr.shape, NEG_BIG, F32)
    acc_scr[...] = jnp.zeros(acc_scr.shape, F32)

    def scores(j, g):
        k = k_ref[0, pl.ds(pl.multiple_of(j * tk, tk), tk), cols(g)]
        return lax.dot_general(k, qs[g], (((1,), (1,)), ((), ())), preferred_element_type=F32)

    def pv(j, p_ref, g):
        vt = vt_scr[g, :, pl.ds(pl.multiple_of(j * tk, tk), tk)]
        return jnp.dot(vt, p_ref[g], preferred_element_type=F32)

    def consume(j, bias_idx, cur, nxt, prefetch):
        s_cur, p_cur = cur
        s_nxt, p_prev = nxt
        for g in heads:
            if prefetch:
                s_nxt[g] = scores(j + 1, g)
            pv_prev = pv(jnp.maximum(j - 1, 0), p_prev, g)
            s = s_cur[g]
            if bias_idx is not None:
                bias = bias_ref[g, bias_idx]
                s = s + jnp.concatenate([bias, bias], axis=1)
            m_prev = m_scr[g]
            m_new = jnp.maximum(m_prev, jnp.max(s, axis=0, keepdims=True))
            alpha = jnp.exp(m_prev - m_new)
            p_cur[g] = jnp.exp((s - m_new).astype(BF16))
            acc_scr[g] = alpha * (acc_scr[g] + pv_prev)
            m_scr[g] = m_new

    for g in heads:
        s0_scr[g] = scores(0, g)
    p1_scr[...] = jnp.zeros(p1_scr.shape, BF16)

    n_far = jnp.maximum(ratio * i - 1, 0)
    even_bufs, odd_bufs = (s0_scr, p0_scr), (s1_scr, p1_scr)

    def far_tiles(first, count):
        for t in range(count):
            cur, nxt = (even_bufs, odd_bufs) if t % 2 == 0 else (odd_bufs, even_bufs)
            consume(first + t, None, cur, nxt, True)

    def far_body(jj, carry):
        far_tiles(ATTN_UNROLL * jj, ATTN_UNROLL)
        return carry

    lax.fori_loop(0, n_far // ATTN_UNROLL, far_body, 0)
    rest = n_far % ATTN_UNROLL
    done = n_far - rest

    @pl.when(rest >= 2)
    def _():
        far_tiles(done, 2)

    @pl.when(rest % 2 == 1)
    def _():
        far_tiles(n_far - 1, 1)

    def near_tiles(with_prev):
        if with_prev:
            consume(ratio * i - 1, 0, odd_bufs, even_bufs, True)
        consume(ratio * i, 1, even_bufs, odd_bufs, True)
        consume(ratio * i + 1, 2, odd_bufs, even_bufs, False)
        for g in heads:
            acc_scr[g] += pv(ratio * i + 1, odd_bufs[1], g)

    @pl.when(i >= 1)
    def _():
        near_tiles(True)

    @pl.when(i == 0)
    def _():
        near_tiles(False)

    for g in heads:
        ot = acc_scr[g, :A_V_DIM, :] / acc_scr[g, A_V_DIM:A_V_DIM + 1, :]
        ot = ot[:, :tq] - lam_ref[0] * ot[:, tq:]
        ot = ot * lax.rsqrt(jnp.mean(ot * ot, axis=0, keepdims=True) + EPS)
        o_ref[0, :, cols(g)] = (ot.T * (gsub_ref[...] * out_scale)).astype(o_ref.dtype)


def _t5_bucket(rel):
    nb = N_BUCKETS // 2
    max_exact = nb // 2
    ret = jnp.where(rel > 0, nb, 0)
    n = jnp.abs(rel)
    nf = jnp.maximum(n, 1).astype(F32)
    large = max_exact + (jnp.log(nf / max_exact) / math.log(MAX_DISTANCE / max_exact)
                         * (nb - max_exact)).astype(jnp.int32)
    large = jnp.minimum(large, nb - 1)
    return ret + jnp.where(n < max_exact, n, large)


def _attn_bias_tiles(rel_bias, tq, tk):
    r = jnp.arange(tq)[None, :]
    rb = rel_bias.astype(F32)

    def lookup(bucket):
        return jnp.einsum("...b,bh->...h", jax.nn.one_hot(bucket, N_BUCKETS, dtype=F32), rb,
                          precision=lax.Precision.HIGHEST)

    far = lookup(_t5_bucket(jnp.asarray(-(tq + tk))))
    tiles = []
    for start in (-tk, 0, tk):
        c = jnp.arange(tk)[:, None] + start
        bias = lookup(_t5_bucket(c - r)) - far
        tiles.append(jnp.where((jnp.floor_divide(c, CHUNK) <= r // CHUNK)[..., None], bias, NEG_BIG))
    return jnp.transpose(jnp.stack(tiles, axis=0), (3, 0, 1, 2))


def diff_attention(proj, lam, bias_tiles, subln_g, *, out_scale):
    B, S, _ = proj.shape
    H = A_HEADS
    G = ATTN_GROUP
    tq, tk = ATTN_Q, ATTN_TILE
    width = G * A_V_DIM
    return pl.pallas_call(
        functools.partial(_attn_kernel, out_scale=out_scale),
        out_shape=jax.ShapeDtypeStruct((B, S, H * A_V_DIM), BF16),
        grid=(B, H // G, S // tq),
        in_specs=[
            pl.BlockSpec(memory_space=pltpu.SMEM),
            pl.BlockSpec((1, tq, width), lambda b, h, i: (b, i, h)),
            pl.BlockSpec((1, S, width), lambda b, h, i: (b, 0, H // G + h)),
            pl.BlockSpec((1, S, width), lambda b, h, i: (b, 0, 2 * (H // G) + h)),
            pl.BlockSpec((G, 3, tk, tq), lambda b, h, i: (h, 0, 0, 0)),
            pl.BlockSpec((1, A_V_DIM), lambda b, h, i: (0, 0)),
        ],
        out_specs=pl.BlockSpec((1, tq, width), lambda b, h, i: (b, i, h)),
        scratch_shapes=[
            pltpu.VMEM((G, A_V_DIM + ATTN_ONES, S), BF16),
            pltpu.VMEM((G, 1, 2 * tq), F32),
            pltpu.VMEM((G, A_V_DIM + ATTN_ONES, 2 * tq), F32),
            pltpu.VMEM((G, tk, 2 * tq), F32),
            pltpu.VMEM((G, tk, 2 * tq), F32),
            pltpu.VMEM((G, tk, 2 * tq), BF16),
            pltpu.VMEM((G, tk, 2 * tq), BF16),
        ],
        compiler_params=_cparams(("parallel", "parallel", "arbitrary")),
        name="diff_attention",
    )(lam.reshape(1).astype(F32), proj, proj, proj, bias_tiles, subln_g.reshape(1, A_V_DIM).astype(F32))


CONV_HALO = 32
CONV_ROWS = 64


def _conv_kernel(val_ref, gate_ref, hval_ref, hgate_ref, w_ref, b_ref, g_ref, beta_ref, o_ref, buf_scr,
                 win_scr):
    i = pl.program_id(1)
    ts = val_ref.shape[1]

    def glu(v, gt):
        return v.astype(F32) * jax.nn.sigmoid(gt.astype(F32))

    halo = glu(hval_ref[0], hgate_ref[0])
    buf_scr[0:CONV_HALO, :] = jnp.where(i > 0, halo, 0.0)
    buf_scr[CONV_HALO:, :] = glu(val_ref[0], gate_ref[0])

    first = CONV_HALO - (CONV_WIDTH - 1)
    sub = 8
    n_ch = buf_scr.shape[1]
    for r in range(ts // CONV_ROWS):
        halves = []
        for c0 in range(0, n_ch, n_ch // 2):
            ch = slice(c0, c0 + n_ch // 2)
            acc = None
            for b in range(sub):
                taps = range(b, CONV_WIDTH, sub)
                lo = r * CONV_ROWS + first + b
                rows_b = CONV_ROWS + sub * (len(taps) - 1)
                win_scr[b, 0:rows_b, :] = buf_scr[lo:lo + rows_b, ch]
                for a, w in enumerate(taps):
                    term = win_scr[b, sub * a:sub * a + CONV_ROWS, :] * w_ref[w:w + 1, ch]
                    acc = term if acc is None else acc + term
            halves.append(acc)
        y = jnp.concatenate(halves, axis=1) + b_ref[...]
        mu = jnp.mean(y, axis=-1, keepdims=True)
        d = y - mu
        var = jnp.mean(d * d, axis=-1, keepdims=True)
        y = d * lax.rsqrt(var + EPS) * g_ref[...] + beta_ref[...]
        y = y * jax.nn.sigmoid(y)
        o_ref[0, r * CONV_ROWS:(r + 1) * CONV_ROWS, :] = y.astype(o_ref.dtype)


def conv_module(proj, conv_w, conv_b, ln_g, ln_b, *, ts=512):
    B, S, _ = proj.shape
    C = conv_w.shape[1]
    hb = ts // CONV_HALO
    row = lambda a: a.reshape(1, C).astype(F32)
    cur = lambda col: pl.BlockSpec((1, ts, C), lambda b, i: (b, i, col))
    halo = lambda col: pl.BlockSpec((1, CONV_HALO, C), lambda b, i: (b, jnp.maximum(i * hb - 1, 0), col))
    vec = pl.BlockSpec((1, C), lambda b, i: (0, 0))
    return pl.pallas_call(
        _conv_kernel,
        out_shape=jax.ShapeDtypeStruct((B, S, C), BF16),
        grid=(B, S // ts),
        in_specs=[cur(3), cur(4), halo(3), halo(4),
                  pl.BlockSpec((CONV_WIDTH + 1, C), lambda b, i: (0, 0)), vec, vec, vec],
        out_specs=pl.BlockSpec((1, ts, C), lambda b, i: (b, i, 0)),
        scratch_shapes=[pltpu.VMEM((ts + CONV_HALO, C), F32),
                        pltpu.VMEM((8, CONV_ROWS + CONV_HALO, C // 2), F32)],
        compiler_params=_cparams(("parallel", "parallel")),
        name="conv_module",
    )(proj, proj, proj, proj, jnp.pad(conv_w.astype(F32), ((0, 1), (0, 0))), row(conv_b), row(ln_g), row(ln_b))


GLA_DK = 128
GLA_DV = 256
GLA_Z_COLS = 128


def _split_bf16(x):
    hi = x.astype(BF16)
    lo = (x - hi.astype(F32)).astype(BF16)
    return hi, lo


def _gla_kernel(q_ref, k_ref, v_ref, g_ref, z_ref, wg_ref, bg_ref, gn_ref, o_ref, state_scr):
    ts = q_ref.shape[0]
    n_chunks = ts // CHUNK

    @pl.when(pl.program_id(1) == 0)
    def _():
        state_scr[...] = jnp.zeros(state_scr.shape, F32)

    zz = jnp.dot(z_ref[...], wg_ref[...], preferred_element_type=F32) + bg_ref[...]
    log_a = (jnp.minimum(zz, 0.0) - jnp.log1p(jnp.exp(-jnp.abs(zz)))) / GATE_TAU

    row = lax.broadcasted_iota(jnp.int32, (CHUNK, CHUNK), 0)
    col = lax.broadcasted_iota(jnp.int32, (CHUNK, CHUNK), 1)
    tri = jnp.where(col <= row, 1.0, 0.0).astype(BF16)
    tn = (((0,), (0,)), ((), ()))
    nt = (((1,), (1,)), ((), ()))
    q_scale = GLA_DK ** -0.5

    for c in range(n_chunks):
        rows = slice(c * CHUNK, (c + 1) * CHUNK)
        la_hi, la_lo = _split_bf16(log_a[rows, :])
        cum = (jnp.dot(tri, la_hi, preferred_element_type=F32)
               + jnp.dot(tri, la_lo, preferred_element_type=F32))
        total = cum[CHUNK - 1:CHUNK, :]
        k_dec = (k_ref[rows, :].astype(F32) * jnp.exp(total - cum)).astype(BF16)
        decay = jnp.exp(total)
        for h in range(GLA_HEADS):
            kc = slice(h * GLA_DK, (h + 1) * GLA_DK)
            vc = slice(h * GLA_DV, (h + 1) * GLA_DV)
            vk = lax.dot_general(v_ref[rows, vc], k_dec[:, kc], tn, preferred_element_type=F32)
            state = state_scr[h] * decay[:, kc] + vk
            state_scr[h] = state
            o = lax.dot_general(q_ref[rows, kc], state.astype(BF16), nt,
                                preferred_element_type=F32) * q_scale
            o = o * lax.rsqrt(jnp.mean(o * o, axis=-1, keepdims=True) + EPS) * gn_ref[...]
            gt = g_ref[rows, vc].astype(F32)
            o_ref[rows, vc] = (o * (gt * jax.nn.sigmoid(gt))).astype(o_ref.dtype)


def gla_scan(proj, w_gate, b_gate, norm_g, *, batch, ts=512):
    T = proj.shape[0]
    S = T // batch
    nb = S // ts
    KD = GLA_HEADS * GLA_DK
    VD = GLA_HEADS * GLA_DV
    rows = lambda width, col: pl.BlockSpec((ts, width), lambda b, i: (b * nb + i, col))
    return pl.pallas_call(
        _gla_kernel,
        out_shape=jax.ShapeDtypeStruct((T, VD), BF16),
        grid=(batch, nb),
        in_specs=[
            rows(KD, 0), rows(KD, 1), rows(VD, 1), rows(VD, 2), rows(GLA_Z_COLS, (2 * KD + 2 * VD) // GLA_Z_COLS),
            pl.BlockSpec((GLA_Z_COLS, KD), lambda b, i: (0, 0)),
            pl.BlockSpec((1, KD), lambda b, i: (0, 0)),
            pl.BlockSpec((1, GLA_DV), lambda b, i: (0, 0)),
        ],
        out_specs=pl.BlockSpec((ts, VD), lambda b, i: (b * nb + i, 0)),
        scratch_shapes=[pltpu.VMEM((GLA_HEADS, GLA_DV, GLA_DK), F32)],
        compiler_params=_cparams(("parallel", "arbitrary")),
        name="gla_scan",
    )(proj, proj, proj, proj, proj, w_gate, b_gate.reshape(1, KD).astype(F32),
      norm_g.reshape(1, GLA_DV).astype(F32))


def kernel(x, rel_bias, ln_mix, ln_ffn, even_w_in, lam_q1, lam_k1, lam_q2, lam_k2, subln_g, conv_w, conv_b, conv_ln_g, conv_ln_b, even_w_out, odd_w_in, w_gate2, b_gate2, gla_norm_g, odd_w_out, peer_w_query, peer_sub_keys, peer_u, peer_v, ln_final):
    B, S, D = x.shape
    T = B * S
    xf = x.reshape(T, D)

    lam_init = 0.8 - 0.6 * math.exp(-0.3 * 0)
    lam = (jnp.exp(jnp.sum(lam_q1[0].astype(F32) * lam_k1[0].astype(F32)))
           - jnp.exp(jnp.sum(lam_q2[0].astype(F32) * lam_k2[0].astype(F32))) + lam_init)
    proj = norm_matmul(xf, ln_mix[0], even_w_in[0].astype(BF16)).reshape(B, S, -1)
    a_out = diff_attention(proj, lam, _attn_bias_tiles(rel_bias, ATTN_Q, ATTN_TILE), subln_g[0],
                           out_scale=1.0 - lam_init)
    c_out = conv_module(proj, conv_w[0], conv_b[0], conv_ln_g[0], conv_ln_b[0])
    a_width = A_HEADS * A_V_DIM
    w_out = even_w_out[0].astype(BF16)
    xf = proj_residual(xf, a_out.reshape(T, -1), c_out.reshape(T, -1), w_out[:a_width], w_out[a_width:])
    xf = peer_layer(xf, ln_ffn[0], peer_w_query[0], peer_sub_keys[0], peer_u, peer_v, 0, ln_final,
                    final_norm=False)

    kd = GLA_HEADS * GLA_DK
    vd = GLA_HEADS * GLA_DV
    w_in = jnp.pad(odd_w_in[0], ((0, 0), (0, GLA_Z_COLS - GATE_RANK))).astype(BF16)
    w_gate = jnp.pad(w_gate2[0], ((0, GLA_Z_COLS - GATE_RANK), (0, 0))).astype(BF16)
    proj = norm_matmul(xf, ln_mix[1], w_in)
    og = gla_scan(proj, w_gate, b_gate2[0], gla_norm_g[0], batch=B)
    w_out = odd_w_out[0].astype(BF16)
    xf = proj_residual(xf, og, og, w_out[:vd // 2], w_out[vd // 2:], a_col=0, c_col=1)
    xf = peer_layer(xf, ln_ffn[1], peer_w_query[1], peer_sub_keys[1], peer_u, peer_v, 1, ln_final,
                    final_norm=True)
    return xf.reshape(B, S, D)
```
